```python
import jax, jax.numpy as jnp
from jax import lax
import numpy as np

D_MODEL = 1024
BATCH = 8
SEQ = 2048
DEPTH = 1
DEC_BATCH = 128
DEC_SEQ = 8
PAST_LEN = 16384
PAGE_SIZE = 128

MIX_W = D_MODEL
GROUP_W = MIX_W // 2
N_HEADS_A = 8
N_HEADS_B = 8
HEAD_DIM = GROUP_W // N_HEADS_A
K_A = 3
K_B = 31
IN_COLS = 5 * GROUP_W
D_FF = ((8 * D_MODEL // 3 + 255) // 256) * 256
EPS = 1e-6

kernel_name = "hybrid_shortconv_conformer_decode_step"


def rmsnorm(x, g):
    xf = x.astype(jnp.float32)
    r = xf * lax.rsqrt(jnp.mean(xf * xf, axis=-1, keepdims=True) + EPS)
    return (r * g.astype(jnp.float32)).astype(x.dtype)


def layernorm(x, g, b):
    xf = x.astype(jnp.float32)
    mu = jnp.mean(xf, axis=-1, keepdims=True)
    xc = xf - mu
    var = jnp.mean(xc * xc, axis=-1, keepdims=True)
    r = xc * lax.rsqrt(var + EPS) * g.astype(jnp.float32) + b.astype(jnp.float32)
    return r.astype(x.dtype)


def causal_dwconv(u, hist, w):
    k, c = w.shape
    up = jnp.concatenate([hist.astype(u.dtype), u], axis=1)
    y = lax.conv_general_dilated(up, w.astype(u.dtype)[:, None, :], window_strides=(1,),
                                 padding='VALID', dimension_numbers=('NWC', 'WIO', 'NWC'),
                                 feature_group_count=c)
    return y, up[:, up.shape[1] - (k - 1):]


def layer(x, hist_a, hist_b, g_mix, w_in, conv_a_w, conv_b_w, conv_b_bias, ln_b_g, ln_b_b,
          w_out, g_ffn, w_gate, w_up, w_down):
    h = rmsnorm(x, g_mix)
    z = jnp.einsum('btd,dc->btc', h, w_in)
    b_gate, c_gate, v, glu_val, glu_gate = jnp.split(z, 5, axis=-1)
    conv_a, new_a = causal_dwconv(c_gate * v, hist_a, conv_a_w)
    y_a = b_gate * conv_a
    u_b = glu_val * jax.nn.sigmoid(glu_gate)
    conv_b, new_b = causal_dwconv(u_b, hist_b, conv_b_w)
    y_b = jax.nn.silu(layernorm(conv_b + conv_b_bias.astype(conv_b.dtype), ln_b_g, ln_b_b))
    x = x + jnp.einsum('btc,cd->btd', jnp.concatenate([y_a, y_b], axis=-1), w_out)
    h2 = rmsnorm(x, g_ffn)
    f = jax.nn.silu(jnp.einsum('btd,df->btf', h2, w_gate)) * jnp.einsum('btd,df->btf', h2, w_up)
    x = x + jnp.einsum('btf,fd->btd', f, w_down)
    return x, new_a, new_b


def setup_inputs(seed: int = 0) -> dict:
    key = jax.random.key(seed)
    ks = jax.random.split(key, 20)
    f32 = jnp.float32
    nrm = lambda k, s, sc: (jax.random.normal(k, s, f32) * sc).astype(f32)
    return {
        "x_prompt": nrm(ks[0], (BATCH, SEQ, D_MODEL), 1.0),
        "x_sample": nrm(ks[1], (DEC_BATCH, DEC_SEQ, D_MODEL), 1.0),
        "state_conv_a": nrm(ks[2], (DEPTH, DEC_BATCH, K_A - 1, GROUP_W), 1.0),
        "state_conv_b": nrm(ks[3], (DEPTH, DEC_BATCH, K_B - 1, GROUP_W), 1.0),
        "g_mix": 1.0 + nrm(ks[4], (DEPTH, D_MODEL), 0.1),
        "w_in": nrm(ks[5], (DEPTH, D_MODEL, IN_COLS), D_MODEL ** -0.5),
        "conv_a_w": nrm(ks[6], (DEPTH, K_A, GROUP_W), K_A ** -0.5),
        "conv_b_w": nrm(ks[7], (DEPTH, K_B, GROUP_W), K_B ** -0.5),
        "conv_b_bias": nrm(ks[8], (DEPTH, GROUP_W), 0.02),
        "ln_b_g": 1.0 + nrm(ks[9], (DEPTH, GROUP_W), 0.1),
        "ln_b_b": nrm(ks[10], (DEPTH, GROUP_W), 0.02),
        "w_out": nrm(ks[11], (DEPTH, MIX_W, D_MODEL), MIX_W ** -0.5),
        "g_ffn": 1.0 + nrm(ks[12], (DEPTH, D_MODEL), 0.1),
        "w_gate": nrm(ks[13], (DEPTH, D_MODEL, D_FF), D_MODEL ** -0.5),
        "w_up": nrm(ks[14], (DEPTH, D_MODEL, D_FF), D_MODEL ** -0.5),
        "w_down": nrm(ks[15], (DEPTH, D_FF, D_MODEL), D_FF ** -0.5),
        "g_final": 1.0 + nrm(ks[16], (D_MODEL,), 0.1),
    }


def reference(x_prompt, x_sample, state_conv_a, state_conv_b, g_mix, w_in, conv_a_w, conv_b_w,
              conv_b_bias, ln_b_g, ln_b_b, w_out, g_ffn, w_gate, w_up, w_down, g_final):
    xp, xs = x_prompt, x_sample
    bp = x_prompt.shape[0]
    pa, pb, sa, sb = [], [], [], []
    for l in range(DEPTH):
        params = (g_mix[l], w_in[l], conv_a_w[l], conv_b_w[l], conv_b_bias[l], ln_b_g[l], ln_b_b[l],
                  w_out[l], g_ffn[l], w_gate[l], w_up[l], w_down[l])
        za = jnp.zeros((bp, K_A - 1, GROUP_W), xp.dtype)
        zb = jnp.zeros((bp, K_B - 1, GROUP_W), xp.dtype)
        xp, npa, npb = layer(xp, za, zb, *params)
        xs, nsa, nsb = layer(xs, state_conv_a[l], state_conv_b[l], *params)
        pa.append(npa); pb.append(npb); sa.append(nsa); sb.append(nsb)
    y_prompt = rmsnorm(xp, g_final)
    y_sample = rmsnorm(xs, g_final)
    new_conv_a_prompt = jnp.stack(pa, axis=0)
    new_conv_b_prompt = jnp.stack(pb, axis=0)
    new_conv_a_sample = jnp.stack(sa, axis=0)
    new_conv_b_sample = jnp.stack(sb, axis=0)
    return (y_prompt, y_sample, new_conv_a_prompt, new_conv_b_prompt, new_conv_a_sample, new_conv_b_sample)
```

```python
import functools

import jax
import jax.numpy as jnp
from jax import lax
from jax.experimental import pallas as pl
from jax.experimental.pallas import tpu as pltpu

EPS = 1e-6
K_A = 3
K_B = 31
LANES = 128
SUBLANES = 8

F32 = jnp.float32
BF16 = jnp.bfloat16


def _rmsnorm(x, g):
    ms = jnp.mean(x * x, axis=-1, keepdims=True)
    return x * lax.rsqrt(ms + EPS) * g


def _layernorm(x, g, b):
    mu = jnp.mean(x, axis=-1, keepdims=True)
    xc = x - mu
    var = jnp.mean(xc * xc, axis=-1, keepdims=True)
    return xc * lax.rsqrt(var + EPS) * g + b


def _sigmoid(x):
    return 1.0 / (1.0 + jnp.exp(-x))


def _silu(x):
    return x * _sigmoid(x)


def _dot(a, b):
    return jnp.dot(a, b, preferred_element_type=F32)


def _conv_windows(ext_ref, j, starts, nrows, w_ref, n_taps):
    lanes = slice(j * LANES, (j + 1) * LANES)
    accs = [None] * len(starts)
    for k in range(n_taps):
        wk = w_ref[k:k + 1, lanes]
        for i, s in enumerate(starts):
            term = wk * ext_ref[j, pl.ds(s + k, nrows), :]
            accs[i] = term if accs[i] is None else accs[i] + term
    return accs


def _in_proj(x, g_mix_ref, w_in_ref, gw):
    h = _rmsnorm(x, g_mix_ref[...]).astype(BF16)
    return [_dot(h, w_in_ref[:, g * gw:(g + 1) * gw]) for g in range(5)]


def _out_ffn(x, y_a, conv_b, bias_ref, ln_g_ref, ln_b_ref, w_out_ref, g_ffn_ref,
             w_gate_ref, w_up_ref, w_down_ref, g_final_ref):
    y_b = _silu(_layernorm(conv_b + bias_ref[...], ln_g_ref[...], ln_b_ref[...]))
    y_cat = jnp.concatenate([y_a, y_b], axis=-1).astype(BF16)
    x1 = x + _dot(y_cat, w_out_ref[...])
    h2 = _rmsnorm(x1, g_ffn_ref[...]).astype(BF16)
    gate = _dot(h2, w_gate_ref[...])
    up = _dot(h2, w_up_ref[...])
    f = (_silu(gate) * up).astype(BF16)
    x2 = x1 + _dot(f, w_down_ref[...])
    return _rmsnorm(x2, g_final_ref[...])


def _prompt_kernel(x_ref, g_mix_ref, w_in_ref, wa_ref, wb_ref, bias_ref, ln_g_ref, ln_b_ref,
                   w_out_ref, g_ffn_ref, w_gate_ref, w_up_ref, w_down_ref, g_final_ref,
                   y_ref, new_a_ref, new_b_ref, exta_ref, extb_ref, *, tile, row_chunk):
    gw = wa_ref.shape[1]
    n_lane = gw // LANES
    hist_a = exta_ref.shape[1] - tile
    hist_b = extb_ref.shape[1] - tile

    @pl.when(pl.program_id(1) == 0)
    def _():
        exta_ref[:, 0:hist_a, :] = jnp.zeros((n_lane, hist_a, LANES), F32)
        extb_ref[:, 0:hist_b, :] = jnp.zeros((n_lane, hist_b, LANES), F32)

    x = x_ref[...]
    b_gate, c_gate, v, glu_val, glu_gate = _in_proj(x, g_mix_ref, w_in_ref, gw)
    cv = c_gate * v
    u_b = glu_val * _sigmoid(glu_gate)
    for j in range(n_lane):
        lanes = slice(j * LANES, (j + 1) * LANES)
        exta_ref[j, hist_a:hist_a + tile, :] = cv[:, lanes]
        extb_ref[j, hist_b:hist_b + tile, :] = u_b[:, lanes]

    starts = list(range(0, tile, row_chunk))
    conv_a, conv_b = [], []
    for j in range(n_lane):
        a = _conv_windows(exta_ref, j, [hist_a - (K_A - 1) + s for s in starts], row_chunk,
                          wa_ref, K_A)
        b = _conv_windows(extb_ref, j, [hist_b - (K_B - 1) + s for s in starts], row_chunk,
                          wb_ref, K_B)
        conv_a.append(jnp.concatenate(a, axis=0))
        conv_b.append(jnp.concatenate(b, axis=0))
    y_a = b_gate * jnp.concatenate(conv_a, axis=1)
    conv_b = jnp.concatenate(conv_b, axis=1)

    y_ref[...] = _out_ffn(x, y_a, conv_b, bias_ref, ln_g_ref, ln_b_ref, w_out_ref, g_ffn_ref,
                          w_gate_ref, w_up_ref, w_down_ref, g_final_ref)

    new_a_ref[...] = cv[tile - (K_A - 1):, :]
    new_b_ref[...] = u_b[tile - (K_B - 1):, :]
    exta_ref[:, 0:hist_a, :] = exta_ref[:, tile:tile + hist_a, :]
    extb_ref[:, 0:hist_b, :] = extb_ref[:, tile:tile + hist_b, :]


def _sample_kernel(x_ref, sa_ref, sb_ref, g_mix_ref, w_in_ref, wa_ref, wb_ref, bias_ref,
                   ln_g_ref, ln_b_ref, w_out_ref, g_ffn_ref, w_gate_ref, w_up_ref, w_down_ref,
                   g_final_ref, y_ref, new_a_ref, new_b_ref, exta_ref, extb_ref,
                   *, n_seq, seq_len, seq_group):
    gw = wa_ref.shape[1]
    n_lane = gw // LANES
    ra = exta_ref.shape[1] // n_seq
    rb = extb_ref.shape[1] // n_seq
    off_a = ra - seq_len - (K_A - 1)
    off_b = rb - seq_len - (K_B - 1)

    x = x_ref[...]
    b_gate, c_gate, v, glu_val, glu_gate = _in_proj(x, g_mix_ref, w_in_ref, gw)
    cv = c_gate * v
    u_b = glu_val * _sigmoid(glu_gate)
    for s in range(n_seq):
        rows = slice(s * seq_len, (s + 1) * seq_len)
        for j in range(n_lane):
            lanes = slice(j * LANES, (j + 1) * LANES)
            exta_ref[j, s * ra + off_a:s * ra + off_a + K_A - 1, :] = sa_ref[s, :, lanes]
            exta_ref[j, (s + 1) * ra - seq_len:(s + 1) * ra, :] = cv[rows, lanes]
            extb_ref[j, s * rb + off_b:s * rb + off_b + K_B - 1, :] = sb_ref[s, :, lanes]
            extb_ref[j, (s + 1) * rb - seq_len:(s + 1) * rb, :] = u_b[rows, lanes]

    conv_a, conv_b = [], []
    for j in range(n_lane):
        a, b = [], []
        for s0 in range(0, n_seq, seq_group):
            group = range(s0, s0 + seq_group)
            a += _conv_windows(exta_ref, j, [s * ra + off_a for s in group], seq_len, wa_ref, K_A)
            b += _conv_windows(extb_ref, j, [s * rb + off_b for s in group], seq_len, wb_ref, K_B)
        conv_a.append(jnp.concatenate(a, axis=0))
        conv_b.append(jnp.concatenate(b, axis=0))
    y_a = b_gate * jnp.concatenate(conv_a, axis=1)
    conv_b = jnp.concatenate(conv_b, axis=1)

    y_ref[...] = _out_ffn(x, y_a, conv_b, bias_ref, ln_g_ref, ln_b_ref, w_out_ref, g_ffn_ref,
                          w_gate_ref, w_up_ref, w_down_ref, g_final_ref)

    for s in range(n_seq):
        for j in range(n_lane):
            lanes = slice(j * LANES, (j + 1) * LANES)
            new_a_ref[s, :, lanes] = exta_ref[j, (s + 1) * ra - (K_A - 1):(s + 1) * ra, :]
            new_b_ref[s, :, lanes] = extb_ref[j, (s + 1) * rb - (K_B - 1):(s + 1) * rb, :]


def _round_up(n, m):
    return -(-n // m) * m


def _const_spec(shape, grid_rank):
    zeros = (0,) * len(shape)
    if grid_rank == 1:
        index_map = lambda i: zeros
    else:
        index_map = lambda b, j: zeros
    return pl.BlockSpec(shape, index_map, pipeline_mode=pl.Buffered(1))


PROMPT_TILE = 256
PROMPT_ROW_CHUNK = 64
SAMPLE_SEQS = 32
SAMPLE_SEQ_GROUP = 16
VMEM_LIMIT_BYTES = 56 * 1024 * 1024


def kernel(x_prompt, x_sample, state_conv_a, state_conv_b, g_mix, w_in, conv_a_w, conv_b_w,
           conv_b_bias, ln_b_g, ln_b_b, w_out, g_ffn, w_gate, w_up, w_down, g_final):
    depth = g_mix.shape[0]
    assert depth == 1, "single-layer trunk"
    bp, seq, d = x_prompt.shape
    bs, dec_seq, _ = x_sample.shape
    gw = conv_a_w.shape[-1]
    assert conv_a_w.shape[1] == K_A and conv_b_w.shape[1] == K_B
    assert dec_seq == SUBLANES and gw % LANES == 0
    assert seq % PROMPT_TILE == 0 and bs % SAMPLE_SEQS == 0

    row = lambda a: a.reshape(1, -1)
    params = (row(g_mix[0]), w_in[0].astype(BF16), conv_a_w[0], conv_b_w[0], row(conv_b_bias[0]),
              row(ln_b_g[0]), row(ln_b_b[0]), w_out[0].astype(BF16), row(g_ffn[0]),
              w_gate[0].astype(BF16), w_up[0].astype(BF16), w_down[0].astype(BF16), row(g_final))

    tile = PROMPT_TILE
    hist_a = _round_up(K_A - 1, SUBLANES)
    hist_b = _round_up(K_B - 1, SUBLANES)
    n_lane = gw // LANES
    y_prompt, pa, pb = pl.pallas_call(
        functools.partial(_prompt_kernel, tile=tile, row_chunk=PROMPT_ROW_CHUNK),
        grid=(bp, seq // tile),
        in_specs=[pl.BlockSpec((None, tile, d), lambda b, j: (b, j, 0))]
        + [_const_spec(p.shape, 2) for p in params],
        out_specs=[pl.BlockSpec((None, tile, d), lambda b, j: (b, j, 0)),
                   pl.BlockSpec((None, K_A - 1, gw), lambda b, j: (b, 0, 0)),
                   pl.BlockSpec((None, K_B - 1, gw), lambda b, j: (b, 0, 0))],
        out_shape=[jax.ShapeDtypeStruct((bp, seq, d), F32),
                   jax.ShapeDtypeStruct((bp, K_A - 1, gw), F32),
                   jax.ShapeDtypeStruct((bp, K_B - 1, gw), F32)],
        scratch_shapes=[pltpu.VMEM((n_lane, hist_a + tile, LANES), F32),
                        pltpu.VMEM((n_lane, hist_b + tile, LANES), F32)],
        compiler_params=pltpu.CompilerParams(
            dimension_semantics=("arbitrary", "arbitrary"),
            vmem_limit_bytes=VMEM_LIMIT_BYTES),
        name="prompt_layer",
    )(x_prompt, *params)

    n_seq = SAMPLE_SEQS
    rows = n_seq * dec_seq
    ra = _round_up(K_A - 1, SUBLANES) + dec_seq
    rb = _round_up(K_B - 1, SUBLANES) + dec_seq
    y_sample, sa, sb = pl.pallas_call(
        functools.partial(_sample_kernel, n_seq=n_seq, seq_len=dec_seq,
                          seq_group=SAMPLE_SEQ_GROUP),
        grid=(bs // n_seq,),
        in_specs=[pl.BlockSpec((rows, d), lambda i: (i, 0)),
                  pl.BlockSpec((n_seq, K_A - 1, gw), lambda i: (i, 0, 0)),
                  pl.BlockSpec((n_seq, K_B - 1, gw), lambda i: (i, 0, 0))]
        + [_const_spec(p.shape, 1) for p in params],
        out_specs=[pl.BlockSpec((rows, d), lambda i: (i, 0)),
                   pl.BlockSpec((n_seq, K_A - 1, gw), lambda i: (i, 0, 0)),
                   pl.BlockSpec((n_seq, K_B - 1, gw), lambda i: (i, 0, 0))],
        out_shape=[jax.ShapeDtypeStruct((bs * dec_seq, d), F32),
                   jax.ShapeDtypeStruct((bs, K_A - 1, gw), F32),
                   jax.ShapeDtypeStruct((bs, K_B - 1, gw), F32)],
        scratch_shapes=[pltpu.VMEM((n_lane, n_seq * ra, LANES), F32),
                        pltpu.VMEM((n_lane, n_seq * rb, LANES), F32)],
        compiler_params=pltpu.CompilerParams(
            dimension_semantics=("arbitrary",),
            vmem_limit_bytes=VMEM_LIMIT_BYTES),
        name="sample_layer",
    )(x_sample.reshape(bs * dec_seq, d), state_conv_a[0], state_conv_b[0], *params)

    return (y_prompt, y_sample.reshape(bs, dec_seq, d), pa[None], pb[None], sa[None], sb[None])
```

```python
import functools

import jax
import jax.numpy as jnp
from jax import lax
from jax.experimental import pallas as pl
from jax.experimental.pallas import tpu as pltpu

EPS = 1e-6
K_A = 3
K_B = 31
LANES = 128
SUBLANES = 8

F32 = jnp.float32
BF16 = jnp.bfloat16


def _rmsnorm(x, g):
    ms = jnp.mean(x * x, axis=-1, keepdims=True)
    return x * lax.rsqrt(ms + EPS) * g


def _layernorm(x, g, b):
    mu = jnp.mean(x, axis=-1, keepdims=True)
    xc = x - mu
    var = jnp.mean(xc * xc, axis=-1, keepdims=True)
    return xc * lax.rsqrt(var + EPS) * g + b


def _sigmoid(x):
    return 1.0 / (1.0 + jnp.exp(-x))


def _silu(x):
    return x * _sigmoid(x)


def _dot(a, b):
    return jnp.dot(a, b, preferred_element_type=F32)


def _conv_windows(ext_ref, j, starts, nrows, w_ref, n_taps):
    lanes = slice(j * LANES, (j + 1) * LANES)
    accs = [None] * len(starts)
    for k in range(n_taps):
        wk = w_ref[k:k + 1, lanes]
        for i, s in enumerate(starts):
            term = wk * ext_ref[j, pl.ds(s + k, nrows), :]
            accs[i] = term if accs[i] is None else accs[i] + term
    return accs


def _after(lhs, pieces, never):
    if pieces is None:
        return lhs
    rows = 2 * SUBLANES
    total = None
    for p in pieces:
        for r in range(0, p.shape[0], rows):
            slab = p[r:r + rows, :]
            total = slab if total is None else total + slab
    head = jnp.where(never, total.astype(lhs.dtype), lhs[0:rows, 0:LANES])
    top = jnp.concatenate([head, lhs[0:rows, LANES:]], axis=1)
    return jnp.concatenate([top, lhs[rows:, :]], axis=0)


def _in_proj(x, g_mix_ref, w_in_ref, gw):
    h = _rmsnorm(x, g_mix_ref[...]).astype(BF16)
    return [_dot(h, w_in_ref[:, g * gw:(g + 1) * gw]) for g in range(5)]


def _out_ffn(x, y_a, conv_b, bias_ref, ln_g_ref, ln_b_ref, w_out_ref, g_ffn_ref,
             w_gate_ref, w_up_ref, w_down_ref, g_final_ref):
    y_b = _silu(_layernorm(conv_b + bias_ref[...], ln_g_ref[...], ln_b_ref[...]))
    y_cat = jnp.concatenate([y_a, y_b], axis=-1).astype(BF16)
    x1 = x + _dot(y_cat, w_out_ref[...])
    h2 = _rmsnorm(x1, g_ffn_ref[...]).astype(BF16)
    gate = _dot(h2, w_gate_ref[...])
    up = _dot(h2, w_up_ref[...])
    f = (_silu(gate) * up).astype(BF16)
    x2 = x1 + _dot(f, w_down_ref[...])
    return _rmsnorm(x2, g_final_ref[...])


def _prompt_kernel(xa_ref, xb_ref, g_mix_ref, w_in_ref, wa_ref, wb_ref, bias_ref, ln_g_ref,
                   ln_b_ref, w_out_ref, g_ffn_ref, w_gate_ref, w_up_ref, w_down_ref, g_final_ref,
                   y_ref, new_a_ref, new_b_ref, exta_ref, extb_ref, bg_ref,
                   *, tile, row_chunk, n_tiles, tiles_per_seq):
    gw = wa_ref.shape[1]
    n_lane = gw // LANES
    hist_a = exta_ref.shape[1] - tile
    hist_b = extb_ref.shape[1] - tile
    g = pl.program_id(0)

    @pl.when(g == 0)
    def _():
        exta_ref[...] = jnp.zeros(exta_ref.shape, F32)
        extb_ref[...] = jnp.zeros(extb_ref.shape, F32)
        bg_ref[...] = jnp.zeros(bg_ref.shape, F32)

    h = _rmsnorm(xa_ref[...], g_mix_ref[...]).astype(BF16)
    never = g < 0
    starts = list(range(0, tile, row_chunk))
    groups, conv_a, conv_b = [], [], []
    done = None
    for j in range(n_lane):
        groups.append(_dot(_after(h, done, never), w_in_ref[:, j * gw:(j + 1) * gw]))
        a = _conv_windows(exta_ref, j, [hist_a - (K_A - 1) + s for s in starts], row_chunk,
                          wa_ref, K_A)
        b = _conv_windows(extb_ref, j, [hist_b - (K_B - 1) + s for s in starts], row_chunk,
                          wb_ref, K_B)
        done = a + b
        conv_a.append(jnp.concatenate(a, axis=0))
        conv_b.append(jnp.concatenate(b, axis=0))
    for j in range(n_lane, 5):
        groups.append(_dot(_after(h, done, never), w_in_ref[:, j * gw:(j + 1) * gw]))
    b_gate, c_gate, v, glu_val, glu_gate = groups
    y_a = bg_ref[...] * jnp.concatenate(conv_a, axis=1)
    conv_b = jnp.concatenate(conv_b, axis=1)
    tail_a = exta_ref[:, tile:tile + hist_a, :]
    tail_b = extb_ref[:, tile:tile + hist_b, :]

    seq_start = (jnp.minimum(g, n_tiles - 1) % tiles_per_seq) == 0
    cv = c_gate * v
    u_b = glu_val * _sigmoid(glu_gate)
    exta_ref[:, 0:hist_a, :] = jnp.where(seq_start, 0.0, tail_a)
    extb_ref[:, 0:hist_b, :] = jnp.where(seq_start, 0.0, tail_b)
    for j in range(n_lane):
        lanes = slice(j * LANES, (j + 1) * LANES)
        exta_ref[j, hist_a:hist_a + tile, :] = cv[:, lanes]
        extb_ref[j, hist_b:hist_b + tile, :] = u_b[:, lanes]
    bg_ref[...] = b_gate
    new_a_ref[...] = cv[tile - (K_A - 1):, :]
    new_b_ref[...] = u_b[tile - (K_B - 1):, :]

    y_ref[...] = _out_ffn(xb_ref[...], y_a, conv_b, bias_ref, ln_g_ref, ln_b_ref, w_out_ref,
                          g_ffn_ref, w_gate_ref, w_up_ref, w_down_ref, g_final_ref)


def _sample_kernel(x_ref, sa_ref, sb_ref, g_mix_ref, w_in_ref, wa_ref, wb_ref, bias_ref,
                   ln_g_ref, ln_b_ref, w_out_ref, g_ffn_ref, w_gate_ref, w_up_ref, w_down_ref,
                   g_final_ref, y_ref, new_a_ref, new_b_ref, exta_ref, extb_ref,
                   *, n_seq, seq_len, seq_group):
    gw = wa_ref.shape[1]
    n_lane = gw // LANES
    ra = exta_ref.shape[1] // n_seq
    rb = extb_ref.shape[1] // n_seq
    off_a = ra - seq_len - (K_A - 1)
    off_b = rb - seq_len - (K_B - 1)

    x = x_ref[...]
    b_gate, c_gate, v, glu_val, glu_gate = _in_proj(x, g_mix_ref, w_in_ref, gw)
    cv = c_gate * v
    u_b = glu_val * _sigmoid(glu_gate)
    for s in range(n_seq):
        rows = slice(s * seq_len, (s + 1) * seq_len)
        for j in range(n_lane):
            lanes = slice(j * LANES, (j + 1) * LANES)
            exta_ref[j, s * ra + off_a:s * ra + off_a + K_A - 1, :] = sa_ref[s, :, lanes]
            exta_ref[j, (s + 1) * ra - seq_len:(s + 1) * ra, :] = cv[rows, lanes]
            extb_ref[j, s * rb + off_b:s * rb + off_b + K_B - 1, :] = sb_ref[s, :, lanes]
            extb_ref[j, (s + 1) * rb - seq_len:(s + 1) * rb, :] = u_b[rows, lanes]

    conv_a, conv_b = [], []
    for j in range(n_lane):
        a, b = [], []
        for s0 in range(0, n_seq, seq_group):
            group = range(s0, s0 + seq_group)
            a += _conv_windows(exta_ref, j, [s * ra + off_a for s in group], seq_len, wa_ref, K_A)
            b += _conv_windows(extb_ref, j, [s * rb + off_b for s in group], seq_len, wb_ref, K_B)
        conv_a.append(jnp.concatenate(a, axis=0))
        conv_b.append(jnp.concatenate(b, axis=0))
    y_a = b_gate * jnp.concatenate(conv_a, axis=1)
    conv_b = jnp.concatenate(conv_b, axis=1)

    y_ref[...] = _out_ffn(x, y_a, conv_b, bias_ref, ln_g_ref, ln_b_ref, w_out_ref, g_ffn_ref,
                          w_gate_ref, w_up_ref, w_down_ref, g_final_ref)

    for s in range(n_seq):
        for j in range(n_lane):
            lanes = slice(j * LANES, (j + 1) * LANES)
            new_a_ref[s, :, lanes] = exta_ref[j, (s + 1) * ra - (K_A - 1):(s + 1) * ra, :]
            new_b_ref[s, :, lanes] = extb_ref[j, (s + 1) * rb - (K_B - 1):(s + 1) * rb, :]


def _round_up(n, m):
    return -(-n // m) * m


def _const_spec(shape):
    zeros = (0,) * len(shape)
    return pl.BlockSpec(shape, lambda g: zeros, pipeline_mode=pl.Buffered(1))


PROMPT_TILE = 256
PROMPT_ROW_CHUNK = 64
SAMPLE_SEQS = 32
SAMPLE_SEQ_GROUP = 16
VMEM_LIMIT_BYTES = 56 * 1024 * 1024


def kernel(x_prompt, x_sample, state_conv_a, state_conv_b, g_mix, w_in, conv_a_w, conv_b_w,
           conv_b_bias, ln_b_g, ln_b_b, w_out, g_ffn, w_gate, w_up, w_down, g_final):
    depth = g_mix.shape[0]
    assert depth == 1, "single-layer trunk"
    bp, seq, d = x_prompt.shape
    bs, dec_seq, _ = x_sample.shape
    gw = conv_a_w.shape[-1]
    assert conv_a_w.shape[1] == K_A and conv_b_w.shape[1] == K_B
    assert dec_seq == SUBLANES and gw % LANES == 0
    assert seq % PROMPT_TILE == 0 and bs % SAMPLE_SEQS == 0

    row = lambda a: a.reshape(1, -1)
    params = (row(g_mix[0]), w_in[0].astype(BF16), conv_a_w[0], conv_b_w[0], row(conv_b_bias[0]),
              row(ln_b_g[0]), row(ln_b_b[0]), w_out[0].astype(BF16), row(g_ffn[0]),
              w_gate[0].astype(BF16), w_up[0].astype(BF16), w_down[0].astype(BF16), row(g_final))

    tile = PROMPT_TILE
    tiles_per_seq = seq // tile
    n_tiles = bp * tiles_per_seq
    hist_a = _round_up(K_A - 1, SUBLANES)
    hist_b = _round_up(K_B - 1, SUBLANES)
    n_lane = gw // LANES
    tile_a = lambda g: jnp.minimum(g, n_tiles - 1)
    tile_b = lambda g: jnp.maximum(g - 1, 0)
    x_rows = x_prompt.reshape(bp * seq, d)
    y_prompt, pa, pb = pl.pallas_call(
        functools.partial(_prompt_kernel, tile=tile, row_chunk=PROMPT_ROW_CHUNK,
                          n_tiles=n_tiles, tiles_per_seq=tiles_per_seq),
        grid=(n_tiles + 1,),
        in_specs=[pl.BlockSpec((tile, d), lambda g: (tile_a(g), 0)),
                  pl.BlockSpec((tile, d), lambda g: (tile_b(g), 0))]
        + [_const_spec(p.shape) for p in params],
        out_specs=[pl.BlockSpec((tile, d), lambda g: (tile_b(g), 0)),
                   pl.BlockSpec((None, K_A - 1, gw), lambda g: (tile_a(g) // tiles_per_seq, 0, 0)),
                   pl.BlockSpec((None, K_B - 1, gw), lambda g: (tile_a(g) // tiles_per_seq, 0, 0))],
        out_shape=[jax.ShapeDtypeStruct((bp * seq, d), F32),
                   jax.ShapeDtypeStruct((bp, K_A - 1, gw), F32),
                   jax.ShapeDtypeStruct((bp, K_B - 1, gw), F32)],
        scratch_shapes=[pltpu.VMEM((n_lane, hist_a + tile, LANES), F32),
                        pltpu.VMEM((n_lane, hist_b + tile, LANES), F32),
                        pltpu.VMEM((tile, gw), F32)],
        compiler_params=pltpu.CompilerParams(
            dimension_semantics=("arbitrary",),
            vmem_limit_bytes=VMEM_LIMIT_BYTES),
        name="prompt_layer",
    )(x_rows, x_rows, *params)
    y_prompt = y_prompt.reshape(bp, seq, d)

    n_seq = SAMPLE_SEQS
    rows = n_seq * dec_seq
    ra = _round_up(K_A - 1, SUBLANES) + dec_seq
    rb = _round_up(K_B - 1, SUBLANES) + dec_seq
    y_sample, sa, sb = pl.pallas_call(
        functools.partial(_sample_kernel, n_seq=n_seq, seq_len=dec_seq,
                          seq_group=SAMPLE_SEQ_GROUP),
        grid=(bs // n_seq,),
        in_specs=[pl.BlockSpec((rows, d), lambda i: (i, 0)),
                  pl.BlockSpec((n_seq, K_A - 1, gw), lambda i: (i, 0, 0)),
                  pl.BlockSpec((n_seq, K_B - 1, gw), lambda i: (i, 0, 0))]
        + [_const_spec(p.shape) for p in params],
        out_specs=[pl.BlockSpec((rows, d), lambda i: (i, 0)),
                   pl.BlockSpec((n_seq, K_A - 1, gw), lambda i: (i, 0, 0)),
                   pl.BlockSpec((n_seq, K_B - 1, gw), lambda i: (i, 0, 0))],
        out_shape=[jax.ShapeDtypeStruct((bs * dec_seq, d), F32),
                   jax.ShapeDtypeStruct((bs, K_A - 1, gw), F32),
                   jax.ShapeDtypeStruct((bs, K_B - 1, gw), F32)],
        scratch_shapes=[pltpu.VMEM((n_lane, n_seq * ra, LANES), F32),
                        pltpu.VMEM((n_lane, n_seq * rb, LANES), F32)],
        compiler_params=pltpu.CompilerParams(
            dimension_semantics=("arbitrary",),
            vmem_limit_bytes=VMEM_LIMIT_BYTES),
        name="sample_layer",
    )(x_sample.reshape(bs * dec_seq, d), state_conv_a[0], state_conv_b[0], *params)

    return (y_prompt, y_sample.reshape(bs, dec_seq, d), pa[None], pb[None], sa[None], sb[None])
```

```python
import functools

import jax
import jax.numpy as jnp
from jax import lax
from jax.experimental import pallas as pl
from jax.experimental.pallas import tpu as pltpu

EPS = 1e-6
K_A = 3
K_B = 31
LANES = 128
SUBLANES = 8

F32 = jnp.float32
BF16 = jnp.bfloat16


def _rmsnorm(x, g):
    ms = jnp.mean(x * x, axis=-1, keepdims=True)
    return x * lax.rsqrt(ms + EPS) * g


def _layernorm(x, g, b):
    mu = jnp.mean(x, axis=-1, keepdims=True)
    xc = x - mu
    var = jnp.mean(xc * xc, axis=-1, keepdims=True)
    return xc * lax.rsqrt(var + EPS) * g + b


def _sigmoid(x):
    return 1.0 / (1.0 + jnp.exp(-x))


def _silu(x):
    return x * _sigmoid(x)


def _dot(a, b):
    return jnp.dot(a, b, preferred_element_type=F32)


def _conv_windows(ext_ref, j, starts, nrows, w_ref, n_taps):
    lanes = slice(j * LANES, (j + 1) * LANES)
    accs = [None] * len(starts)
    for k in range(n_taps):
        wk = w_ref[k:k + 1, lanes]
        for i, s in enumerate(starts):
            term = wk * ext_ref[j, pl.ds(s + k, nrows), :]
            accs[i] = term if accs[i] is None else accs[i] + term
    return accs


def _after(lhs, pieces, never):
    if not pieces:
        return lhs
    rows = 2 * SUBLANES
    total = None
    for p in pieces:
        for r in range(0, p.shape[0], rows):
            for c in range(0, p.shape[1], LANES):
                slab = p[r:r + rows, c:c + LANES].astype(lhs.dtype)
                total = slab if total is None else total + slab
    head = jnp.where(never, total, lhs[0:rows, 0:LANES])
    top = jnp.concatenate([head, lhs[0:rows, LANES:]], axis=1)
    return jnp.concatenate([top, lhs[rows:, :]], axis=0)


def _gating(x, g_mix_ref, w_in_ref, gw):
    h = _rmsnorm(x, g_mix_ref[...]).astype(BF16)
    b_gate, c_gate, v, glu_val, glu_gate = [
        _dot(h, w_in_ref[:, k * gw:(k + 1) * gw]) for k in range(5)]
    return b_gate, c_gate * v, glu_val * _sigmoid(glu_gate)


def _mix(b_gate, conv_a, conv_b, bias_ref, ln_g_ref, ln_b_ref):
    y_b = _silu(_layernorm(conv_b + bias_ref[...], ln_g_ref[...], ln_b_ref[...]))
    return jnp.concatenate([b_gate * conv_a, y_b], axis=-1).astype(BF16)


def _ffn(x1, g_ffn_ref, w_gate_ref, w_up_ref, w_down_ref, g_final_ref, ff_chunk,
         hide=(), never=None):
    d_ff = w_gate_ref.shape[1]
    h2 = _rmsnorm(x1, g_ffn_ref[...]).astype(BF16)
    acc = None
    pending = None
    for c, c0 in enumerate(range(0, d_ff, ff_chunk)):
        cols = slice(c0, min(c0 + ff_chunk, d_ff))
        lhs = _after(h2, hide[c - 1:c] if c >= 1 else (), never)
        gate = _dot(lhs, w_gate_ref[:, cols])
        up = _dot(lhs, w_up_ref[:, cols])
        if pending is not None:
            part = _dot(pending[0], w_down_ref[pending[1], :])
            acc = part if acc is None else acc + part
        pending = ((_silu(gate) * up).astype(BF16), cols)
    part = _dot(pending[0], w_down_ref[pending[1], :])
    acc = part if acc is None else acc + part
    return _rmsnorm(x1 + acc, g_final_ref[...])


def _prompt_kernel(x1_ref, x3_ref, g_mix_ref, w_in_ref, wa_ref, wb_ref, bias_ref, ln_g_ref,
                   ln_b_ref, w_out_ref, g_ffn_ref, w_gate_ref, w_up_ref, w_down_ref, g_final_ref,
                   y_ref, new_a_ref, new_b_ref, exta_ref, extb_ref, bg_ref, ycat_ref,
                   *, tile, row_chunk, n_tiles, tiles_per_seq, ff_chunk):
    gw = wa_ref.shape[1]
    n_lane = gw // LANES
    hist_a = exta_ref.shape[1] - tile
    hist_b = extb_ref.shape[1] - tile
    g = pl.program_id(0)
    never = g < 0

    @pl.when(g == 0)
    def _():
        exta_ref[...] = jnp.zeros(exta_ref.shape, F32)
        extb_ref[...] = jnp.zeros(extb_ref.shape, F32)
        bg_ref[...] = jnp.zeros(bg_ref.shape, F32)
        ycat_ref[...] = jnp.zeros(ycat_ref.shape, ycat_ref.dtype)

    x1 = x3_ref[...] + _dot(ycat_ref[...], w_out_ref[...])

    h = _rmsnorm(x1_ref[...], g_mix_ref[...]).astype(BF16)
    starts = list(range(0, tile, row_chunk))
    groups, conv_a, conv_b = [], [], []
    done = None
    for j in range(n_lane):
        groups.append(_dot(_after(h, done, never), w_in_ref[:, j * gw:(j + 1) * gw]))
        a = _conv_windows(exta_ref, j, [hist_a - (K_A - 1) + s for s in starts], row_chunk,
                          wa_ref, K_A)
        b = _conv_windows(extb_ref, j, [hist_b - (K_B - 1) + s for s in starts], row_chunk,
                          wb_ref, K_B)
        done = a + b
        conv_a.append(jnp.concatenate(a, axis=0))
        conv_b.append(jnp.concatenate(b, axis=0))
    for j in range(n_lane, 5):
        groups.append(_dot(_after(h, done, never), w_in_ref[:, j * gw:(j + 1) * gw]))
    b_gate, c_gate, v, glu_val, glu_gate = groups
    y_cat = _mix(bg_ref[...], jnp.concatenate(conv_a, axis=1), jnp.concatenate(conv_b, axis=1),
                 bias_ref, ln_g_ref, ln_b_ref)
    tail_a = exta_ref[:, tile:tile + hist_a, :]
    tail_b = extb_ref[:, tile:tile + hist_b, :]

    seq_start = (jnp.minimum(g, n_tiles - 1) % tiles_per_seq) == 0
    cv = c_gate * v
    u_b = glu_val * _sigmoid(glu_gate)
    exta_ref[:, 0:hist_a, :] = jnp.where(seq_start, 0.0, tail_a)
    extb_ref[:, 0:hist_b, :] = jnp.where(seq_start, 0.0, tail_b)
    for j in range(n_lane):
        lanes = slice(j * LANES, (j + 1) * LANES)
        exta_ref[j, hist_a:hist_a + tile, :] = cv[:, lanes]
        extb_ref[j, hist_b:hist_b + tile, :] = u_b[:, lanes]
    bg_ref[...] = b_gate
    new_a_ref[...] = cv[tile - (K_A - 1):, :]
    new_b_ref[...] = u_b[tile - (K_B - 1):, :]

    y_ref[...] = _ffn(x1, g_ffn_ref, w_gate_ref, w_up_ref, w_down_ref, g_final_ref, ff_chunk)

    ycat_ref[...] = y_cat


def _sample_kernel(x_ref, sa_ref, sb_ref, g_mix_ref, w_in_ref, wa_ref, wb_ref, bias_ref,
                   ln_g_ref, ln_b_ref, w_out_ref, g_ffn_ref, w_gate_ref, w_up_ref, w_down_ref,
                   g_final_ref, y_ref, new_a_ref, new_b_ref, exta_ref, extb_ref,
                   *, n_seq, seq_len, seq_group, ff_chunk):
    gw = wa_ref.shape[1]
    n_lane = gw // LANES
    ra = exta_ref.shape[1] // n_seq
    rb = extb_ref.shape[1] // n_seq
    off_a = ra - seq_len - (K_A - 1)
    off_b = rb - seq_len - (K_B - 1)

    x = x_ref[...]
    b_gate, cv, u_b = _gating(x, g_mix_ref, w_in_ref, gw)
    for s in range(n_seq):
        rows = slice(s * seq_len, (s + 1) * seq_len)
        for j in range(n_lane):
            lanes = slice(j * LANES, (j + 1) * LANES)
            exta_ref[j, s * ra + off_a:s * ra + off_a + K_A - 1, :] = sa_ref[s, :, lanes]
            exta_ref[j, (s + 1) * ra - seq_len:(s + 1) * ra, :] = cv[rows, lanes]
            extb_ref[j, s * rb + off_b:s * rb + off_b + K_B - 1, :] = sb_ref[s, :, lanes]
            extb_ref[j, (s + 1) * rb - seq_len:(s + 1) * rb, :] = u_b[rows, lanes]

    conv_a, conv_b = [], []
    for j in range(n_lane):
        a, b = [], []
        for s0 in range(0, n_seq, seq_group):
            group = range(s0, s0 + seq_group)
            a += _conv_windows(exta_ref, j, [s * ra + off_a for s in group], seq_len, wa_ref, K_A)
            b += _conv_windows(extb_ref, j, [s * rb + off_b for s in group], seq_len, wb_ref, K_B)
        conv_a.append(jnp.concatenate(a, axis=0))
        conv_b.append(jnp.concatenate(b, axis=0))
    y_cat = _mix(b_gate, jnp.concatenate(conv_a, axis=1), jnp.concatenate(conv_b, axis=1),
                 bias_ref, ln_g_ref, ln_b_ref)
    x1 = x + _dot(y_cat, w_out_ref[...])
    y_ref[...] = _ffn(x1, g_ffn_ref, w_gate_ref, w_up_ref, w_down_ref, g_final_ref, ff_chunk)

    for s in range(n_seq):
        for j in range(n_lane):
            lanes = slice(j * LANES, (j + 1) * LANES)
            new_a_ref[s, :, lanes] = exta_ref[j, (s + 1) * ra - (K_A - 1):(s + 1) * ra, :]
            new_b_ref[s, :, lanes] = extb_ref[j, (s + 1) * rb - (K_B - 1):(s + 1) * rb, :]


def _round_up(n, m):
    return -(-n // m) * m


def _const_spec(shape):
    zeros = (0,) * len(shape)
    return pl.BlockSpec(shape, lambda g: zeros, pipeline_mode=pl.Buffered(1))


PROMPT_TILE = 256
PROMPT_ROW_CHUNK = 64
FF_CHUNK = 512
SAMPLE_SEQS = 32
SAMPLE_SEQ_GROUP = 16
VMEM_LIMIT_BYTES = 56 * 1024 * 1024


def kernel(x_prompt, x_sample, state_conv_a, state_conv_b, g_mix, w_in, conv_a_w, conv_b_w,
           conv_b_bias, ln_b_g, ln_b_b, w_out, g_ffn, w_gate, w_up, w_down, g_final):
    depth = g_mix.shape[0]
    assert depth == 1, "single-layer trunk"
    bp, seq, d = x_prompt.shape
    bs, dec_seq, _ = x_sample.shape
    gw = conv_a_w.shape[-1]
    assert conv_a_w.shape[1] == K_A and conv_b_w.shape[1] == K_B
    assert dec_seq == SUBLANES and gw % LANES == 0
    assert seq % PROMPT_TILE == 0 and bs % SAMPLE_SEQS == 0

    row = lambda a: a.reshape(1, -1)
    params = (row(g_mix[0]), w_in[0].astype(BF16), conv_a_w[0], conv_b_w[0], row(conv_b_bias[0]),
              row(ln_b_g[0]), row(ln_b_b[0]), w_out[0].astype(BF16), row(g_ffn[0]),
              w_gate[0].astype(BF16), w_up[0].astype(BF16), w_down[0].astype(BF16), row(g_final))

    tile = PROMPT_TILE
    tiles_per_seq = seq // tile
    n_tiles = bp * tiles_per_seq
    hist_a = _round_up(K_A - 1, SUBLANES)
    hist_b = _round_up(K_B - 1, SUBLANES)
    n_lane = gw // LANES
    tile_1 = lambda g: jnp.minimum(g, n_tiles - 1)
    tile_3 = lambda g: jnp.maximum(g - 2, 0)
    x_rows = x_prompt.reshape(bp * seq, d)
    y_prompt, pa, pb = pl.pallas_call(
        functools.partial(_prompt_kernel, tile=tile, row_chunk=PROMPT_ROW_CHUNK,
                          n_tiles=n_tiles, tiles_per_seq=tiles_per_seq, ff_chunk=FF_CHUNK),
        grid=(n_tiles + 2,),
        in_specs=[pl.BlockSpec((tile, d), lambda g: (tile_1(g), 0)),
                  pl.BlockSpec((tile, d), lambda g: (tile_3(g), 0))]
        + [_const_spec(p.shape) for p in params],
        out_specs=[pl.BlockSpec((tile, d), lambda g: (tile_3(g), 0)),
                   pl.BlockSpec((None, K_A - 1, gw), lambda g: (tile_1(g) // tiles_per_seq, 0, 0)),
                   pl.BlockSpec((None, K_B - 1, gw), lambda g: (tile_1(g) // tiles_per_seq, 0, 0))],
        out_shape=[jax.ShapeDtypeStruct((bp * seq, d), F32),
                   jax.ShapeDtypeStruct((bp, K_A - 1, gw), F32),
                   jax.ShapeDtypeStruct((bp, K_B - 1, gw), F32)],
        scratch_shapes=[pltpu.VMEM((n_lane, hist_a + tile, LANES), F32),
                        pltpu.VMEM((n_lane, hist_b + tile, LANES), F32),
                        pltpu.VMEM((tile, gw), F32),
                        pltpu.VMEM((tile, 2 * gw), BF16)],
        compiler_params=pltpu.CompilerParams(
            dimension_semantics=("arbitrary",),
            vmem_limit_bytes=VMEM_LIMIT_BYTES),
        name="prompt_layer",
    )(x_rows, x_rows, *params)
    y_prompt = y_prompt.reshape(bp, seq, d)

    n_seq = SAMPLE_SEQS
    rows = n_seq * dec_seq
    ra = _round_up(K_A - 1, SUBLANES) + dec_seq
    rb = _round_up(K_B - 1, SUBLANES) + dec_seq
    y_sample, sa, sb = pl.pallas_call(
        functools.partial(_sample_kernel, n_seq=n_seq, seq_len=dec_seq,
                          seq_group=SAMPLE_SEQ_GROUP, ff_chunk=FF_CHUNK),
        grid=(bs // n_seq,),
        in_specs=[pl.BlockSpec((rows, d), lambda i: (i, 0)),
                  pl.BlockSpec((n_seq, K_A - 1, gw), lambda i: (i, 0, 0)),
                  pl.BlockSpec((n_seq, K_B - 1, gw), lambda i: (i, 0, 0))]
        + [_const_spec(p.shape) for p in params],
        out_specs=[pl.BlockSpec((rows, d), lambda i: (i, 0)),
                   pl.BlockSpec((n_seq, K_A - 1, gw), lambda i: (i, 0, 0)),
                   pl.BlockSpec((n_seq, K_B - 1, gw), lambda i: (i, 0, 0))],
        out_shape=[jax.ShapeDtypeStruct((bs * dec_seq, d), F32),
                   jax.ShapeDtypeStruct((bs, K_A - 1, gw), F32),
                   jax.ShapeDtypeStruct((bs, K_B - 1, gw), F32)],
        scratch_shapes=[pltpu.VMEM((n_lane, n_seq * ra, LANES), F32),
                        pltpu.VMEM((n_lane, n_seq * rb, LANES), F32)],
        compiler_params=pltpu.CompilerParams(
            dimension_semantics=("arbitrary",),
            vmem_limit_bytes=VMEM_LIMIT_BYTES),
        name="sample_layer",
    )(x_sample.reshape(bs * dec_seq, d), state_conv_a[0], state_conv_b[0], *params)

    return (y_prompt, y_sample.reshape(bs, dec_seq, d), pa[None], pb[None], sa[None], sb[None])
```

```python
import functools

import jax
import jax.numpy as jnp
from jax import lax
from jax.experimental import pallas as pl
from jax.experimental.pallas import tpu as pltpu

EPS = 1e-6
K_A = 3
K_B = 31
LANES = 128
SUBLANES = 8

F32 = jnp.float32
BF16 = jnp.bfloat16


def _rmsnorm(x, g):
    ms = jnp.mean(x * x, axis=-1, keepdims=True)
    return x * lax.rsqrt(ms + EPS) * g


def _layernorm(x, g, b):
    mu = jnp.mean(x, axis=-1, keepdims=True)
    xc = x - mu
    var = jnp.mean(xc * xc, axis=-1, keepdims=True)
    return xc * lax.rsqrt(var + EPS) * g + b


def _sigmoid(x):
    return 1.0 / (1.0 + jnp.exp(-x))


def _silu(x):
    return x * _sigmoid(x)


def _dot(a, b):
    return jnp.dot(a, b, preferred_element_type=F32)


def _conv_windows(ext_ref, j, starts, nrows, w_ref, n_taps):
    lanes = slice(j * LANES, (j + 1) * LANES)
    accs = [None] * len(starts)
    for k in range(n_taps):
        wk = w_ref[k:k + 1, lanes]
        for i, s in enumerate(starts):
            term = wk * ext_ref[j, pl.ds(s + k, nrows), :]
            accs[i] = term if accs[i] is None else accs[i] + term
    return accs


def _after(lhs, pieces, never):
    if not pieces:
        return lhs
    rows = 2 * SUBLANES
    total = None
    for p in pieces:
        for r in range(0, p.shape[0], rows):
            for c in range(0, p.shape[1], LANES):
                slab = p[r:r + rows, c:c + LANES].astype(lhs.dtype)
                total = slab if total is None else total + slab
    head = jnp.where(never, total, lhs[0:rows, 0:LANES])
    top = jnp.concatenate([head, lhs[0:rows, LANES:]], axis=1)
    return jnp.concatenate([top, lhs[rows:, :]], axis=0)


def _gating(x, g_mix_ref, w_in_ref, gw):
    h = _rmsnorm(x, g_mix_ref[...]).astype(BF16)
    b_gate, c_gate, v, glu_val, glu_gate = [
        _dot(h, w_in_ref[:, k * gw:(k + 1) * gw]) for k in range(5)]
    return b_gate, c_gate * v, glu_val * _sigmoid(glu_gate)


def _mix(b_gate, conv_a, conv_b, bias_ref, ln_g_ref, ln_b_ref):
    y_b = _silu(_layernorm(conv_b + bias_ref[...], ln_g_ref[...], ln_b_ref[...]))
    return jnp.concatenate([b_gate * conv_a, y_b], axis=-1).astype(BF16)


def _ffn(x1, g_ffn_ref, w_gate_ref, w_up_ref, w_down_ref, ff_chunk, hide=None, never=None):
    d_ff = w_gate_ref.shape[1]
    h2 = _rmsnorm(x1, g_ffn_ref[...]).astype(BF16)
    acc = None
    pending = None
    for c, c0 in enumerate(range(0, d_ff, ff_chunk)):
        cols = slice(c0, min(c0 + ff_chunk, d_ff))
        lhs = _after(h2, (hide or {}).get(c), never)
        gate = _dot(lhs, w_gate_ref[:, cols])
        up = _dot(lhs, w_up_ref[:, cols])
        if pending is not None:
            part = _dot(pending[0], w_down_ref[pending[1], :])
            acc = part if acc is None else acc + part
        pending = ((_silu(gate) * up).astype(BF16), cols)
    part = _dot(pending[0], w_down_ref[pending[1], :])
    acc = part if acc is None else acc + part
    return x1 + acc


def _prompt_kernel(x1_ref, x3_ref, g_mix_ref, w_in_ref, wa_ref, wb_ref, bias_ref, ln_g_ref,
                   ln_b_ref, w_out_ref, g_ffn_ref, w_gate_ref, w_up_ref, w_down_ref, g_final_ref,
                   y_ref, new_a_ref, new_b_ref, exta_ref, extb_ref, bg_ref, ycat_ref,
                   *, tile, row_chunk, n_tiles, tiles_per_seq, ff_chunk):
    gw = wa_ref.shape[1]
    n_lane = gw // LANES
    hist_a = exta_ref.shape[1] - tile
    hist_b = extb_ref.shape[1] - tile
    g = pl.program_id(0)
    never = g < 0

    @pl.when(g == 0)
    def _():
        exta_ref[...] = jnp.zeros(exta_ref.shape, F32)
        extb_ref[...] = jnp.zeros(extb_ref.shape, F32)
        bg_ref[...] = jnp.zeros(bg_ref.shape, F32)
        ycat_ref[...] = jnp.zeros(ycat_ref.shape, ycat_ref.dtype)

    x1 = x3_ref[...] + _dot(ycat_ref[...], w_out_ref[...])

    starts = list(range(0, tile, row_chunk))
    units = []
    for j in range(n_lane):
        a = _conv_windows(exta_ref, j, [hist_a - (K_A - 1) + s for s in starts], row_chunk,
                          wa_ref, K_A)
        b = _conv_windows(extb_ref, j, [hist_b - (K_B - 1) + s for s in starts], row_chunk,
                          wb_ref, K_B)
        units.append(a + b)
    conv_a = [jnp.concatenate(u[:len(starts)], axis=0) for u in units]
    conv_b = [jnp.concatenate(u[len(starts):], axis=0) for u in units]
    y_cat = _mix(bg_ref[...], jnp.concatenate(conv_a, axis=1), jnp.concatenate(conv_b, axis=1),
                 bias_ref, ln_g_ref, ln_b_ref)
    tail_a = exta_ref[:, tile:tile + hist_a, :]
    tail_b = extb_ref[:, tile:tile + hist_b, :]

    h = _rmsnorm(x1_ref[...], g_mix_ref[...]).astype(BF16)
    in_proj = lambda k: _dot(h, w_in_ref[:, k * gw:(k + 1) * gw])
    glu_val, glu_gate = in_proj(3), in_proj(4)
    x2 = _ffn(x1, g_ffn_ref, w_gate_ref, w_up_ref, w_down_ref, ff_chunk,
              hide={c: units[j] for c, j in CONV_UNIT_BEFORE_FFN_CHUNK.items()}, never=never)
    c_gate, v, b_gate = in_proj(1), in_proj(2), in_proj(0)
    y_ref[...] = _rmsnorm(x2, g_final_ref[...])

    seq_start = (jnp.minimum(g, n_tiles - 1) % tiles_per_seq) == 0
    cv = c_gate * v
    u_b = glu_val * _sigmoid(glu_gate)
    exta_ref[:, 0:hist_a, :] = jnp.where(seq_start, 0.0, tail_a)
    extb_ref[:, 0:hist_b, :] = jnp.where(seq_start, 0.0, tail_b)
    for j in range(n_lane):
        lanes = slice(j * LANES, (j + 1) * LANES)
        exta_ref[j, hist_a:hist_a + tile, :] = cv[:, lanes]
        extb_ref[j, hist_b:hist_b + tile, :] = u_b[:, lanes]
    bg_ref[...] = b_gate
    new_a_ref[...] = cv[tile - (K_A - 1):, :]
    new_b_ref[...] = u_b[tile - (K_B - 1):, :]

    ycat_ref[...] = y_cat


def _sample_kernel(x_ref, sa_ref, sb_ref, g_mix_ref, w_in_ref, wa_ref, wb_ref, bias_ref,
                   ln_g_ref, ln_b_ref, w_out_ref, g_ffn_ref, w_gate_ref, w_up_ref, w_down_ref,
                   g_final_ref, y_ref, new_a_ref, new_b_ref, exta_ref, extb_ref,
                   *, n_seq, seq_len, seq_group, ff_chunk):
    gw = wa_ref.shape[1]
    n_lane = gw // LANES
    ra = exta_ref.shape[1] // n_seq
    rb = extb_ref.shape[1] // n_seq
    off_a = ra - seq_len - (K_A - 1)
    off_b = rb - seq_len - (K_B - 1)

    x = x_ref[...]
    b_gate, cv, u_b = _gating(x, g_mix_ref, w_in_ref, gw)
    for s in range(n_seq):
        rows = slice(s * seq_len, (s + 1) * seq_len)
        for j in range(n_lane):
            lanes = slice(j * LANES, (j + 1) * LANES)
            exta_ref[j, s * ra + off_a:s * ra + off_a + K_A - 1, :] = sa_ref[s, :, lanes]
            exta_ref[j, (s + 1) * ra - seq_len:(s + 1) * ra, :] = cv[rows, lanes]
            extb_ref[j, s * rb + off_b:s * rb + off_b + K_B - 1, :] = sb_ref[s, :, lanes]
            extb_ref[j, (s + 1) * rb - seq_len:(s + 1) * rb, :] = u_b[rows, lanes]

    conv_a, conv_b = [], []
    for j in range(n_lane):
        a, b = [], []
        for s0 in range(0, n_seq, seq_group):
            group = range(s0, s0 + seq_group)
            a += _conv_windows(exta_ref, j, [s * ra + off_a for s in group], seq_len, wa_ref, K_A)
            b += _conv_windows(extb_ref, j, [s * rb + off_b for s in group], seq_len, wb_ref, K_B)
        conv_a.append(jnp.concatenate(a, axis=0))
        conv_b.append(jnp.concatenate(b, axis=0))
    y_cat = _mix(b_gate, jnp.concatenate(conv_a, axis=1), jnp.concatenate(conv_b, axis=1),
                 bias_ref, ln_g_ref, ln_b_ref)
    x1 = x + _dot(y_cat, w_out_ref[...])
    x2 = _ffn(x1, g_ffn_ref, w_gate_ref, w_up_ref, w_down_ref, ff_chunk)
    y_ref[...] = _rmsnorm(x2, g_final_ref[...])

    for s in range(n_seq):
        for j in range(n_lane):
            lanes = slice(j * LANES, (j + 1) * LANES)
            new_a_ref[s, :, lanes] = exta_ref[j, (s + 1) * ra - (K_A - 1):(s + 1) * ra, :]
            new_b_ref[s, :, lanes] = extb_ref[j, (s + 1) * rb - (K_B - 1):(s + 1) * rb, :]


def _round_up(n, m):
    return -(-n // m) * m


def _const_spec(shape):
    zeros = (0,) * len(shape)
    return pl.BlockSpec(shape, lambda g: zeros, pipeline_mode=pl.Buffered(1))


PROMPT_TILE = 256
PROMPT_ROW_CHUNK = 64
CONV_UNIT_BEFORE_FFN_CHUNK = {1: 0, 2: 1, 3: 2, 4: 3}
FF_CHUNK = 512
SAMPLE_SEQS = 32
SAMPLE_SEQ_GROUP = 16
VMEM_LIMIT_BYTES = 56 * 1024 * 1024


def kernel(x_prompt, x_sample, state_conv_a, state_conv_b, g_mix, w_in, conv_a_w, conv_b_w,
           conv_b_bias, ln_b_g, ln_b_b, w_out, g_ffn, w_gate, w_up, w_down, g_final):
    depth = g_mix.shape[0]
    assert depth == 1, "single-layer trunk"
    bp, seq, d = x_prompt.shape
    bs, dec_seq, _ = x_sample.shape
    gw = conv_a_w.shape[-1]
    assert conv_a_w.shape[1] == K_A and conv_b_w.shape[1] == K_B
    assert dec_seq == SUBLANES and gw % LANES == 0
    assert seq % PROMPT_TILE == 0 and bs % SAMPLE_SEQS == 0

    row = lambda a: a.reshape(1, -1)
    params = (row(g_mix[0]), w_in[0].astype(BF16), conv_a_w[0], conv_b_w[0], row(conv_b_bias[0]),
              row(ln_b_g[0]), row(ln_b_b[0]), w_out[0].astype(BF16), row(g_ffn[0]),
              w_gate[0].astype(BF16), w_up[0].astype(BF16), w_down[0].astype(BF16), row(g_final))

    tile = PROMPT_TILE
    tiles_per_seq = seq // tile
    n_tiles = bp * tiles_per_seq
    hist_a = _round_up(K_A - 1, SUBLANES)
    hist_b = _round_up(K_B - 1, SUBLANES)
    n_lane = gw // LANES
    tile_1 = lambda g: jnp.minimum(g, n_tiles - 1)
    tile_3 = lambda g: jnp.maximum(g - 2, 0)
    x_rows = x_prompt.reshape(bp * seq, d)
    y_prompt, pa, pb = pl.pallas_call(
        functools.partial(_prompt_kernel, tile=tile, row_chunk=PROMPT_ROW_CHUNK,
                          n_tiles=n_tiles, tiles_per_seq=tiles_per_seq, ff_chunk=FF_CHUNK),
        grid=(n_tiles + 2,),
        in_specs=[pl.BlockSpec((tile, d), lambda g: (tile_1(g), 0)),
                  pl.BlockSpec((tile, d), lambda g: (tile_3(g), 0))]
        + [_const_spec(p.shape) for p in params],
        out_specs=[pl.BlockSpec((tile, d), lambda g: (tile_3(g), 0)),
                   pl.BlockSpec((None, K_A - 1, gw), lambda g: (tile_1(g) // tiles_per_seq, 0, 0)),
                   pl.BlockSpec((None, K_B - 1, gw), lambda g: (tile_1(g) // tiles_per_seq, 0, 0))],
        out_shape=[jax.ShapeDtypeStruct((bp * seq, d), F32),
                   jax.ShapeDtypeStruct((bp, K_A - 1, gw), F32),
                   jax.ShapeDtypeStruct((bp, K_B - 1, gw), F32)],
        scratch_shapes=[pltpu.VMEM((n_lane, hist_a + tile, LANES), F32),
                        pltpu.VMEM((n_lane, hist_b + tile, LANES), F32),
                        pltpu.VMEM((tile, gw), F32),
                        pltpu.VMEM((tile, 2 * gw), BF16)],
        compiler_params=pltpu.CompilerParams(
            dimension_semantics=("arbitrary",),
            vmem_limit_bytes=VMEM_LIMIT_BYTES),
        name="prompt_layer",
    )(x_rows, x_rows, *params)
    y_prompt = y_prompt.reshape(bp, seq, d)

    n_seq = SAMPLE_SEQS
    rows = n_seq * dec_seq
    ra = _round_up(K_A - 1, SUBLANES) + dec_seq
    rb = _round_up(K_B - 1, SUBLANES) + dec_seq
    y_sample, sa, sb = pl.pallas_call(
        functools.partial(_sample_kernel, n_seq=n_seq, seq_len=dec_seq,
                          seq_group=SAMPLE_SEQ_GROUP, ff_chunk=FF_CHUNK),
        grid=(bs // n_seq,),
        in_specs=[pl.BlockSpec((rows, d), lambda i: (i, 0)),
                  pl.BlockSpec((n_seq, K_A - 1, gw), lambda i: (i, 0, 0)),
                  pl.BlockSpec((n_seq, K_B - 1, gw), lambda i: (i, 0, 0))]
        + [_const_spec(p.shape) for p in params],
        out_specs=[pl.BlockSpec((rows, d), lambda i: (i, 0)),
                   pl.BlockSpec((n_seq, K_A - 1, gw), lambda i: (i, 0, 0)),
                   pl.BlockSpec((n_seq, K_B - 1, gw), lambda i: (i, 0, 0))],
        out_shape=[jax.ShapeDtypeStruct((bs * dec_seq, d), F32),
                   jax.ShapeDtypeStruct((bs, K_A - 1, gw), F32),
                   jax.ShapeDtypeStruct((bs, K_B - 1, gw), F32)],
        scratch_shapes=[pltpu.VMEM((n_lane, n_seq * ra, LANES), F32),
                        pltpu.VMEM((n_lane, n_seq * rb, LANES), F32)],
        compiler_params=pltpu.CompilerParams(
            dimension_semantics=("arbitrary",),
            vmem_limit_bytes=VMEM_LIMIT_BYTES),
        name="sample_layer",
    )(x_sample.reshape(bs * dec_seq, d), state_conv_a[0], state_conv_b[0], *params)

    return (y_prompt, y_sample.reshape(bs, dec_seq, d), pa[None], pb[None], sa[None], sb[None])
```

```python
import functools

import jax
import jax.numpy as jnp
from jax import lax
from jax.experimental import pallas as pl
from jax.experimental.pallas import tpu as pltpu

EPS = 1e-6
K_A = 3
K_B = 31
LANES = 128
SUBLANES = 8

F32 = jnp.float32
BF16 = jnp.bfloat16


def _rmsnorm(x, g):
    ms = jnp.mean(x * x, axis=-1, keepdims=True)
    return x * lax.rsqrt(ms + EPS) * g


def _layernorm(x, g, b):
    mu = jnp.mean(x, axis=-1, keepdims=True)
    xc = x - mu
    var = jnp.mean(xc * xc, axis=-1, keepdims=True)
    return xc * lax.rsqrt(var + EPS) * g + b


def _sigmoid(x):
    return 1.0 / (1.0 + jnp.exp(-x))


def _silu(x):
    return x * _sigmoid(x)


def _dot(a, b):
    return jnp.dot(a, b, preferred_element_type=F32)


def _conv_windows(ext_ref, j, starts, nrows, w_ref, n_taps):
    lanes = slice(j * LANES, (j + 1) * LANES)
    accs = [None] * len(starts)
    for k in range(n_taps):
        wk = w_ref[k:k + 1, lanes]
        for i, s in enumerate(starts):
            term = wk * ext_ref[j, pl.ds(s + k, nrows), :]
            accs[i] = term if accs[i] is None else accs[i] + term
    return accs


def _after(lhs, pieces, never):
    if not pieces:
        return lhs
    rows = 2 * SUBLANES
    total = None
    for p in pieces:
        for r in range(0, p.shape[0], rows):
            for c in range(0, p.shape[1], LANES):
                slab = p[r:r + rows, c:c + LANES].astype(lhs.dtype)
                total = slab if total is None else total + slab
    head = jnp.where(never, total, lhs[0:rows, 0:LANES])
    top = jnp.concatenate([head, lhs[0:rows, LANES:]], axis=1)
    return jnp.concatenate([top, lhs[rows:, :]], axis=0)


def _gating(x, g_mix_ref, w_in_ref, gw):
    h = _rmsnorm(x, g_mix_ref[...]).astype(BF16)
    b_gate, c_gate, v, glu_val, glu_gate = [
        _dot(h, w_in_ref[:, k * gw:(k + 1) * gw]) for k in range(5)]
    return b_gate, c_gate * v, glu_val * _sigmoid(glu_gate)


def _mix(b_gate, conv_a, conv_b, bias_ref, ln_g_ref, ln_b_ref):
    y_b = _silu(_layernorm(conv_b + bias_ref[...], ln_g_ref[...], ln_b_ref[...]))
    return jnp.concatenate([b_gate * conv_a, y_b], axis=-1).astype(BF16)


def _ffn(x1, g_ffn_ref, w_gate_ref, w_up_ref, w_down_ref, ff_chunk, hide=None, never=None):
    d_ff = w_gate_ref.shape[1]
    h2 = _rmsnorm(x1, g_ffn_ref[...]).astype(BF16)
    acc = None
    pending = None
    for c, c0 in enumerate(range(0, d_ff, ff_chunk)):
        cols = slice(c0, min(c0 + ff_chunk, d_ff))
        lhs = _after(h2, (hide or {}).get(c), never)
        gate = _dot(lhs, w_gate_ref[:, cols])
        up = _dot(lhs, w_up_ref[:, cols])
        if pending is not None:
            part = _dot(pending[0], w_down_ref[pending[1], :])
            acc = part if acc is None else acc + part
        pending = ((_silu(gate) * up).astype(BF16), cols)
    part = _dot(pending[0], w_down_ref[pending[1], :])
    acc = part if acc is None else acc + part
    return x1 + acc


def _prompt_kernel(x1_ref, x3_ref, g_mix_ref, w_in_ref, wa_ref, wb_ref, bias_ref, ln_g_ref,
                   ln_b_ref, w_out_ref, g_ffn_ref, w_gate_ref, w_up_ref, w_down_ref, g_final_ref,
                   y_ref, new_a_ref, new_b_ref, exta_ref, extb_ref, bg_ref, ycat_ref,
                   *, tile, row_chunk, n_tiles, tiles_per_seq, ff_chunk):
    gw = wa_ref.shape[1]
    n_lane = gw // LANES
    hist_a = exta_ref.shape[1] - tile
    hist_b = extb_ref.shape[1] - tile
    g = pl.program_id(0)
    never = g < 0

    @pl.when(g == 0)
    def _():
        exta_ref[...] = jnp.zeros(exta_ref.shape, F32)
        extb_ref[...] = jnp.zeros(extb_ref.shape, F32)
        bg_ref[...] = jnp.zeros(bg_ref.shape, F32)
        ycat_ref[...] = jnp.zeros(ycat_ref.shape, ycat_ref.dtype)

    x1 = x3_ref[...] + _dot(ycat_ref[...], w_out_ref[...])

    starts = list(range(0, tile, row_chunk))
    units = []
    for j in range(n_lane):
        a = _conv_windows(exta_ref, j, [hist_a - (K_A - 1) + s for s in starts], row_chunk,
                          wa_ref, K_A)
        b = _conv_windows(extb_ref, j, [hist_b - (K_B - 1) + s for s in starts], row_chunk,
                          wb_ref, K_B)
        units.append(a + b)
    conv_a = [jnp.concatenate(u[:len(starts)], axis=0) for u in units]
    conv_b = [jnp.concatenate(u[len(starts):], axis=0) for u in units]
    y_cat = _mix(bg_ref[...], jnp.concatenate(conv_a, axis=1), jnp.concatenate(conv_b, axis=1),
                 bias_ref, ln_g_ref, ln_b_ref)
    tail_a = exta_ref[:, tile:tile + hist_a, :]
    tail_b = extb_ref[:, tile:tile + hist_b, :]

    h = _rmsnorm(x1_ref[...], g_mix_ref[...]).astype(BF16)
    in_proj = lambda k: _dot(h, w_in_ref[:, k * gw:(k + 1) * gw])
    glu_val, glu_gate = in_proj(3), in_proj(4)
    x2 = _ffn(x1, g_ffn_ref, w_gate_ref, w_up_ref, w_down_ref, ff_chunk,
              hide={c: units[j] for c, j in CONV_UNIT_BEFORE_FFN_CHUNK.items()}, never=never)
    c_gate, v, b_gate = in_proj(1), in_proj(2), in_proj(0)
    y_ref[...] = _rmsnorm(x2, g_final_ref[...])

    seq_start = (jnp.minimum(g, n_tiles - 1) % tiles_per_seq) == 0
    cv = c_gate * v
    u_b = glu_val * _sigmoid(glu_gate)
    exta_ref[:, 0:hist_a, :] = jnp.where(seq_start, 0.0, tail_a)
    extb_ref[:, 0:hist_b, :] = jnp.where(seq_start, 0.0, tail_b)
    for j in range(n_lane):
        lanes = slice(j * LANES, (j + 1) * LANES)
        exta_ref[j, hist_a:hist_a + tile, :] = cv[:, lanes]
        extb_ref[j, hist_b:hist_b + tile, :] = u_b[:, lanes]
    bg_ref[...] = b_gate
    new_a_ref[...] = cv[tile - (K_A - 1):, :]
    new_b_ref[...] = u_b[tile - (K_B - 1):, :]

    ycat_ref[...] = y_cat


def _sample_kernel(x_ref, sa_ref, sb_ref, g_mix_ref, w_in_ref, wa_ref, wb_ref, bias_ref,
                   ln_g_ref, ln_b_ref, w_out_ref, g_ffn_ref, w_gate_ref, w_up_ref, w_down_ref,
                   g_final_ref, y_ref, new_a_ref, new_b_ref, exta_ref, extb_ref,
                   *, n_seq, seq_len, seq_group, ff_chunk):
    gw = wa_ref.shape[1]
    n_lane = gw // LANES
    ra = exta_ref.shape[1] // n_seq
    rb = extb_ref.shape[1] // n_seq
    off_a = ra - seq_len - (K_A - 1)
    off_b = rb - seq_len - (K_B - 1)

    x = x_ref[...]
    b_gate, cv, u_b = _gating(x, g_mix_ref, w_in_ref, gw)
    for s in range(n_seq):
        rows = slice(s * seq_len, (s + 1) * seq_len)
        for j in range(n_lane):
            lanes = slice(j * LANES, (j + 1) * LANES)
            exta_ref[j, s * ra + off_a:s * ra + off_a + K_A - 1, :] = sa_ref[s, :, lanes]
            exta_ref[j, (s + 1) * ra - seq_len:(s + 1) * ra, :] = cv[rows, lanes]
            extb_ref[j, s * rb + off_b:s * rb + off_b + K_B - 1, :] = sb_ref[s, :, lanes]
            extb_ref[j, (s + 1) * rb - seq_len:(s + 1) * rb, :] = u_b[rows, lanes]

    conv_a, conv_b = [], []
    for j in range(n_lane):
        a, b = [], []
        for s0 in range(0, n_seq, seq_group):
            group = range(s0, s0 + seq_group)
            a += _conv_windows(exta_ref, j, [s * ra + off_a for s in group], seq_len, wa_ref, K_A)
            b += _conv_windows(extb_ref, j, [s * rb + off_b for s in group], seq_len, wb_ref, K_B)
        conv_a.append(jnp.concatenate(a, axis=0))
        conv_b.append(jnp.concatenate(b, axis=0))
    y_cat = _mix(b_gate, jnp.concatenate(conv_a, axis=1), jnp.concatenate(conv_b, axis=1),
                 bias_ref, ln_g_ref, ln_b_ref)
    x1 = x + _dot(y_cat, w_out_ref[...])
    x2 = _ffn(x1, g_ffn_ref, w_gate_ref, w_up_ref, w_down_ref, ff_chunk)
    y_ref[...] = _rmsnorm(x2, g_final_ref[...])

    for s in range(n_seq):
        for j in range(n_lane):
            lanes = slice(j * LANES, (j + 1) * LANES)
            new_a_ref[s, :, lanes] = exta_ref[j, (s + 1) * ra - (K_A - 1):(s + 1) * ra, :]
            new_b_ref[s, :, lanes] = extb_ref[j, (s + 1) * rb - (K_B - 1):(s + 1) * rb, :]


def _round_up(n, m):
    return -(-n // m) * m


def _const_spec(shape):
    zeros = (0,) * len(shape)
    return pl.BlockSpec(shape, lambda g: zeros, pipeline_mode=pl.Buffered(1))


PROMPT_TILE = 512
PROMPT_ROW_CHUNK = 64
CONV_UNIT_BEFORE_FFN_CHUNK = {1: 0, 2: 1, 3: 2, 4: 3}
FF_CHUNK = 512
SAMPLE_SEQS = 32
SAMPLE_SEQ_GROUP = 16
VMEM_LIMIT_BYTES = 56 * 1024 * 1024


def kernel(x_prompt, x_sample, state_conv_a, state_conv_b, g_mix, w_in, conv_a_w, conv_b_w,
           conv_b_bias, ln_b_g, ln_b_b, w_out, g_ffn, w_gate, w_up, w_down, g_final):
    depth = g_mix.shape[0]
    assert depth == 1, "single-layer trunk"
    bp, seq, d = x_prompt.shape
    bs, dec_seq, _ = x_sample.shape
    gw = conv_a_w.shape[-1]
    assert conv_a_w.shape[1] == K_A and conv_b_w.shape[1] == K_B
    assert dec_seq == SUBLANES and gw % LANES == 0
    assert seq % PROMPT_TILE == 0 and bs % SAMPLE_SEQS == 0

    row = lambda a: a.reshape(1, -1)
    params = (row(g_mix[0]), w_in[0].astype(BF16), conv_a_w[0], conv_b_w[0], row(conv_b_bias[0]),
              row(ln_b_g[0]), row(ln_b_b[0]), w_out[0].astype(BF16), row(g_ffn[0]),
              w_gate[0].astype(BF16), w_up[0].astype(BF16), w_down[0].astype(BF16), row(g_final))

    tile = PROMPT_TILE
    tiles_per_seq = seq // tile
    n_tiles = bp * tiles_per_seq
    hist_a = _round_up(K_A - 1, SUBLANES)
    hist_b = _round_up(K_B - 1, SUBLANES)
    n_lane = gw // LANES
    tile_1 = lambda g: jnp.minimum(g, n_tiles - 1)
    tile_3 = lambda g: jnp.maximum(g - 2, 0)
    x_rows = x_prompt.reshape(bp * seq, d)
    y_prompt, pa, pb = pl.pallas_call(
        functools.partial(_prompt_kernel, tile=tile, row_chunk=PROMPT_ROW_CHUNK,
                          n_tiles=n_tiles, tiles_per_seq=tiles_per_seq, ff_chunk=FF_CHUNK),
        grid=(n_tiles + 2,),
        in_specs=[pl.BlockSpec((tile, d), lambda g: (tile_1(g), 0)),
                  pl.BlockSpec((tile, d), lambda g: (tile_3(g), 0))]
        + [_const_spec(p.shape) for p in params],
        out_specs=[pl.BlockSpec((tile, d), lambda g: (tile_3(g), 0)),
                   pl.BlockSpec((None, K_A - 1, gw), lambda g: (tile_1(g) // tiles_per_seq, 0, 0)),
                   pl.BlockSpec((None, K_B - 1, gw), lambda g: (tile_1(g) // tiles_per_seq, 0, 0))],
        out_shape=[jax.ShapeDtypeStruct((bp * seq, d), F32),
                   jax.ShapeDtypeStruct((bp, K_A - 1, gw), F32),
                   jax.ShapeDtypeStruct((bp, K_B - 1, gw), F32)],
        scratch_shapes=[pltpu.VMEM((n_lane, hist_a + tile, LANES), F32),
                        pltpu.VMEM((n_lane, hist_b + tile, LANES), F32),
                        pltpu.VMEM((tile, gw), F32),
                        pltpu.VMEM((tile, 2 * gw), BF16)],
        compiler_params=pltpu.CompilerParams(
            dimension_semantics=("arbitrary",),
            vmem_limit_bytes=VMEM_LIMIT_BYTES),
        name="prompt_layer",
    )(x_rows, x_rows, *params)
    y_prompt = y_prompt.reshape(bp, seq, d)

    n_seq = SAMPLE_SEQS
    rows = n_seq * dec_seq
    ra = _round_up(K_A - 1, SUBLANES) + dec_seq
    rb = _round_up(K_B - 1, SUBLANES) + dec_seq
    y_sample, sa, sb = pl.pallas_call(
        functools.partial(_sample_kernel, n_seq=n_seq, seq_len=dec_seq,
                          seq_group=SAMPLE_SEQ_GROUP, ff_chunk=FF_CHUNK),
        grid=(bs // n_seq,),
        in_specs=[pl.BlockSpec((rows, d), lambda i: (i, 0)),
                  pl.BlockSpec((n_seq, K_A - 1, gw), lambda i: (i, 0, 0)),
                  pl.BlockSpec((n_seq, K_B - 1, gw), lambda i: (i, 0, 0))]
        + [_const_spec(p.shape) for p in params],
        out_specs=[pl.BlockSpec((rows, d), lambda i: (i, 0)),
                   pl.BlockSpec((n_seq, K_A - 1, gw), lambda i: (i, 0, 0)),
                   pl.BlockSpec((n_seq, K_B - 1, gw), lambda i: (i, 0, 0))],
        out_shape=[jax.ShapeDtypeStruct((bs * dec_seq, d), F32),
                   jax.ShapeDtypeStruct((bs, K_A - 1, gw), F32),
                   jax.ShapeDtypeStruct((bs, K_B - 1, gw), F32)],
        scratch_shapes=[pltpu.VMEM((n_lane, n_seq * ra, LANES), F32),
                        pltpu.VMEM((n_lane, n_seq * rb, LANES), F32)],
        compiler_params=pltpu.CompilerParams(
            dimension_semantics=("arbitrary",),
            vmem_limit_bytes=VMEM_LIMIT_BYTES),
        name="sample_layer",
    )(x_sample.reshape(bs * dec_seq, d), state_conv_a[0], state_conv_b[0], *params)

    return (y_prompt, y_sample.reshape(bs, dec_seq, d), pa[None], pb[None], sa[None], sb[None])
```

```python
import functools

import jax
import jax.numpy as jnp
from jax import lax
from jax.experimental import pallas as pl
from jax.experimental.pallas import tpu as pltpu

EPS = 1e-6
K_A = 3
K_B = 31
LANES = 128
SUBLANES = 8

F32 = jnp.float32
BF16 = jnp.bfloat16


def _rmsnorm(x, g):
    ms = jnp.mean(x * x, axis=-1, keepdims=True)
    return x * lax.rsqrt(ms + EPS) * g


def _layernorm(x, g, b):
    mu = jnp.mean(x, axis=-1, keepdims=True)
    xc = x - mu
    var = jnp.mean(xc * xc, axis=-1, keepdims=True)
    return xc * lax.rsqrt(var + EPS) * g + b


def _sigmoid(x):
    return 1.0 / (1.0 + jnp.exp(-x))


def _silu(x):
    return x * _sigmoid(x)


def _dot(a, b):
    return jnp.dot(a, b, preferred_element_type=F32)


def _conv_windows(ext_ref, j, starts, nrows, w_ref, n_taps):
    lanes = slice(j * LANES, (j + 1) * LANES)
    accs = [None] * len(starts)
    for k in range(n_taps):
        wk = w_ref[k:k + 1, lanes]
        for i, s in enumerate(starts):
            term = wk * ext_ref[j, pl.ds(s + k, nrows), :]
            accs[i] = term if accs[i] is None else accs[i] + term
    return accs


def _after(lhs, pieces, never):
    if not pieces:
        return lhs
    rows = 2 * SUBLANES
    total = None
    for p in pieces:
        for r in range(0, p.shape[0], rows):
            for c in range(0, p.shape[1], LANES):
                slab = p[r:r + rows, c:c + LANES].astype(lhs.dtype)
                total = slab if total is None else total + slab
    head = jnp.where(never, total, lhs[0:rows, 0:LANES])
    top = jnp.concatenate([head, lhs[0:rows, LANES:]], axis=1)
    return jnp.concatenate([top, lhs[rows:, :]], axis=0)


def _gating(x, g_mix_ref, w_in_ref, gw):
    h = _rmsnorm(x, g_mix_ref[...]).astype(BF16)
    b_gate, c_gate, v, glu_val, glu_gate = [
        _dot(h, w_in_ref[:, k * gw:(k + 1) * gw]) for k in range(5)]
    return b_gate, c_gate * v, glu_val * _sigmoid(glu_gate)


def _mix(b_gate, conv_a, conv_b, bias_ref, ln_g_ref, ln_b_ref):
    y_b = _silu(_layernorm(conv_b + bias_ref[...], ln_g_ref[...], ln_b_ref[...]))
    return jnp.concatenate([b_gate * conv_a, y_b], axis=-1).astype(BF16)


def _ffn(x1, g_ffn_ref, w_gate_ref, w_up_ref, w_down_ref, ff_chunk, hide=None, never=None):
    d_ff = w_gate_ref.shape[1]
    h2 = _rmsnorm(x1, g_ffn_ref[...]).astype(BF16)
    acc = None
    pending = None
    for c, c0 in enumerate(range(0, d_ff, ff_chunk)):
        cols = slice(c0, min(c0 + ff_chunk, d_ff))
        lhs = _after(h2, (hide or {}).get(c), never)
        gate = _dot(lhs, w_gate_ref[:, cols])
        up = _dot(lhs, w_up_ref[:, cols])
        if pending is not None:
            part = _dot(pending[0], w_down_ref[pending[1], :])
            acc = part if acc is None else acc + part
        pending = ((_silu(gate) * up).astype(BF16), cols)
    part = _dot(pending[0], w_down_ref[pending[1], :])
    acc = part if acc is None else acc + part
    return x1 + acc


def _prompt_kernel(x1_ref, x3_ref, g_mix_ref, w_in_ref, wa_ref, wb_ref, bias_ref, ln_g_ref,
                   ln_b_ref, w_out_ref, g_ffn_ref, w_gate_ref, w_up_ref, w_down_ref, g_final_ref,
                   y_ref, new_a_ref, new_b_ref, exta_ref, extb_ref, bg_ref, ycat_ref,
                   *, tile, row_chunk, n_tiles, tiles_per_seq, ff_chunk):
    gw = wa_ref.shape[1]
    n_lane = gw // LANES
    hist_a = exta_ref.shape[1] - tile
    hist_b = extb_ref.shape[1] - tile
    g = pl.program_id(0)
    never = g < 0

    @pl.when(g == 0)
    def _():
        exta_ref[...] = jnp.zeros(exta_ref.shape, F32)
        extb_ref[...] = jnp.zeros(extb_ref.shape, F32)
        bg_ref[...] = jnp.zeros(bg_ref.shape, F32)
        ycat_ref[...] = jnp.zeros(ycat_ref.shape, ycat_ref.dtype)

    x1 = x3_ref[...] + _dot(ycat_ref[...], w_out_ref[...])

    starts = list(range(0, tile, row_chunk))
    units = []
    for j in range(n_lane):
        a = _conv_windows(exta_ref, j, [hist_a - (K_A - 1) + s for s in starts], row_chunk,
                          wa_ref, K_A)
        b = _conv_windows(extb_ref, j, [hist_b - (K_B - 1) + s for s in starts], row_chunk,
                          wb_ref, K_B)
        units.append(a + b)
    conv_a = [jnp.concatenate(u[:len(starts)], axis=0) for u in units]
    conv_b = [jnp.concatenate(u[len(starts):], axis=0) for u in units]
    y_cat = _mix(bg_ref[...], jnp.concatenate(conv_a, axis=1), jnp.concatenate(conv_b, axis=1),
                 bias_ref, ln_g_ref, ln_b_ref)
    tail_a = exta_ref[:, tile:tile + hist_a, :]
    tail_b = extb_ref[:, tile:tile + hist_b, :]

    h = _rmsnorm(x1_ref[...], g_mix_ref[...]).astype(BF16)
    in_proj = lambda k: _dot(h, w_in_ref[:, k * gw:(k + 1) * gw])
    glu_val, glu_gate = in_proj(3), in_proj(4)
    x2 = _ffn(x1, g_ffn_ref, w_gate_ref, w_up_ref, w_down_ref, ff_chunk,
              hide={c: units[j] for c, j in CONV_UNIT_BEFORE_FFN_CHUNK.items()}, never=never)
    c_gate, v, b_gate = in_proj(1), in_proj(2), in_proj(0)
    y_ref[...] = _rmsnorm(x2, g_final_ref[...])

    seq_start = (jnp.minimum(g, n_tiles - 1) % tiles_per_seq) == 0
    cv = c_gate * v
    u_b = glu_val * _sigmoid(glu_gate)
    exta_ref[:, 0:hist_a, :] = jnp.where(seq_start, 0.0, tail_a)
    extb_ref[:, 0:hist_b, :] = jnp.where(seq_start, 0.0, tail_b)
    for j in range(n_lane):
        lanes = slice(j * LANES, (j + 1) * LANES)
        exta_ref[j, hist_a:hist_a + tile, :] = cv[:, lanes]
        extb_ref[j, hist_b:hist_b + tile, :] = u_b[:, lanes]
    bg_ref[...] = b_gate
    new_a_ref[...] = cv[tile - (K_A - 1):, :]
    new_b_ref[...] = u_b[tile - (K_B - 1):, :]

    ycat_ref[...] = y_cat


def _sample_kernel(x_ref, sa_ref, sb_ref, g_mix_ref, w_in_ref, wa_ref, wb_ref, bias_ref,
                   ln_g_ref, ln_b_ref, w_out_ref, g_ffn_ref, w_gate_ref, w_up_ref, w_down_ref,
                   g_final_ref, y_ref, new_a_ref, new_b_ref, exta_ref, extb_ref,
                   *, n_seq, seq_len, seq_group, ff_chunk):
    gw = wa_ref.shape[1]
    n_lane = gw // LANES
    ra = exta_ref.shape[1] // n_seq
    rb = extb_ref.shape[1] // n_seq
    off_a = ra - seq_len - (K_A - 1)
    off_b = rb - seq_len - (K_B - 1)

    x = x_ref[...]
    b_gate, cv, u_b = _gating(x, g_mix_ref, w_in_ref, gw)
    for j in range(n_lane):
        lanes = slice(j * LANES, (j + 1) * LANES)
        for r in range(K_B - 1):
            extb_ref[j, pl.ds(off_b + r, n_seq, stride=rb), :] = sb_ref[r, :, lanes]
        for s in range(n_seq):
            rows = slice(s * seq_len, (s + 1) * seq_len)
            exta_ref[j, s * ra + off_a:s * ra + off_a + K_A - 1, :] = sa_ref[s, :, lanes]
            exta_ref[j, (s + 1) * ra - seq_len:(s + 1) * ra, :] = cv[rows, lanes]
            extb_ref[j, (s + 1) * rb - seq_len:(s + 1) * rb, :] = u_b[rows, lanes]

    conv_a, conv_b = [], []
    for j in range(n_lane):
        a, b = [], []
        for s0 in range(0, n_seq, seq_group):
            group = range(s0, s0 + seq_group)
            a += _conv_windows(exta_ref, j, [s * ra + off_a for s in group], seq_len, wa_ref, K_A)
            b += _conv_windows(extb_ref, j, [s * rb + off_b for s in group], seq_len, wb_ref, K_B)
        conv_a.append(jnp.concatenate(a, axis=0))
        conv_b.append(jnp.concatenate(b, axis=0))
    y_cat = _mix(b_gate, jnp.concatenate(conv_a, axis=1), jnp.concatenate(conv_b, axis=1),
                 bias_ref, ln_g_ref, ln_b_ref)
    x1 = x + _dot(y_cat, w_out_ref[...])
    x2 = _ffn(x1, g_ffn_ref, w_gate_ref, w_up_ref, w_down_ref, ff_chunk)
    y_ref[...] = _rmsnorm(x2, g_final_ref[...])

    for j in range(n_lane):
        lanes = slice(j * LANES, (j + 1) * LANES)
        for s in range(n_seq):
            new_a_ref[s, :, lanes] = exta_ref[j, (s + 1) * ra - (K_A - 1):(s + 1) * ra, :]
        for r in range(K_B - 1):
            new_b_ref[r, :, lanes] = extb_ref[j, pl.ds(rb - (K_B - 1) + r, n_seq, stride=rb), :]


def _round_up(n, m):
    return -(-n // m) * m


def _const_spec(shape):
    zeros = (0,) * len(shape)
    block = shape if len(shape) == 2 else (None,) + tuple(shape[1:])
    return pl.BlockSpec(block, lambda g: zeros, pipeline_mode=pl.Buffered(1))


PROMPT_TILE = 256
PROMPT_ROW_CHUNK = 64
CONV_UNIT_BEFORE_FFN_CHUNK = {1: 0, 2: 1, 3: 2, 4: 3}
FF_CHUNK = 512
SAMPLE_SEQS = 32
SAMPLE_SEQ_GROUP = 16
VMEM_LIMIT_BYTES = 56 * 1024 * 1024


def kernel(x_prompt, x_sample, state_conv_a, state_conv_b, g_mix, w_in, conv_a_w, conv_b_w,
           conv_b_bias, ln_b_g, ln_b_b, w_out, g_ffn, w_gate, w_up, w_down, g_final):
    depth = g_mix.shape[0]
    assert depth == 1, "single-layer trunk"
    bp, seq, d = x_prompt.shape
    bs, dec_seq, _ = x_sample.shape
    gw = conv_a_w.shape[-1]
    assert conv_a_w.shape[1] == K_A and conv_b_w.shape[1] == K_B
    assert dec_seq == SUBLANES and gw % LANES == 0
    assert seq % PROMPT_TILE == 0 and bs % SAMPLE_SEQS == 0

    row = lambda a: a.reshape(1, -1)
    params = (row(g_mix[0]), w_in[0].astype(BF16), conv_a_w, conv_b_w, row(conv_b_bias[0]),
              row(ln_b_g[0]), row(ln_b_b[0]), w_out[0].astype(BF16), row(g_ffn[0]),
              w_gate[0].astype(BF16), w_up[0].astype(BF16), w_down[0].astype(BF16), row(g_final))

    tile = PROMPT_TILE
    tiles_per_seq = seq // tile
    n_tiles = bp * tiles_per_seq
    hist_a = _round_up(K_A - 1, SUBLANES)
    hist_b = _round_up(K_B - 1, SUBLANES)
    n_lane = gw // LANES
    tile_1 = lambda g: jnp.minimum(g, n_tiles - 1)
    tile_3 = lambda g: jnp.maximum(g - 2, 0)
    x_rows = x_prompt.reshape(bp * seq, d)
    y_prompt, pa, pb = pl.pallas_call(
        functools.partial(_prompt_kernel, tile=tile, row_chunk=PROMPT_ROW_CHUNK,
                          n_tiles=n_tiles, tiles_per_seq=tiles_per_seq, ff_chunk=FF_CHUNK),
        grid=(n_tiles + 2,),
        in_specs=[pl.BlockSpec((tile, d), lambda g: (tile_1(g), 0)),
                  pl.BlockSpec((tile, d), lambda g: (tile_3(g), 0))]
        + [_const_spec(p.shape) for p in params],
        out_specs=[pl.BlockSpec((tile, d), lambda g: (tile_3(g), 0)),
                   pl.BlockSpec((None, K_A - 1, gw), lambda g: (tile_1(g) // tiles_per_seq, 0, 0)),
                   pl.BlockSpec((None, K_B - 1, gw), lambda g: (tile_1(g) // tiles_per_seq, 0, 0))],
        out_shape=[jax.ShapeDtypeStruct((bp * seq, d), F32),
                   jax.ShapeDtypeStruct((bp, K_A - 1, gw), F32),
                   jax.ShapeDtypeStruct((bp, K_B - 1, gw), F32)],
        scratch_shapes=[pltpu.VMEM((n_lane, hist_a + tile, LANES), F32),
                        pltpu.VMEM((n_lane, hist_b + tile, LANES), F32),
                        pltpu.VMEM((tile, gw), F32),
                        pltpu.VMEM((tile, 2 * gw), BF16)],
        compiler_params=pltpu.CompilerParams(
            dimension_semantics=("arbitrary",),
            vmem_limit_bytes=VMEM_LIMIT_BYTES),
        name="prompt_layer",
    )(x_rows, x_rows, *params)
    y_prompt = y_prompt.reshape(bp, seq, d)

    by_row = lambda a: jnp.swapaxes(a, 0, 1)
    n_seq = SAMPLE_SEQS
    rows = n_seq * dec_seq
    ra = _round_up(K_A - 1, SUBLANES) + dec_seq
    rb = _round_up(K_B - 1, SUBLANES) + dec_seq
    y_sample, sa, sb = pl.pallas_call(
        functools.partial(_sample_kernel, n_seq=n_seq, seq_len=dec_seq,
                          seq_group=SAMPLE_SEQ_GROUP, ff_chunk=FF_CHUNK),
        grid=(bs // n_seq,),
        in_specs=[pl.BlockSpec((rows, d), lambda i: (i, 0)),
                  pl.BlockSpec((n_seq, K_A - 1, gw), lambda i: (i, 0, 0)),
                  pl.BlockSpec((K_B - 1, n_seq, gw), lambda i: (0, i, 0))]
        + [_const_spec(p.shape) for p in params],
        out_specs=[pl.BlockSpec((rows, d), lambda i: (i, 0)),
                   pl.BlockSpec((n_seq, K_A - 1, gw), lambda i: (i, 0, 0)),
                   pl.BlockSpec((K_B - 1, n_seq, gw), lambda i: (0, i, 0))],
        out_shape=[jax.ShapeDtypeStruct((bs * dec_seq, d), F32),
                   jax.ShapeDtypeStruct((bs, K_A - 1, gw), F32),
                   jax.ShapeDtypeStruct((K_B - 1, bs, gw), F32)],
        scratch_shapes=[pltpu.VMEM((n_lane, n_seq * ra, LANES), F32),
                        pltpu.VMEM((n_lane, n_seq * rb, LANES), F32)],
        compiler_params=pltpu.CompilerParams(
            dimension_semantics=("arbitrary",),
            vmem_limit_bytes=VMEM_LIMIT_BYTES),
        name="sample_layer",
    )(x_sample.reshape(bs * dec_seq, d), state_conv_a[0], by_row(state_conv_b[0]), *params)

    return (y_prompt, y_sample.reshape(bs, dec_seq, d), pa[None], pb[None],
            sa[None], by_row(sb)[None])
```

```python
import functools

import jax
import jax.numpy as jnp
from jax import lax
from jax.experimental import pallas as pl
from jax.experimental.pallas import tpu as pltpu

EPS = 1e-6
K_A = 3
K_B = 31
LANES = 128
SUBLANES = 8

F32 = jnp.float32
BF16 = jnp.bfloat16


def _rmsnorm(x, g):
    ms = jnp.mean(x * x, axis=-1, keepdims=True)
    return x * lax.rsqrt(ms + EPS) * g


def _layernorm(x, g, b):
    mu = jnp.mean(x, axis=-1, keepdims=True)
    xc = x - mu
    var = jnp.mean(xc * xc, axis=-1, keepdims=True)
    return xc * lax.rsqrt(var + EPS) * g + b


def _sigmoid(x):
    return 1.0 / (1.0 + jnp.exp(-x))


def _silu(x):
    return x * _sigmoid(x)


def _dot(a, b):
    return jnp.dot(a, b, preferred_element_type=F32)


def _conv_windows(ext_ref, j, starts, nrows, w_ref, n_taps):
    lanes = slice(j * LANES, (j + 1) * LANES)
    accs = [None] * len(starts)
    for k in range(n_taps):
        wk = w_ref[k:k + 1, lanes]
        for i, s in enumerate(starts):
            term = wk * ext_ref[j, pl.ds(s + k, nrows), :]
            accs[i] = term if accs[i] is None else accs[i] + term
    return accs


def _after(lhs, pieces, never):
    if not pieces:
        return lhs
    rows = 2 * SUBLANES
    total = None
    for p in pieces:
        for r in range(0, p.shape[0], rows):
            for c in range(0, p.shape[1], LANES):
                slab = p[r:r + rows, c:c + LANES].astype(lhs.dtype)
                total = slab if total is None else total + slab
    head = jnp.where(never, total, lhs[0:rows, 0:LANES])
    top = jnp.concatenate([head, lhs[0:rows, LANES:]], axis=1)
    return jnp.concatenate([top, lhs[rows:, :]], axis=0)


def _gating(x, g_mix_ref, w_in_ref, gw):
    h = _rmsnorm(x, g_mix_ref[...]).astype(BF16)
    b_gate, c_gate, v, glu_val, glu_gate = [
        _dot(h, w_in_ref[:, k * gw:(k + 1) * gw]) for k in range(5)]
    return b_gate, c_gate * v, glu_val * _sigmoid(glu_gate)


def _mix(b_gate, conv_a, conv_b, bias_ref, ln_g_ref, ln_b_ref):
    y_b = _silu(_layernorm(conv_b + bias_ref[...], ln_g_ref[...], ln_b_ref[...]))
    return jnp.concatenate([b_gate * conv_a, y_b], axis=-1).astype(BF16)


def _ffn(x1, g_ffn_ref, w_gate_ref, w_up_ref, w_down_ref, ff_chunk, hide=None, never=None):
    d_ff = w_gate_ref.shape[1]
    h2 = _rmsnorm(x1, g_ffn_ref[...]).astype(BF16)
    acc = None
    pending = None
    for c, c0 in enumerate(range(0, d_ff, ff_chunk)):
        cols = slice(c0, min(c0 + ff_chunk, d_ff))
        lhs = _after(h2, (hide or {}).get(c), never)
        gate = _dot(lhs, w_gate_ref[:, cols])
        up = _dot(lhs, w_up_ref[:, cols])
        if pending is not None:
            part = _dot(pending[0], w_down_ref[pending[1], :])
            acc = part if acc is None else acc + part
        pending = ((_silu(gate) * up).astype(BF16), cols)
    part = _dot(pending[0], w_down_ref[pending[1], :])
    acc = part if acc is None else acc + part
    return x1 + acc


def _prompt_kernel(x1_ref, x3_ref, g_mix_ref, w_in_ref, wa_ref, wb_ref, bias_ref, ln_g_ref,
                   ln_b_ref, w_out_ref, g_ffn_ref, w_gate_ref, w_up_ref, w_down_ref, g_final_ref,
                   y_ref, new_a_ref, new_b_ref, exta_ref, extb_ref, bg_ref, ycat_ref,
                   *, tile, row_chunk, n_tiles, tiles_per_seq, ff_chunk):
    gw = wa_ref.shape[1]
    n_lane = gw // LANES
    hist_a = exta_ref.shape[1] - tile
    hist_b = extb_ref.shape[1] - tile
    g = pl.program_id(0)
    never = g < 0

    @pl.when(g == 0)
    def _():
        exta_ref[...] = jnp.zeros(exta_ref.shape, F32)
        extb_ref[...] = jnp.zeros(extb_ref.shape, F32)
        bg_ref[...] = jnp.zeros(bg_ref.shape, F32)
        ycat_ref[...] = jnp.zeros(ycat_ref.shape, ycat_ref.dtype)

    x1 = x3_ref[...] + _dot(ycat_ref[...], w_out_ref[...])

    starts = list(range(0, tile, row_chunk))
    units = []
    for j in range(n_lane):
        a = _conv_windows(exta_ref, j, [hist_a - (K_A - 1) + s for s in starts], row_chunk,
                          wa_ref, K_A)
        b = _conv_windows(extb_ref, j, [hist_b - (K_B - 1) + s for s in starts], row_chunk,
                          wb_ref, K_B)
        units.append(a + b)
    conv_a = [jnp.concatenate(u[:len(starts)], axis=0) for u in units]
    conv_b = [jnp.concatenate(u[len(starts):], axis=0) for u in units]
    y_cat = _mix(bg_ref[...], jnp.concatenate(conv_a, axis=1), jnp.concatenate(conv_b, axis=1),
                 bias_ref, ln_g_ref, ln_b_ref)
    tail_a = exta_ref[:, tile:tile + hist_a, :]
    tail_b = extb_ref[:, tile:tile + hist_b, :]

    h = _rmsnorm(x1_ref[...], g_mix_ref[...]).astype(BF16)
    in_proj = lambda k: _dot(h, w_in_ref[:, k * gw:(k + 1) * gw])
    glu_val, glu_gate = in_proj(3), in_proj(4)
    x2 = _ffn(x1, g_ffn_ref, w_gate_ref, w_up_ref, w_down_ref, ff_chunk,
              hide={c: units[j] for c, j in CONV_UNIT_BEFORE_FFN_CHUNK.items()}, never=never)
    c_gate, v, b_gate = in_proj(1), in_proj(2), in_proj(0)
    y_ref[...] = _rmsnorm(x2, g_final_ref[...])

    seq_start = (jnp.minimum(g, n_tiles - 1) % tiles_per_seq) == 0
    cv = c_gate * v
    u_b = glu_val * _sigmoid(glu_gate)
    exta_ref[:, 0:hist_a, :] = jnp.where(seq_start, 0.0, tail_a)
    extb_ref[:, 0:hist_b, :] = jnp.where(seq_start, 0.0, tail_b)
    for j in range(n_lane):
        lanes = slice(j * LANES, (j + 1) * LANES)
        exta_ref[j, hist_a:hist_a + tile, :] = cv[:, lanes]
        extb_ref[j, hist_b:hist_b + tile, :] = u_b[:, lanes]
    bg_ref[...] = b_gate
    new_a_ref[...] = cv[tile - (K_A - 1):, :]
    new_b_ref[...] = u_b[tile - (K_B - 1):, :]

    ycat_ref[...] = y_cat


def _sample_kernel(x_ref, sa_ref, sb_ref, g_mix_ref, w_in_ref, wa_ref, wb_ref, bias_ref,
                   ln_g_ref, ln_b_ref, w_out_ref, g_ffn_ref, w_gate_ref, w_up_ref, w_down_ref,
                   g_final_ref, y_ref, new_a_ref, new_b_ref, exta_ref, extb_ref,
                   *, n_seq, seq_len, seq_group, ff_chunk):
    gw = wa_ref.shape[1]
    n_lane = gw // LANES
    ra = exta_ref.shape[1] // n_seq
    rb = extb_ref.shape[1] // n_seq
    off_a = ra - seq_len - (K_A - 1)
    off_b = rb - seq_len - (K_B - 1)

    x = x_ref[...]
    b_gate, cv, u_b = _gating(x, g_mix_ref, w_in_ref, gw)
    for j in range(n_lane):
        lanes = slice(j * LANES, (j + 1) * LANES)
        for r in range(K_B - 1):
            extb_ref[j, pl.ds(off_b + r, n_seq, stride=rb), :] = sb_ref[r, :, lanes]
        for s in range(n_seq):
            rows = slice(s * seq_len, (s + 1) * seq_len)
            exta_ref[j, s * ra + off_a:s * ra + off_a + K_A - 1, :] = sa_ref[s, :, lanes]
            exta_ref[j, (s + 1) * ra - seq_len:(s + 1) * ra, :] = cv[rows, lanes]
            extb_ref[j, (s + 1) * rb - seq_len:(s + 1) * rb, :] = u_b[rows, lanes]

    conv_a, conv_b = [], []
    for j in range(n_lane):
        a, b = [], []
        for s0 in range(0, n_seq, seq_group):
            group = range(s0, s0 + seq_group)
            a += _conv_windows(exta_ref, j, [s * ra + off_a for s in group], seq_len, wa_ref, K_A)
            b += _conv_windows(extb_ref, j, [s * rb + off_b for s in group], seq_len, wb_ref, K_B)
        conv_a.append(jnp.concatenate(a, axis=0))
        conv_b.append(jnp.concatenate(b, axis=0))
    y_cat = _mix(b_gate, jnp.concatenate(conv_a, axis=1), jnp.concatenate(conv_b, axis=1),
                 bias_ref, ln_g_ref, ln_b_ref)
    x1 = x + _dot(y_cat, w_out_ref[...])
    x2 = _ffn(x1, g_ffn_ref, w_gate_ref, w_up_ref, w_down_ref, ff_chunk)
    y_ref[...] = _rmsnorm(x2, g_final_ref[...])

    for j in range(n_lane):
        lanes = slice(j * LANES, (j + 1) * LANES)
        for s in range(n_seq):
            new_a_ref[s, :, lanes] = exta_ref[j, (s + 1) * ra - (K_A - 1):(s + 1) * ra, :]
        for r in range(K_B - 1):
            new_b_ref[r, :, lanes] = extb_ref[j, pl.ds(rb - (K_B - 1) + r, n_seq, stride=rb), :]


N_PARAMS = 13


def _layer_kernel(*refs, n_prompt_steps, prompt, sample):
    x1_ref, x3_ref, xs_ref, sa_ref, sb_ref = refs[:5]
    params = refs[5:5 + N_PARAMS]
    outs = refs[5 + N_PARAMS:11 + N_PARAMS]
    exta_ref, extb_ref, bg_ref, ycat_ref, exta_s_ref, extb_s_ref = refs[11 + N_PARAMS:]
    g = pl.program_id(0)

    @pl.when(g < n_prompt_steps)
    def _():
        _prompt_kernel(x1_ref, x3_ref, *params, *outs[:3], exta_ref, extb_ref, bg_ref, ycat_ref,
                       **prompt)

    @pl.when(g >= n_prompt_steps)
    def _():
        _sample_kernel(xs_ref, sa_ref, sb_ref, *params, *outs[3:], exta_s_ref, extb_s_ref,
                       **sample)


def _round_up(n, m):
    return -(-n // m) * m


def _const_spec(shape):
    zeros = (0,) * len(shape)
    block = shape if len(shape) == 2 else (None,) + tuple(shape[1:])
    return pl.BlockSpec(block, lambda g: zeros, pipeline_mode=pl.Buffered(1))


PROMPT_TILE = 256
PROMPT_ROW_CHUNK = 64
CONV_UNIT_BEFORE_FFN_CHUNK = {1: 0, 2: 1, 3: 2, 4: 3}
FF_CHUNK = 512
SAMPLE_SEQS = 32
SAMPLE_SEQ_GROUP = 16
VMEM_LIMIT_BYTES = 56 * 1024 * 1024


def kernel(x_prompt, x_sample, state_conv_a, state_conv_b, g_mix, w_in, conv_a_w, conv_b_w,
           conv_b_bias, ln_b_g, ln_b_b, w_out, g_ffn, w_gate, w_up, w_down, g_final):
    depth = g_mix.shape[0]
    assert depth == 1, "single-layer trunk"
    bp, seq, d = x_prompt.shape
    bs, dec_seq, _ = x_sample.shape
    gw = conv_a_w.shape[-1]
    assert conv_a_w.shape[1] == K_A and conv_b_w.shape[1] == K_B
    assert dec_seq == SUBLANES and gw % LANES == 0
    assert seq % PROMPT_TILE == 0 and bs % SAMPLE_SEQS == 0

    row = lambda a: a.reshape(1, -1)
    params = (row(g_mix[0]), w_in[0].astype(BF16), conv_a_w, conv_b_w, row(conv_b_bias[0]),
              row(ln_b_g[0]), row(ln_b_b[0]), w_out[0].astype(BF16), row(g_ffn[0]),
              w_gate[0].astype(BF16), w_up[0].astype(BF16), w_down[0].astype(BF16), row(g_final))

    tile = PROMPT_TILE
    tiles_per_seq = seq // tile
    n_tiles = bp * tiles_per_seq
    hist_a = _round_up(K_A - 1, SUBLANES)
    hist_b = _round_up(K_B - 1, SUBLANES)
    n_lane = gw // LANES
    n_prompt_steps = n_tiles + 2
    tile_1 = lambda g: jnp.minimum(g, n_tiles - 1)
    tile_3 = lambda g: jnp.clip(g - 2, 0, n_tiles - 1)
    seq_1 = lambda g: tile_1(g) // tiles_per_seq

    by_row = lambda a: jnp.swapaxes(a, 0, 1)
    n_seq = SAMPLE_SEQS
    n_sample_steps = bs // n_seq
    rows = n_seq * dec_seq
    ra = _round_up(K_A - 1, SUBLANES) + dec_seq
    rb = _round_up(K_B - 1, SUBLANES) + dec_seq
    group = lambda g: jnp.clip(g - n_prompt_steps, 0, n_sample_steps - 1)

    x_rows = x_prompt.reshape(bp * seq, d)
    y_prompt, pa, pb, y_sample, sa, sb = pl.pallas_call(
        functools.partial(
            _layer_kernel, n_prompt_steps=n_prompt_steps,
            prompt=dict(tile=tile, row_chunk=PROMPT_ROW_CHUNK, n_tiles=n_tiles,
                        tiles_per_seq=tiles_per_seq, ff_chunk=FF_CHUNK),
            sample=dict(n_seq=n_seq, seq_len=dec_seq, seq_group=SAMPLE_SEQ_GROUP,
                        ff_chunk=FF_CHUNK)),
        grid=(n_prompt_steps + n_sample_steps,),
        in_specs=[pl.BlockSpec((tile, d), lambda g: (tile_1(g), 0)),
                  pl.BlockSpec((tile, d), lambda g: (tile_3(g), 0)),
                  pl.BlockSpec((rows, d), lambda g: (group(g), 0)),
                  pl.BlockSpec((n_seq, K_A - 1, gw), lambda g: (group(g), 0, 0)),
                  pl.BlockSpec((K_B - 1, n_seq, gw), lambda g: (0, group(g), 0))]
        + [_const_spec(p.shape) for p in params],
        out_specs=[pl.BlockSpec((tile, d), lambda g: (tile_3(g), 0)),
                   pl.BlockSpec((None, K_A - 1, gw), lambda g: (seq_1(g), 0, 0)),
                   pl.BlockSpec((None, K_B - 1, gw), lambda g: (seq_1(g), 0, 0)),
                   pl.BlockSpec((rows, d), lambda g: (group(g), 0)),
                   pl.BlockSpec((n_seq, K_A - 1, gw), lambda g: (group(g), 0, 0)),
                   pl.BlockSpec((K_B - 1, n_seq, gw), lambda g: (0, group(g), 0))],
        out_shape=[jax.ShapeDtypeStruct((bp * seq, d), F32),
                   jax.ShapeDtypeStruct((bp, K_A - 1, gw), F32),
                   jax.ShapeDtypeStruct((bp, K_B - 1, gw), F32),
                   jax.ShapeDtypeStruct((bs * dec_seq, d), F32),
                   jax.ShapeDtypeStruct((bs, K_A - 1, gw), F32),
                   jax.ShapeDtypeStruct((K_B - 1, bs, gw), F32)],
        scratch_shapes=[pltpu.VMEM((n_lane, hist_a + tile, LANES), F32),
                        pltpu.VMEM((n_lane, hist_b + tile, LANES), F32),
                        pltpu.VMEM((tile, gw), F32),
                        pltpu.VMEM((tile, 2 * gw), BF16),
                        pltpu.VMEM((n_lane, n_seq * ra, LANES), F32),
                        pltpu.VMEM((n_lane, n_seq * rb, LANES), F32)],
        compiler_params=pltpu.CompilerParams(
            dimension_semantics=("arbitrary",),
            vmem_limit_bytes=VMEM_LIMIT_BYTES),
        name="layer_step",
    )(x_rows, x_rows, x_sample.reshape(bs * dec_seq, d), state_conv_a[0],
      by_row(state_conv_b[0]), *params)
    y_prompt = y_prompt.reshape(bp, seq, d)

    return (y_prompt, y_sample.reshape(bs, dec_seq, d), pa[None], pb[None],
            sa[None], by_row(sb)[None])
```

```python
import functools

import jax
import jax.numpy as jnp
from jax import lax
from jax.experimental import pallas as pl
from jax.experimental.pallas import tpu as pltpu

EPS = 1e-6
K_A = 3
K_B = 31
LANES = 128
SUBLANES = 8

F32 = jnp.float32
BF16 = jnp.bfloat16


def _rmsnorm(x, g):
    ms = jnp.mean(x * x, axis=-1, keepdims=True)
    return x * lax.rsqrt(ms + EPS) * g


def _layernorm(x, g, b):
    mu = jnp.mean(x, axis=-1, keepdims=True)
    xc = x - mu
    var = jnp.mean(xc * xc, axis=-1, keepdims=True)
    return xc * lax.rsqrt(var + EPS) * g + b


def _sigmoid(x):
    return 1.0 / (1.0 + jnp.exp(-x))


def _silu(x):
    return x * _sigmoid(x)


def _dot(a, b):
    return jnp.dot(a, b, preferred_element_type=F32)


def _conv_windows(ext_ref, j, starts, nrows, w_ref, n_taps):
    lanes = slice(j * LANES, (j + 1) * LANES)
    accs = [None] * len(starts)
    for k in range(n_taps):
        wk = w_ref[k:k + 1, lanes]
        for i, s in enumerate(starts):
            term = wk * ext_ref[j, pl.ds(s + k, nrows), :]
            accs[i] = term if accs[i] is None else accs[i] + term
    return accs


def _after(lhs, pieces, never):
    if not pieces:
        return lhs
    rows = 2 * SUBLANES
    total = None
    for p in pieces:
        for r in range(0, p.shape[0], rows):
            for c in range(0, p.shape[1], LANES):
                slab = p[r:r + rows, c:c + LANES].astype(lhs.dtype)
                total = slab if total is None else total + slab
    head = jnp.where(never, total, lhs[0:rows, 0:LANES])
    top = jnp.concatenate([head, lhs[0:rows, LANES:]], axis=1)
    return jnp.concatenate([top, lhs[rows:, :]], axis=0)


def _gating(x, g_mix_ref, w_in_ref, gw):
    h = _rmsnorm(x, g_mix_ref[...]).astype(BF16)
    b_gate, c_gate, v, glu_val, glu_gate = [
        _dot(h, w_in_ref[:, k * gw:(k + 1) * gw]) for k in range(5)]
    return b_gate, c_gate * v, glu_val * _sigmoid(glu_gate)


def _mix(b_gate, conv_a, conv_b, bias_ref, ln_g_ref, ln_b_ref):
    y_b = _silu(_layernorm(conv_b + bias_ref[...], ln_g_ref[...], ln_b_ref[...]))
    return jnp.concatenate([b_gate * conv_a, y_b], axis=-1).astype(BF16)


def _ffn(x1, g_ffn_ref, w_gate_ref, w_up_ref, w_down_ref, ff_chunk, hide=None, never=None):
    d_ff = w_gate_ref.shape[1]
    h2 = _rmsnorm(x1, g_ffn_ref[...]).astype(BF16)
    acc = None
    pending = None
    for c, c0 in enumerate(range(0, d_ff, ff_chunk)):
        cols = slice(c0, min(c0 + ff_chunk, d_ff))
        lhs = _after(h2, (hide or {}).get(c), never)
        gate = _dot(lhs, w_gate_ref[:, cols])
        up = _dot(lhs, w_up_ref[:, cols])
        if pending is not None:
            part = _dot(pending[0], w_down_ref[pending[1], :])
            acc = part if acc is None else acc + part
        pending = ((_silu(gate) * up).astype(BF16), cols)
    part = _dot(pending[0], w_down_ref[pending[1], :])
    acc = part if acc is None else acc + part
    return x1 + acc


def _prompt_kernel(x1_ref, x3_ref, g_mix_ref, w_in_ref, wa_ref, wb_ref, bias_ref, ln_g_ref,
                   ln_b_ref, w_out_ref, g_ffn_ref, w_gate_ref, w_up_ref, w_down_ref, g_final_ref,
                   y_ref, new_a_ref, new_b_ref, exta_ref, extb_ref, bg_ref, ycat_ref,
                   *, tile, row_chunk, n_tiles, tiles_per_seq, ff_chunk):
    gw = wa_ref.shape[1]
    n_lane = gw // LANES
    hist_a = exta_ref.shape[1] - tile
    hist_b = extb_ref.shape[1] - tile
    g = pl.program_id(0)
    never = g < 0

    @pl.when(g == 0)
    def _():
        exta_ref[...] = jnp.zeros(exta_ref.shape, F32)
        extb_ref[...] = jnp.zeros(extb_ref.shape, F32)
        bg_ref[...] = jnp.zeros(bg_ref.shape, F32)
        ycat_ref[...] = jnp.zeros(ycat_ref.shape, ycat_ref.dtype)

    x1 = x3_ref[...] + _dot(ycat_ref[...], w_out_ref[...])

    starts = list(range(0, tile, row_chunk))
    units = []
    for j in range(n_lane):
        a = _conv_windows(exta_ref, j, [hist_a - (K_A - 1) + s for s in starts], row_chunk,
                          wa_ref, K_A)
        b = _conv_windows(extb_ref, j, [hist_b - (K_B - 1) + s for s in starts], row_chunk,
                          wb_ref, K_B)
        units.append(a + b)
    conv_a = [jnp.concatenate(u[:len(starts)], axis=0) for u in units]
    conv_b = [jnp.concatenate(u[len(starts):], axis=0) for u in units]
    y_cat = _mix(bg_ref[...], jnp.concatenate(conv_a, axis=1), jnp.concatenate(conv_b, axis=1),
                 bias_ref, ln_g_ref, ln_b_ref)
    tail_a = exta_ref[:, tile:tile + hist_a, :]
    tail_b = extb_ref[:, tile:tile + hist_b, :]

    h = _rmsnorm(x1_ref[...], g_mix_ref[...]).astype(BF16)
    in_proj = lambda k: _dot(h, w_in_ref[:, k * gw:(k + 1) * gw])
    glu_val, glu_gate = in_proj(3), in_proj(4)
    x2 = _ffn(x1, g_ffn_ref, w_gate_ref, w_up_ref, w_down_ref, ff_chunk,
              hide={c: units[j] for c, j in CONV_UNIT_BEFORE_FFN_CHUNK.items()}, never=never)
    c_gate, v, b_gate = in_proj(1), in_proj(2), in_proj(0)
    y_ref[...] = _rmsnorm(x2, g_final_ref[...])

    seq_start = (jnp.minimum(g, n_tiles - 1) % tiles_per_seq) == 0
    cv = c_gate * v
    u_b = glu_val * _sigmoid(glu_gate)
    exta_ref[:, 0:hist_a, :] = jnp.where(seq_start, 0.0, tail_a)
    extb_ref[:, 0:hist_b, :] = jnp.where(seq_start, 0.0, tail_b)
    for j in range(n_lane):
        lanes = slice(j * LANES, (j + 1) * LANES)
        exta_ref[j, hist_a:hist_a + tile, :] = cv[:, lanes]
        extb_ref[j, hist_b:hist_b + tile, :] = u_b[:, lanes]
    bg_ref[...] = b_gate
    new_a_ref[...] = cv[tile - (K_A - 1):, :]
    new_b_ref[...] = u_b[tile - (K_B - 1):, :]

    ycat_ref[...] = y_cat


def _sample_kernel(x_ref, sa_ref, sb_ref, g_mix_ref, w_in_ref, wa_ref, wb_ref, bias_ref,
                   ln_g_ref, ln_b_ref, w_out_ref, g_ffn_ref, w_gate_ref, w_up_ref, w_down_ref,
                   g_final_ref, y_ref, new_a_ref, new_b_ref, exta_ref, extb_ref,
                   *, n_seq, seq_len, seq_group, ff_chunk):
    gw = wa_ref.shape[1]
    n_lane = gw // LANES
    ra = exta_ref.shape[1] // n_seq
    rb = extb_ref.shape[1] // n_seq
    off_a = ra - seq_len - (K_A - 1)
    off_b = rb - seq_len - (K_B - 1)

    x = x_ref[...]
    b_gate, cv, u_b = _gating(x, g_mix_ref, w_in_ref, gw)
    for j in range(n_lane):
        lanes = slice(j * LANES, (j + 1) * LANES)
        for r in range(K_B - 1):
            extb_ref[j, pl.ds(off_b + r, n_seq, stride=rb), :] = sb_ref[r, :, lanes]
        for s in range(n_seq):
            rows = slice(s * seq_len, (s + 1) * seq_len)
            exta_ref[j, s * ra + off_a:s * ra + off_a + K_A - 1, :] = sa_ref[s, :, lanes]
            exta_ref[j, (s + 1) * ra - seq_len:(s + 1) * ra, :] = cv[rows, lanes]
            extb_ref[j, (s + 1) * rb - seq_len:(s + 1) * rb, :] = u_b[rows, lanes]

    conv_a, conv_b = [], []
    for j in range(n_lane):
        a, b = [], []
        for s0 in range(0, n_seq, seq_group):
            group = range(s0, s0 + seq_group)
            a += _conv_windows(exta_ref, j, [s * ra + off_a for s in group], seq_len, wa_ref, K_A)
            b += _conv_windows(extb_ref, j, [s * rb + off_b for s in group], seq_len, wb_ref, K_B)
        conv_a.append(jnp.concatenate(a, axis=0))
        conv_b.append(jnp.concatenate(b, axis=0))
    y_cat = _mix(b_gate, jnp.concatenate(conv_a, axis=1), jnp.concatenate(conv_b, axis=1),
                 bias_ref, ln_g_ref, ln_b_ref)
    x1 = x + _dot(y_cat, w_out_ref[...])
    x2 = _ffn(x1, g_ffn_ref, w_gate_ref, w_up_ref, w_down_ref, ff_chunk)
    y_ref[...] = _rmsnorm(x2, g_final_ref[...])

    for j in range(n_lane):
        lanes = slice(j * LANES, (j + 1) * LANES)
        for s in range(n_seq):
            new_a_ref[s, :, lanes] = exta_ref[j, (s + 1) * ra - (K_A - 1):(s + 1) * ra, :]
        for r in range(K_B - 1):
            new_b_ref[r, :, lanes] = extb_ref[j, pl.ds(rb - (K_B - 1) + r, n_seq, stride=rb), :]


N_PARAMS = 13
WEIGHT_SLOTS = (1, 7, 9, 10, 11)
WEIGHT_CHUNK_ROWS = 128


def _convert_weights(w_hbm_refs, w_vmem_refs, stage_ref, sem_ref):
    chunk = stage_ref.shape[1]
    plan, slot = [], 0
    for src, dst in zip(w_hbm_refs, w_vmem_refs):
        n = src.shape[0] // chunk
        plan.append((src, dst, n, slot))
        slot = (slot + n) % 2

    def fetch(src, i, slot):
        return pltpu.make_async_copy(src.at[pl.ds(i * chunk, chunk), :],
                                     stage_ref.at[slot, :, pl.ds(0, src.shape[1])],
                                     sem_ref.at[slot])

    fetch(plan[0][0], 0, plan[0][3]).start()
    for idx, (src, dst, n, slot0) in enumerate(plan):
        nxt = plan[idx + 1] if idx + 1 < len(plan) else None

        def body(i, carry, src=src, dst=dst, n=n, slot0=slot0, nxt=nxt):
            slot = (slot0 + i) % 2

            @pl.when(i + 1 < n)
            def _():
                fetch(src, i + 1, 1 - slot).start()

            if nxt is not None:
                @pl.when(i + 1 == n)
                def _():
                    fetch(nxt[0], 0, nxt[3]).start()

            fetch(src, i, slot).wait()
            row0 = pl.multiple_of(i * chunk, chunk)
            dst[pl.ds(row0, chunk), :] = stage_ref[slot, :, 0:src.shape[1]].astype(BF16)
            return carry

        lax.fori_loop(0, n, body, 0)


def _prompt_entry(*refs, n_steps, step):
    n_w = len(WEIGHT_SLOTS)
    x1_ref, x3_ref = refs[:2]
    params = list(refs[2:2 + N_PARAMS - n_w])
    w_hbm = refs[2 + N_PARAMS - n_w:2 + N_PARAMS]
    outs = refs[2 + N_PARAMS:5 + N_PARAMS]
    w_out_hbm = refs[5 + N_PARAMS:5 + N_PARAMS + n_w]
    scratch = refs[5 + N_PARAMS + n_w:]
    exta_ref, extb_ref, bg_ref, ycat_ref = scratch[:4]
    w_vmem = scratch[4:4 + n_w]
    stage_ref, sem_ref, out_sem_ref = scratch[4 + n_w:]
    g = pl.program_id(0)

    def copy_out(k):
        return pltpu.make_async_copy(w_vmem[k], w_out_hbm[k], out_sem_ref.at[k])

    @pl.when(g == 0)
    def _():
        _convert_weights(w_hbm, w_vmem, stage_ref, sem_ref)
        for k in range(n_w):
            copy_out(k).start()

    for slot, w in zip(WEIGHT_SLOTS, w_vmem):
        params.insert(slot, w)
    _prompt_kernel(x1_ref, x3_ref, *params, *outs, exta_ref, extb_ref, bg_ref, ycat_ref, **step)

    @pl.when(g == n_steps - 1)
    def _():
        for k in range(n_w):
            copy_out(k).wait()


def _round_up(n, m):
    return -(-n // m) * m


def _const_spec(shape):
    zeros = (0,) * len(shape)
    block = shape if len(shape) == 2 else (None,) + tuple(shape[1:])
    return pl.BlockSpec(block, lambda g: zeros, pipeline_mode=pl.Buffered(1))


PROMPT_TILE = 256
PROMPT_ROW_CHUNK = 64
CONV_UNIT_BEFORE_FFN_CHUNK = {1: 0, 2: 1, 3: 2, 4: 3}
FF_CHUNK = 512
SAMPLE_SEQS = 32
SAMPLE_SEQ_GROUP = 16
VMEM_LIMIT_BYTES = 56 * 1024 * 1024


def kernel(x_prompt, x_sample, state_conv_a, state_conv_b, g_mix, w_in, conv_a_w, conv_b_w,
           conv_b_bias, ln_b_g, ln_b_b, w_out, g_ffn, w_gate, w_up, w_down, g_final):
    depth = g_mix.shape[0]
    assert depth == 1, "single-layer trunk"
    bp, seq, d = x_prompt.shape
    bs, dec_seq, _ = x_sample.shape
    gw = conv_a_w.shape[-1]
    assert conv_a_w.shape[1] == K_A and conv_b_w.shape[1] == K_B
    assert dec_seq == SUBLANES and gw % LANES == 0
    assert seq % PROMPT_TILE == 0 and bs % SAMPLE_SEQS == 0

    row = lambda a: a.reshape(1, -1)
    small = (row(g_mix[0]), conv_a_w, conv_b_w, row(conv_b_bias[0]), row(ln_b_g[0]),
             row(ln_b_b[0]), row(g_ffn[0]), row(g_final))
    weights = (w_in[0], w_out[0], w_gate[0], w_up[0], w_down[0])
    assert len(small) + len(weights) == N_PARAMS
    assert all(w.shape[0] % WEIGHT_CHUNK_ROWS == 0 for w in weights)

    tile = PROMPT_TILE
    tiles_per_seq = seq // tile
    n_tiles = bp * tiles_per_seq
    n_steps = n_tiles + 2
    hist_a = _round_up(K_A - 1, SUBLANES)
    hist_b = _round_up(K_B - 1, SUBLANES)
    n_lane = gw // LANES
    tile_1 = lambda g: jnp.minimum(g, n_tiles - 1)
    tile_3 = lambda g: jnp.maximum(g - 2, 0)
    any_spec = pl.BlockSpec(memory_space=pl.ANY)
    x_rows = x_prompt.reshape(bp * seq, d)
    y_prompt, pa, pb, *weights_bf16 = pl.pallas_call(
        functools.partial(
            _prompt_entry, n_steps=n_steps,
            step=dict(tile=tile, row_chunk=PROMPT_ROW_CHUNK, n_tiles=n_tiles,
                      tiles_per_seq=tiles_per_seq, ff_chunk=FF_CHUNK)),
        grid=(n_steps,),
        in_specs=[pl.BlockSpec((tile, d), lambda g: (tile_1(g), 0)),
                  pl.BlockSpec((tile, d), lambda g: (tile_3(g), 0))]
        + [_const_spec(p.shape) for p in small] + [any_spec] * len(weights),
        out_specs=[pl.BlockSpec((tile, d), lambda g: (tile_3(g), 0)),
                   pl.BlockSpec((None, K_A - 1, gw), lambda g: (tile_1(g) // tiles_per_seq, 0, 0)),
                   pl.BlockSpec((None, K_B - 1, gw), lambda g: (tile_1(g) // tiles_per_seq, 0, 0))]
        + [any_spec] * len(weights),
        out_shape=[jax.ShapeDtypeStruct((bp * seq, d), F32),
                   jax.ShapeDtypeStruct((bp, K_A - 1, gw), F32),
                   jax.ShapeDtypeStruct((bp, K_B - 1, gw), F32)]
        + [jax.ShapeDtypeStruct(w.shape, BF16) for w in weights],
        scratch_shapes=[pltpu.VMEM((n_lane, hist_a + tile, LANES), F32),
                        pltpu.VMEM((n_lane, hist_b + tile, LANES), F32),
                        pltpu.VMEM((tile, gw), F32),
                        pltpu.VMEM((tile, 2 * gw), BF16)]
        + [pltpu.VMEM(w.shape, BF16) for w in weights]
        + [pltpu.VMEM((2, WEIGHT_CHUNK_ROWS, max(w.shape[1] for w in weights)), F32),
           pltpu.SemaphoreType.DMA((2,)),
           pltpu.SemaphoreType.DMA((len(weights),))],
        compiler_params=pltpu.CompilerParams(
            dimension_semantics=("arbitrary",),
            vmem_limit_bytes=VMEM_LIMIT_BYTES),
        name="prompt_layer",
    )(x_rows, x_rows, *small, *weights)
    y_prompt = y_prompt.reshape(bp, seq, d)

    params = list(small)
    for slot, w in zip(WEIGHT_SLOTS, weights_bf16):
        params.insert(slot, w)
    by_row = lambda a: jnp.swapaxes(a, 0, 1)
    n_seq = SAMPLE_SEQS
    rows = n_seq * dec_seq
    ra = _round_up(K_A - 1, SUBLANES) + dec_seq
    rb = _round_up(K_B - 1, SUBLANES) + dec_seq
    y_sample, sa, sb = pl.pallas_call(
        functools.partial(_sample_kernel, n_seq=n_seq, seq_len=dec_seq,
                          seq_group=SAMPLE_SEQ_GROUP, ff_chunk=FF_CHUNK),
        grid=(bs // n_seq,),
        in_specs=[pl.BlockSpec((rows, d), lambda i: (i, 0)),
                  pl.BlockSpec((n_seq, K_A - 1, gw), lambda i: (i, 0, 0)),
                  pl.BlockSpec((K_B - 1, n_seq, gw), lambda i: (0, i, 0))]
        + [_const_spec(p.shape) for p in params],
        out_specs=[pl.BlockSpec((rows, d), lambda i: (i, 0)),
                   pl.BlockSpec((n_seq, K_A - 1, gw), lambda i: (i, 0, 0)),
                   pl.BlockSpec((K_B - 1, n_seq, gw), lambda i: (0, i, 0))],
        out_shape=[jax.ShapeDtypeStruct((bs * dec_seq, d), F32),
                   jax.ShapeDtypeStruct((bs, K_A - 1, gw), F32),
                   jax.ShapeDtypeStruct((K_B - 1, bs, gw), F32)],
        scratch_shapes=[pltpu.VMEM((n_lane, n_seq * ra, LANES), F32),
                        pltpu.VMEM((n_lane, n_seq * rb, LANES), F32)],
        compiler_params=pltpu.CompilerParams(
            dimension_semantics=("arbitrary",),
            vmem_limit_bytes=VMEM_LIMIT_BYTES),
        name="sample_layer",
    )(x_sample.reshape(bs * dec_seq, d), state_conv_a[0], by_row(state_conv_b[0]), *params)

    return (y_prompt, y_sample.reshape(bs, dec_seq, d), pa[None], pb[None],
            sa[None], by_row(sb)[None])
```

```python
import functools

import jax
import jax.numpy as jnp
from jax import lax
from jax.experimental import pallas as pl
from jax.experimental.pallas import tpu as pltpu

EPS = 1e-6
K_A = 3
K_B = 31
LANES = 128
SUBLANES = 8

F32 = jnp.float32
BF16 = jnp.bfloat16


def _rmsnorm(x, g):
    ms = jnp.mean(x * x, axis=-1, keepdims=True)
    return x * lax.rsqrt(ms + EPS) * g


def _layernorm(x, g, b):
    mu = jnp.mean(x, axis=-1, keepdims=True)
    xc = x - mu
    var = jnp.mean(xc * xc, axis=-1, keepdims=True)
    return xc * lax.rsqrt(var + EPS) * g + b


def _sigmoid(x):
    return 1.0 / (1.0 + jnp.exp(-x))


def _silu(x):
    return x * _sigmoid(x)


def _dot(a, b):
    return jnp.dot(a, b, preferred_element_type=F32)


def _conv_windows(ext_ref, j, starts, nrows, w_ref, n_taps):
    lanes = slice(j * LANES, (j + 1) * LANES)
    accs = [None] * len(starts)
    for k in range(n_taps):
        wk = w_ref[k:k + 1, lanes]
        for i, s in enumerate(starts):
            term = wk * ext_ref[j, pl.ds(s + k, nrows), :]
            accs[i] = term if accs[i] is None else accs[i] + term
    return accs


def _after(lhs, pieces, never):
    if not pieces:
        return lhs
    rows = 2 * SUBLANES
    total = None
    for p in pieces:
        for r in range(0, p.shape[0], rows):
            for c in range(0, p.shape[1], LANES):
                slab = p[r:r + rows, c:c + LANES].astype(lhs.dtype)
                total = slab if total is None else total + slab
    head = jnp.where(never, total, lhs[0:rows, 0:LANES])
    top = jnp.concatenate([head, lhs[0:rows, LANES:]], axis=1)
    return jnp.concatenate([top, lhs[rows:, :]], axis=0)


def _gating(x, g_mix_ref, w_in_ref, gw):
    h = _rmsnorm(x, g_mix_ref[...]).astype(BF16)
    b_gate, c_gate, v, glu_val, glu_gate = [
        _dot(h, w_in_ref[:, k * gw:(k + 1) * gw]) for k in range(5)]
    return b_gate, c_gate * v, glu_val * _sigmoid(glu_gate)


def _mix(b_gate, conv_a, conv_b, bias_ref, ln_g_ref, ln_b_ref):
    y_b = _silu(_layernorm(conv_b + bias_ref[...], ln_g_ref[...], ln_b_ref[...]))
    return jnp.concatenate([b_gate * conv_a, y_b], axis=-1).astype(BF16)


def _ffn(x1, g_ffn_ref, w_gate_ref, w_up_ref, w_down_ref, ff_chunk, hide=None, never=None):
    d_ff = w_gate_ref.shape[1]
    h2 = _rmsnorm(x1, g_ffn_ref[...]).astype(BF16)
    acc = None
    pending = None
    for c, c0 in enumerate(range(0, d_ff, ff_chunk)):
        cols = slice(c0, min(c0 + ff_chunk, d_ff))
        lhs = _after(h2, (hide or {}).get(c), never)
        gate = _dot(lhs, w_gate_ref[:, cols])
        up = _dot(lhs, w_up_ref[:, cols])
        if pending is not None:
            part = _dot(pending[0], w_down_ref[pending[1], :])
            acc = part if acc is None else acc + part
        pending = ((_silu(gate) * up).astype(BF16), cols)
    part = _dot(pending[0], w_down_ref[pending[1], :])
    acc = part if acc is None else acc + part
    return x1 + acc


def _prompt_kernel(x1_ref, x3_ref, g_mix_ref, w_in_ref, wa_ref, wb_ref, bias_ref, ln_g_ref,
                   ln_b_ref, w_out_ref, g_ffn_ref, w_gate_ref, w_up_ref, w_down_ref, g_final_ref,
                   y_ref, new_a_ref, new_b_ref, exta_ref, extb_ref, bg_ref, ycat_ref,
                   *, tile, row_chunk, n_tiles, tiles_per_seq, ff_chunk):
    gw = wa_ref.shape[1]
    n_lane = gw // LANES
    hist_a = exta_ref.shape[1] - tile
    hist_b = extb_ref.shape[1] - tile
    g = pl.program_id(0)
    never = g < 0

    @pl.when(g == 0)
    def _():
        exta_ref[...] = jnp.zeros(exta_ref.shape, F32)
        extb_ref[...] = jnp.zeros(extb_ref.shape, F32)
        bg_ref[...] = jnp.zeros(bg_ref.shape, F32)
        ycat_ref[...] = jnp.zeros(ycat_ref.shape, ycat_ref.dtype)

    x1 = x3_ref[...] + _dot(ycat_ref[...], w_out_ref[...])

    starts = list(range(0, tile, row_chunk))
    units = []
    for j in range(n_lane):
        a = _conv_windows(exta_ref, j, [hist_a - (K_A - 1) + s for s in starts], row_chunk,
                          wa_ref, K_A)
        b = _conv_windows(extb_ref, j, [hist_b - (K_B - 1) + s for s in starts], row_chunk,
                          wb_ref, K_B)
        units.append(a + b)
    conv_a = [jnp.concatenate(u[:len(starts)], axis=0) for u in units]
    conv_b = [jnp.concatenate(u[len(starts):], axis=0) for u in units]
    y_cat = _mix(bg_ref[...], jnp.concatenate(conv_a, axis=1), jnp.concatenate(conv_b, axis=1),
                 bias_ref, ln_g_ref, ln_b_ref)
    tail_a = exta_ref[:, tile:tile + hist_a, :]
    tail_b = extb_ref[:, tile:tile + hist_b, :]

    h = _rmsnorm(x1_ref[...], g_mix_ref[...]).astype(BF16)
    in_proj = lambda k: _dot(h, w_in_ref[:, k * gw:(k + 1) * gw])
    glu_val, glu_gate = in_proj(3), in_proj(4)
    x2 = _ffn(x1, g_ffn_ref, w_gate_ref, w_up_ref, w_down_ref, ff_chunk,
              hide={c: units[j] for c, j in CONV_UNIT_BEFORE_FFN_CHUNK.items()}, never=never)
    c_gate, v, b_gate = in_proj(1), in_proj(2), in_proj(0)
    y_ref[...] = _rmsnorm(x2, g_final_ref[...])

    seq_start = (jnp.minimum(g, n_tiles - 1) % tiles_per_seq) == 0
    cv = c_gate * v
    u_b = glu_val * _sigmoid(glu_gate)
    exta_ref[:, 0:hist_a, :] = jnp.where(seq_start, 0.0, tail_a)
    extb_ref[:, 0:hist_b, :] = jnp.where(seq_start, 0.0, tail_b)
    for j in range(n_lane):
        lanes = slice(j * LANES, (j + 1) * LANES)
        exta_ref[j, hist_a:hist_a + tile, :] = cv[:, lanes]
        extb_ref[j, hist_b:hist_b + tile, :] = u_b[:, lanes]
    bg_ref[...] = b_gate
    new_a_ref[...] = cv[tile - (K_A - 1):, :]
    new_b_ref[...] = u_b[tile - (K_B - 1):, :]

    ycat_ref[...] = y_cat


def _sample_kernel(x_ref, sa_ref, sb_ref, g_mix_ref, w_in_ref, wa_ref, wb_ref, bias_ref,
                   ln_g_ref, ln_b_ref, w_out_ref, g_ffn_ref, w_gate_ref, w_up_ref, w_down_ref,
                   g_final_ref, y_ref, new_a_ref, new_b_ref, exta_ref, extb_ref,
                   *, n_seq, seq_len, seq_group, ff_chunk):
    gw = wa_ref.shape[1]
    n_lane = gw // LANES
    ra = exta_ref.shape[1] // n_seq
    rb = extb_ref.shape[1] // n_seq
    off_a = ra - seq_len - (K_A - 1)
    off_b = rb - seq_len - (K_B - 1)

    x = x_ref[...]
    b_gate, cv, u_b = _gating(x, g_mix_ref, w_in_ref, gw)
    for j in range(n_lane):
        lanes = slice(j * LANES, (j + 1) * LANES)
        for r in range(K_B - 1):
            extb_ref[j, pl.ds(off_b + r, n_seq, stride=rb), :] = sb_ref[r, :, lanes]
        for s in range(n_seq):
            rows = slice(s * seq_len, (s + 1) * seq_len)
            exta_ref[j, s * ra + off_a:s * ra + off_a + K_A - 1, :] = sa_ref[s, :, lanes]
            exta_ref[j, (s + 1) * ra - seq_len:(s + 1) * ra, :] = cv[rows, lanes]
            extb_ref[j, (s + 1) * rb - seq_len:(s + 1) * rb, :] = u_b[rows, lanes]

    conv_a, conv_b = [], []
    for j in range(n_lane):
        a, b = [], []
        for s0 in range(0, n_seq, seq_group):
            group = range(s0, s0 + seq_group)
            a += _conv_windows(exta_ref, j, [s * ra + off_a for s in group], seq_len, wa_ref, K_A)
            b += _conv_windows(extb_ref, j, [s * rb + off_b for s in group], seq_len, wb_ref, K_B)
        conv_a.append(jnp.concatenate(a, axis=0))
        conv_b.append(jnp.concatenate(b, axis=0))
    y_cat = _mix(b_gate, jnp.concatenate(conv_a, axis=1), jnp.concatenate(conv_b, axis=1),
                 bias_ref, ln_g_ref, ln_b_ref)
    x1 = x + _dot(y_cat, w_out_ref[...])
    x2 = _ffn(x1, g_ffn_ref, w_gate_ref, w_up_ref, w_down_ref, ff_chunk)
    y_ref[...] = _rmsnorm(x2, g_final_ref[...])

    for j in range(n_lane):
        lanes = slice(j * LANES, (j + 1) * LANES)
        for s in range(n_seq):
            new_a_ref[s, :, lanes] = exta_ref[j, (s + 1) * ra - (K_A - 1):(s + 1) * ra, :]
        for r in range(K_B - 1):
            new_b_ref[r, :, lanes] = extb_ref[j, pl.ds(rb - (K_B - 1) + r, n_seq, stride=rb), :]


N_PARAMS = 13
WEIGHT_SLOTS = (1, 7, 9, 10, 11)
WEIGHT_CHUNK_ROWS = 256
WEIGHT_RING = 4


def _convert_weights(w_hbm_refs, w_vmem_refs, stage_ref, sem_ref):
    ring, chunk = stage_ref.shape[0], stage_ref.shape[1]
    pieces = [(src, dst, r0) for src, dst in zip(w_hbm_refs, w_vmem_refs)
              for r0 in range(0, src.shape[0], chunk)]

    def fetch(p):
        src, _, r0 = pieces[p]
        return pltpu.make_async_copy(src.at[r0:r0 + chunk, :],
                                     stage_ref.at[p % ring, :, 0:src.shape[1]],
                                     sem_ref.at[p % ring])

    for p in range(min(ring - 1, len(pieces))):
        fetch(p).start()
    for p, (src, dst, r0) in enumerate(pieces):
        if p + ring - 1 < len(pieces):
            fetch(p + ring - 1).start()
        fetch(p).wait()
        dst[r0:r0 + chunk, :] = stage_ref[p % ring, :, 0:src.shape[1]].astype(BF16)


def _prompt_entry(*refs, n_steps, step):
    n_w = len(WEIGHT_SLOTS)
    x1_ref, x3_ref = refs[:2]
    params = list(refs[2:2 + N_PARAMS - n_w])
    w_hbm = refs[2 + N_PARAMS - n_w:2 + N_PARAMS]
    outs = refs[2 + N_PARAMS:5 + N_PARAMS]
    w_out_hbm = refs[5 + N_PARAMS:5 + N_PARAMS + n_w]
    scratch = refs[5 + N_PARAMS + n_w:]
    exta_ref, extb_ref, bg_ref, ycat_ref = scratch[:4]
    w_vmem = scratch[4:4 + n_w]
    stage_ref, sem_ref, out_sem_ref = scratch[4 + n_w:]
    g = pl.program_id(0)

    def copy_out(k):
        return pltpu.make_async_copy(w_vmem[k], w_out_hbm[k], out_sem_ref.at[k])

    @pl.when(g == 0)
    def _():
        _convert_weights(w_hbm, w_vmem, stage_ref, sem_ref)
        for k in range(n_w):
            copy_out(k).start()

    for slot, w in zip(WEIGHT_SLOTS, w_vmem):
        params.insert(slot, w)
    _prompt_kernel(x1_ref, x3_ref, *params, *outs, exta_ref, extb_ref, bg_ref, ycat_ref, **step)

    @pl.when(g == n_steps - 1)
    def _():
        for k in range(n_w):
            copy_out(k).wait()


def _round_up(n, m):
    return -(-n // m) * m


def _const_spec(shape):
    zeros = (0,) * len(shape)
    block = shape if len(shape) == 2 else (None,) + tuple(shape[1:])
    return pl.BlockSpec(block, lambda g: zeros, pipeline_mode=pl.Buffered(1))


PROMPT_TILE = 256
PROMPT_ROW_CHUNK = 64
CONV_UNIT_BEFORE_FFN_CHUNK = {1: 0, 2: 1, 3: 2, 4: 3}
FF_CHUNK = 512
SAMPLE_SEQS = 32
SAMPLE_SEQ_GROUP = 16
VMEM_LIMIT_BYTES = 56 * 1024 * 1024


def kernel(x_prompt, x_sample, state_conv_a, state_conv_b, g_mix, w_in, conv_a_w, conv_b_w,
           conv_b_bias, ln_b_g, ln_b_b, w_out, g_ffn, w_gate, w_up, w_down, g_final):
    depth = g_mix.shape[0]
    assert depth == 1, "single-layer trunk"
    bp, seq, d = x_prompt.shape
    bs, dec_seq, _ = x_sample.shape
    gw = conv_a_w.shape[-1]
    assert conv_a_w.shape[1] == K_A and conv_b_w.shape[1] == K_B
    assert dec_seq == SUBLANES and gw % LANES == 0
    assert seq % PROMPT_TILE == 0 and bs % SAMPLE_SEQS == 0

    row = lambda a: a.reshape(1, -1)
    small = (row(g_mix[0]), conv_a_w, conv_b_w, row(conv_b_bias[0]), row(ln_b_g[0]),
             row(ln_b_b[0]), row(g_ffn[0]), row(g_final))
    weights = (w_in[0], w_out[0], w_gate[0], w_up[0], w_down[0])
    assert len(small) + len(weights) == N_PARAMS
    assert all(w.shape[0] % WEIGHT_CHUNK_ROWS == 0 for w in weights)

    tile = PROMPT_TILE
    tiles_per_seq = seq // tile
    n_tiles = bp * tiles_per_seq
    n_steps = n_tiles + 2
    hist_a = _round_up(K_A - 1, SUBLANES)
    hist_b = _round_up(K_B - 1, SUBLANES)
    n_lane = gw // LANES
    tile_1 = lambda g: jnp.minimum(g, n_tiles - 1)
    tile_3 = lambda g: jnp.maximum(g - 2, 0)
    any_spec = pl.BlockSpec(memory_space=pl.ANY)
    x_rows = x_prompt.reshape(bp * seq, d)
    y_prompt, pa, pb, *weights_bf16 = pl.pallas_call(
        functools.partial(
            _prompt_entry, n_steps=n_steps,
            step=dict(tile=tile, row_chunk=PROMPT_ROW_CHUNK, n_tiles=n_tiles,
                      tiles_per_seq=tiles_per_seq, ff_chunk=FF_CHUNK)),
        grid=(n_steps,),
        in_specs=[pl.BlockSpec((tile, d), lambda g: (tile_1(g), 0)),
                  pl.BlockSpec((tile, d), lambda g: (tile_3(g), 0))]
        + [_const_spec(p.shape) for p in small] + [any_spec] * len(weights),
        out_specs=[pl.BlockSpec((tile, d), lambda g: (tile_3(g), 0)),
                   pl.BlockSpec((None, K_A - 1, gw), lambda g: (tile_1(g) // tiles_per_seq, 0, 0)),
                   pl.BlockSpec((None, K_B - 1, gw), lambda g: (tile_1(g) // tiles_per_seq, 0, 0))]
        + [any_spec] * len(weights),
        out_shape=[jax.ShapeDtypeStruct((bp * seq, d), F32),
                   jax.ShapeDtypeStruct((bp, K_A - 1, gw), F32),
                   jax.ShapeDtypeStruct((bp, K_B - 1, gw), F32)]
        + [jax.ShapeDtypeStruct(w.shape, BF16) for w in weights],
        scratch_shapes=[pltpu.VMEM((n_lane, hist_a + tile, LANES), F32),
                        pltpu.VMEM((n_lane, hist_b + tile, LANES), F32),
                        pltpu.VMEM((tile, gw), F32),
                        pltpu.VMEM((tile, 2 * gw), BF16)]
        + [pltpu.VMEM(w.shape, BF16) for w in weights]
        + [pltpu.VMEM((WEIGHT_RING, WEIGHT_CHUNK_ROWS, max(w.shape[1] for w in weights)), F32),
           pltpu.SemaphoreType.DMA((WEIGHT_RING,)),
           pltpu.SemaphoreType.DMA((len(weights),))],
        compiler_params=pltpu.CompilerParams(
            dimension_semantics=("arbitrary",),
            vmem_limit_bytes=VMEM_LIMIT_BYTES),
        name="prompt_layer",
    )(x_rows, x_rows, *small, *weights)
    y_prompt = y_prompt.reshape(bp, seq, d)

    params = list(small)
    for slot, w in zip(WEIGHT_SLOTS, weights_bf16):
        params.insert(slot, w)
    by_row = lambda a: jnp.swapaxes(a, 0, 1)
    n_seq = SAMPLE_SEQS
    rows = n_seq * dec_seq
    ra = _round_up(K_A - 1, SUBLANES) + dec_seq
    rb = _round_up(K_B - 1, SUBLANES) + dec_seq
    y_sample, sa, sb = pl.pallas_call(
        functools.partial(_sample_kernel, n_seq=n_seq, seq_len=dec_seq,
                          seq_group=SAMPLE_SEQ_GROUP, ff_chunk=FF_CHUNK),
        grid=(bs // n_seq,),
        in_specs=[pl.BlockSpec((rows, d), lambda i: (i, 0)),
                  pl.BlockSpec((n_seq, K_A - 1, gw), lambda i: (i, 0, 0)),
                  pl.BlockSpec((K_B - 1, n_seq, gw), lambda i: (0, i, 0))]
        + [_const_spec(p.shape) for p in params],
        out_specs=[pl.BlockSpec((rows, d), lambda i: (i, 0)),
                   pl.BlockSpec((n_seq, K_A - 1, gw), lambda i: (i, 0, 0)),
                   pl.BlockSpec((K_B - 1, n_seq, gw), lambda i: (0, i, 0))],
        out_shape=[jax.ShapeDtypeStruct((bs * dec_seq, d), F32),
                   jax.ShapeDtypeStruct((bs, K_A - 1, gw), F32),
                   jax.ShapeDtypeStruct((K_B - 1, bs, gw), F32)],
        scratch_shapes=[pltpu.VMEM((n_lane, n_seq * ra, LANES), F32),
                        pltpu.VMEM((n_lane, n_seq * rb, LANES), F32)],
        compiler_params=pltpu.CompilerParams(
            dimension_semantics=("arbitrary",),
            vmem_limit_bytes=VMEM_LIMIT_BYTES),
        name="sample_layer",
    )(x_sample.reshape(bs * dec_seq, d), state_conv_a[0], by_row(state_conv_b[0]), *params)

    return (y_prompt, y_sample.reshape(bs, dec_seq, d), pa[None], pb[None],
            sa[None], by_row(sb)[None])
```

```python
import functools

import jax
import jax.numpy as jnp
from jax import lax
from jax.experimental import pallas as pl
from jax.experimental.pallas import tpu as pltpu

EPS = 1e-6
K_A = 3
K_B = 31
LANES = 128
SUBLANES = 8

F32 = jnp.float32
BF16 = jnp.bfloat16


def _rmsnorm(x, g):
    ms = jnp.mean(x * x, axis=-1, keepdims=True)
    return x * lax.rsqrt(ms + EPS) * g


def _layernorm(x, g, b):
    mu = jnp.mean(x, axis=-1, keepdims=True)
    xc = x - mu
    var = jnp.mean(xc * xc, axis=-1, keepdims=True)
    return xc * lax.rsqrt(var + EPS) * g + b


def _sigmoid(x):
    return 1.0 / (1.0 + jnp.exp(-x))


def _silu(x):
    return x * _sigmoid(x)


def _dot(a, b):
    return jnp.dot(a, b, preferred_element_type=F32)


def _conv_windows(ext_ref, j, starts, nrows, w_ref, n_taps):
    lanes = slice(j * LANES, (j + 1) * LANES)
    accs = [None] * len(starts)
    for k in range(n_taps):
        wk = w_ref[k:k + 1, lanes]
        for i, s in enumerate(starts):
            term = wk * ext_ref[j, pl.ds(s + k, nrows), :]
            accs[i] = term if accs[i] is None else accs[i] + term
    return accs


def _after(lhs, pieces, never):
    if not pieces:
        return lhs
    rows = 2 * SUBLANES
    total = None
    for p in pieces:
        for r in range(0, p.shape[0], rows):
            for c in range(0, p.shape[1], LANES):
                slab = p[r:r + rows, c:c + LANES].astype(lhs.dtype)
                total = slab if total is None else total + slab
    head = jnp.where(never, total, lhs[0:rows, 0:LANES])
    top = jnp.concatenate([head, lhs[0:rows, LANES:]], axis=1)
    return jnp.concatenate([top, lhs[rows:, :]], axis=0)


def _gating(x, g_mix_ref, w_in_ref, gw):
    h = _rmsnorm(x, g_mix_ref[...]).astype(BF16)
    b_gate, c_gate, v, glu_val, glu_gate = [
        _dot(h, w_in_ref[:, k * gw:(k + 1) * gw]) for k in range(5)]
    return b_gate, c_gate * v, glu_val * _sigmoid(glu_gate)


def _mix(b_gate, conv_a, conv_b, bias_ref, ln_g_ref, ln_b_ref):
    y_b = _silu(_layernorm(conv_b + bias_ref[...], ln_g_ref[...], ln_b_ref[...]))
    return jnp.concatenate([b_gate * conv_a, y_b], axis=-1).astype(BF16)


def _ffn(x1, g_ffn_ref, w_gate_ref, w_up_ref, w_down_ref, ff_chunk, hide=None, never=None):
    d_ff = w_gate_ref.shape[1]
    h2 = _rmsnorm(x1, g_ffn_ref[...]).astype(BF16)
    acc = None
    pending = None
    for c, c0 in enumerate(range(0, d_ff, ff_chunk)):
        cols = slice(c0, min(c0 + ff_chunk, d_ff))
        lhs = _after(h2, (hide or {}).get(c), never)
        gate = _dot(lhs, w_gate_ref[:, cols])
        up = _dot(lhs, w_up_ref[:, cols])
        if pending is not None:
            part = _dot(pending[0], w_down_ref[pending[1], :])
            acc = part if acc is None else acc + part
        pending = ((_silu(gate) * up).astype(BF16), cols)
    part = _dot(pending[0], w_down_ref[pending[1], :])
    acc = part if acc is None else acc + part
    return x1 + acc


def _prompt_kernel(x1_ref, x3_ref, g_mix_ref, w_in_ref, wa_ref, wb_ref, bias_ref, ln_g_ref,
                   ln_b_ref, w_out_ref, g_ffn_ref, w_gate_ref, w_up_ref, w_down_ref, g_final_ref,
                   y_ref, new_a_ref, new_b_ref, exta_ref, extb_ref, bg_ref, ycat_ref,
                   *, tile, row_chunk, tiles_per_seq, ff_chunk, stage1=True, stage3=True):
    gw = wa_ref.shape[1]
    n_lane = gw // LANES
    hist_a = exta_ref.shape[1] - tile
    hist_b = extb_ref.shape[1] - tile
    g = pl.program_id(0)
    never = g < 0

    @pl.when(g == 0)
    def _():
        exta_ref[...] = jnp.zeros(exta_ref.shape, F32)
        extb_ref[...] = jnp.zeros(extb_ref.shape, F32)
        bg_ref[...] = jnp.zeros(bg_ref.shape, F32)

    if stage3:
        x1 = x3_ref[...] + _dot(ycat_ref[...], w_out_ref[...])

    starts = list(range(0, tile, row_chunk))
    units = []
    for j in range(n_lane):
        a = _conv_windows(exta_ref, j, [hist_a - (K_A - 1) + s for s in starts], row_chunk,
                          wa_ref, K_A)
        b = _conv_windows(extb_ref, j, [hist_b - (K_B - 1) + s for s in starts], row_chunk,
                          wb_ref, K_B)
        units.append(a + b)
    conv_a = [jnp.concatenate(u[:len(starts)], axis=0) for u in units]
    conv_b = [jnp.concatenate(u[len(starts):], axis=0) for u in units]
    y_cat = _mix(bg_ref[...], jnp.concatenate(conv_a, axis=1), jnp.concatenate(conv_b, axis=1),
                 bias_ref, ln_g_ref, ln_b_ref)
    tail_a = exta_ref[:, tile:tile + hist_a, :]
    tail_b = extb_ref[:, tile:tile + hist_b, :]

    if stage1:
        h = _rmsnorm(x1_ref[...], g_mix_ref[...]).astype(BF16)
        in_proj = lambda k, wait=None: _dot(_after(h, wait, never),
                                            w_in_ref[:, k * gw:(k + 1) * gw])
        glu_val, glu_gate = in_proj(3), in_proj(4)
    if stage3:
        x2 = _ffn(x1, g_ffn_ref, w_gate_ref, w_up_ref, w_down_ref, ff_chunk,
                  hide={c: units[j] for c, j in CONV_UNIT_BEFORE_FFN_CHUNK.items()}, never=never)
        c_gate, v, b_gate = (in_proj(1), in_proj(2), in_proj(0)) if stage1 else (None,) * 3
    else:
        c_gate, v = in_proj(1, units[0]), in_proj(2, units[1])
        b_gate = in_proj(0, units[2] + units[3])
    if stage3:
        y_ref[...] = _rmsnorm(x2, g_final_ref[...])

    if stage1:
        seq_start = (g % tiles_per_seq) == 0
        cv = c_gate * v
        u_b = glu_val * _sigmoid(glu_gate)
        exta_ref[:, 0:hist_a, :] = jnp.where(seq_start, 0.0, tail_a)
        extb_ref[:, 0:hist_b, :] = jnp.where(seq_start, 0.0, tail_b)
        for j in range(n_lane):
            lanes = slice(j * LANES, (j + 1) * LANES)
            exta_ref[j, hist_a:hist_a + tile, :] = cv[:, lanes]
            extb_ref[j, hist_b:hist_b + tile, :] = u_b[:, lanes]
        bg_ref[...] = b_gate
        new_a_ref[...] = cv[tile - (K_A - 1):, :]
        new_b_ref[...] = u_b[tile - (K_B - 1):, :]

    ycat_ref[...] = y_cat


def _sample_kernel(x_ref, sa_ref, sb_ref, g_mix_ref, w_in_ref, wa_ref, wb_ref, bias_ref,
                   ln_g_ref, ln_b_ref, w_out_ref, g_ffn_ref, w_gate_ref, w_up_ref, w_down_ref,
                   g_final_ref, y_ref, new_a_ref, new_b_ref, exta_ref, extb_ref,
                   *, n_seq, seq_len, seq_group, ff_chunk):
    gw = wa_ref.shape[1]
    n_lane = gw // LANES
    ra = exta_ref.shape[1] // n_seq
    rb = extb_ref.shape[1] // n_seq
    off_a = ra - seq_len - (K_A - 1)
    off_b = rb - seq_len - (K_B - 1)

    x = x_ref[...]
    b_gate, cv, u_b = _gating(x, g_mix_ref, w_in_ref, gw)
    for j in range(n_lane):
        lanes = slice(j * LANES, (j + 1) * LANES)
        for r in range(K_B - 1):
            extb_ref[j, pl.ds(off_b + r, n_seq, stride=rb), :] = sb_ref[r, :, lanes]
        for s in range(n_seq):
            rows = slice(s * seq_len, (s + 1) * seq_len)
            exta_ref[j, s * ra + off_a:s * ra + off_a + K_A - 1, :] = sa_ref[s, :, lanes]
            exta_ref[j, (s + 1) * ra - seq_len:(s + 1) * ra, :] = cv[rows, lanes]
            extb_ref[j, (s + 1) * rb - seq_len:(s + 1) * rb, :] = u_b[rows, lanes]

    conv_a, conv_b = [], []
    for j in range(n_lane):
        a, b = [], []
        for s0 in range(0, n_seq, seq_group):
            group = range(s0, s0 + seq_group)
            a += _conv_windows(exta_ref, j, [s * ra + off_a for s in group], seq_len, wa_ref, K_A)
            b += _conv_windows(extb_ref, j, [s * rb + off_b for s in group], seq_len, wb_ref, K_B)
        conv_a.append(jnp.concatenate(a, axis=0))
        conv_b.append(jnp.concatenate(b, axis=0))
    y_cat = _mix(b_gate, jnp.concatenate(conv_a, axis=1), jnp.concatenate(conv_b, axis=1),
                 bias_ref, ln_g_ref, ln_b_ref)
    x1 = x + _dot(y_cat, w_out_ref[...])
    x2 = _ffn(x1, g_ffn_ref, w_gate_ref, w_up_ref, w_down_ref, ff_chunk)
    y_ref[...] = _rmsnorm(x2, g_final_ref[...])

    for j in range(n_lane):
        lanes = slice(j * LANES, (j + 1) * LANES)
        for s in range(n_seq):
            new_a_ref[s, :, lanes] = exta_ref[j, (s + 1) * ra - (K_A - 1):(s + 1) * ra, :]
        for r in range(K_B - 1):
            new_b_ref[r, :, lanes] = extb_ref[j, pl.ds(rb - (K_B - 1) + r, n_seq, stride=rb), :]


N_PARAMS = 13
WEIGHT_SLOTS = (1, 7, 9, 10, 11)
WEIGHT_CHUNK_ROWS = 256
WEIGHT_RING = 4


def _convert_weights(w_hbm_refs, w_vmem_refs, stage_ref, sem_ref):
    ring, chunk = stage_ref.shape[0], stage_ref.shape[1]
    pieces = [(src, dst, r0) for src, dst in zip(w_hbm_refs, w_vmem_refs)
              for r0 in range(0, src.shape[0], chunk)]

    def fetch(p):
        src, _, r0 = pieces[p]
        return pltpu.make_async_copy(src.at[r0:r0 + chunk, :],
                                     stage_ref.at[p % ring, :, 0:src.shape[1]],
                                     sem_ref.at[p % ring])

    for p in range(min(ring - 1, len(pieces))):
        fetch(p).start()
    for p, (src, dst, r0) in enumerate(pieces):
        if p + ring - 1 < len(pieces):
            fetch(p + ring - 1).start()
        fetch(p).wait()
        dst[r0:r0 + chunk, :] = stage_ref[p % ring, :, 0:src.shape[1]].astype(BF16)


def _prompt_entry(*refs, n_steps, step):
    n_w = len(WEIGHT_SLOTS)
    x1_ref, x3_ref = refs[:2]
    params = list(refs[2:2 + N_PARAMS - n_w])
    w_hbm = refs[2 + N_PARAMS - n_w:2 + N_PARAMS]
    outs = refs[2 + N_PARAMS:5 + N_PARAMS]
    w_out_hbm = refs[5 + N_PARAMS:5 + N_PARAMS + n_w]
    scratch = refs[5 + N_PARAMS + n_w:]
    exta_ref, extb_ref, bg_ref, ycat_ref = scratch[:4]
    w_vmem = scratch[4:4 + n_w]
    stage_ref, sem_ref, out_sem_ref = scratch[4 + n_w:]
    g = pl.program_id(0)

    def copy_out(k):
        return pltpu.make_async_copy(w_vmem[k], w_out_hbm[k], out_sem_ref.at[k])

    @pl.when(g == 0)
    def _():
        _convert_weights(w_hbm, w_vmem, stage_ref, sem_ref)
        for k in range(n_w):
            copy_out(k).start()

    for slot, w in zip(WEIGHT_SLOTS, w_vmem):
        params.insert(slot, w)
    run = functools.partial(_prompt_kernel, x1_ref, x3_ref, *params, *outs,
                            exta_ref, extb_ref, bg_ref, ycat_ref, **step)
    skew = 2

    @pl.when(g < skew)
    def _():
        run(stage3=False)

    @pl.when(jnp.logical_and(g >= skew, g < n_steps - skew))
    def _():
        run()

    @pl.when(g >= n_steps - skew)
    def _():
        run(stage1=False)

    @pl.when(g == n_steps - 1)
    def _():
        for k in range(n_w):
            copy_out(k).wait()


def _round_up(n, m):
    return -(-n // m) * m


def _const_spec(shape):
    zeros = (0,) * len(shape)
    block = shape if len(shape) == 2 else (None,) + tuple(shape[1:])
    return pl.BlockSpec(block, lambda g: zeros, pipeline_mode=pl.Buffered(1))


PROMPT_TILE = 256
PROMPT_ROW_CHUNK = 64
CONV_UNIT_BEFORE_FFN_CHUNK = {1: 0, 2: 1, 3: 2, 4: 3}
FF_CHUNK = 512
SAMPLE_SEQS = 32
SAMPLE_SEQ_GROUP = 16
VMEM_LIMIT_BYTES = 56 * 1024 * 1024


def kernel(x_prompt, x_sample, state_conv_a, state_conv_b, g_mix, w_in, conv_a_w, conv_b_w,
           conv_b_bias, ln_b_g, ln_b_b, w_out, g_ffn, w_gate, w_up, w_down, g_final):
    depth = g_mix.shape[0]
    assert depth == 1, "single-layer trunk"
    bp, seq, d = x_prompt.shape
    bs, dec_seq, _ = x_sample.shape
    gw = conv_a_w.shape[-1]
    assert conv_a_w.shape[1] == K_A and conv_b_w.shape[1] == K_B
    assert dec_seq == SUBLANES and gw % LANES == 0
    assert seq % PROMPT_TILE == 0 and bs % SAMPLE_SEQS == 0

    row = lambda a: a.reshape(1, -1)
    small = (row(g_mix[0]), conv_a_w, conv_b_w, row(conv_b_bias[0]), row(ln_b_g[0]),
             row(ln_b_b[0]), row(g_ffn[0]), row(g_final))
    weights = (w_in[0], w_out[0], w_gate[0], w_up[0], w_down[0])
    assert len(small) + len(weights) == N_PARAMS
    assert all(w.shape[0] % WEIGHT_CHUNK_ROWS == 0 for w in weights)

    tile = PROMPT_TILE
    tiles_per_seq = seq // tile
    n_tiles = bp * tiles_per_seq
    n_steps = n_tiles + 2
    hist_a = _round_up(K_A - 1, SUBLANES)
    hist_b = _round_up(K_B - 1, SUBLANES)
    n_lane = gw // LANES
    tile_1 = lambda g: jnp.minimum(g, n_tiles - 1)
    tile_3 = lambda g: jnp.maximum(g - 2, 0)
    any_spec = pl.BlockSpec(memory_space=pl.ANY)
    x_rows = x_prompt.reshape(bp * seq, d)
    y_prompt, pa, pb, *weights_bf16 = pl.pallas_call(
        functools.partial(
            _prompt_entry, n_steps=n_steps,
            step=dict(tile=tile, row_chunk=PROMPT_ROW_CHUNK, tiles_per_seq=tiles_per_seq,
                      ff_chunk=FF_CHUNK)),
        grid=(n_steps,),
        in_specs=[pl.BlockSpec((tile, d), lambda g: (tile_1(g), 0)),
                  pl.BlockSpec((tile, d), lambda g: (tile_3(g), 0))]
        + [_const_spec(p.shape) for p in small] + [any_spec] * len(weights),
        out_specs=[pl.BlockSpec((tile, d), lambda g: (tile_3(g), 0)),
                   pl.BlockSpec((None, K_A - 1, gw), lambda g: (tile_1(g) // tiles_per_seq, 0, 0)),
                   pl.BlockSpec((None, K_B - 1, gw), lambda g: (tile_1(g) // tiles_per_seq, 0, 0))]
        + [any_spec] * len(weights),
        out_shape=[jax.ShapeDtypeStruct((bp * seq, d), F32),
                   jax.ShapeDtypeStruct((bp, K_A - 1, gw), F32),
                   jax.ShapeDtypeStruct((bp, K_B - 1, gw), F32)]
        + [jax.ShapeDtypeStruct(w.shape, BF16) for w in weights],
        scratch_shapes=[pltpu.VMEM((n_lane, hist_a + tile, LANES), F32),
                        pltpu.VMEM((n_lane, hist_b + tile, LANES), F32),
                        pltpu.VMEM((tile, gw), F32),
                        pltpu.VMEM((tile, 2 * gw), BF16)]
        + [pltpu.VMEM(w.shape, BF16) for w in weights]
        + [pltpu.VMEM((WEIGHT_RING, WEIGHT_CHUNK_ROWS, max(w.shape[1] for w in weights)), F32),
           pltpu.SemaphoreType.DMA((WEIGHT_RING,)),
           pltpu.SemaphoreType.DMA((len(weights),))],
        compiler_params=pltpu.CompilerParams(
            dimension_semantics=("arbitrary",),
            vmem_limit_bytes=VMEM_LIMIT_BYTES),
        name="prompt_layer",
    )(x_rows, x_rows, *small, *weights)
    y_prompt = y_prompt.reshape(bp, seq, d)

    params = list(small)
    for slot, w in zip(WEIGHT_SLOTS, weights_bf16):
        params.insert(slot, w)
    by_row = lambda a: jnp.swapaxes(a, 0, 1)
    n_seq = SAMPLE_SEQS
    rows = n_seq * dec_seq
    ra = _round_up(K_A - 1, SUBLANES) + dec_seq
    rb = _round_up(K_B - 1, SUBLANES) + dec_seq
    y_sample, sa, sb = pl.pallas_call(
        functools.partial(_sample_kernel, n_seq=n_seq, seq_len=dec_seq,
                          seq_group=SAMPLE_SEQ_GROUP, ff_chunk=FF_CHUNK),
        grid=(bs // n_seq,),
        in_specs=[pl.BlockSpec((rows, d), lambda i: (i, 0)),
                  pl.BlockSpec((n_seq, K_A - 1, gw), lambda i: (i, 0, 0)),
                  pl.BlockSpec((K_B - 1, n_seq, gw), lambda i: (0, i, 0))]
        + [_const_spec(p.shape) for p in params],
        out_specs=[pl.BlockSpec((rows, d), lambda i: (i, 0)),
                   pl.BlockSpec((n_seq, K_A - 1, gw), lambda i: (i, 0, 0)),
                   pl.BlockSpec((K_B - 1, n_seq, gw), lambda i: (0, i, 0))],
        out_shape=[jax.ShapeDtypeStruct((bs * dec_seq, d), F32),
                   jax.ShapeDtypeStruct((bs, K_A - 1, gw), F32),
                   jax.ShapeDtypeStruct((K_B - 1, bs, gw), F32)],
        scratch_shapes=[pltpu.VMEM((n_lane, n_seq * ra, LANES), F32),
                        pltpu.VMEM((n_lane, n_seq * rb, LANES), F32)],
        compiler_params=pltpu.CompilerParams(
            dimension_semantics=("arbitrary",),
            vmem_limit_bytes=VMEM_LIMIT_BYTES),
        name="sample_layer",
    )(x_sample.reshape(bs * dec_seq, d), state_conv_a[0], by_row(state_conv_b[0]), *params)

    return (y_prompt, y_sample.reshape(bs, dec_seq, d), pa[None], pb[None],
            sa[None], by_row(sb)[None])
```

```python
import functools

import jax
import jax.numpy as jnp
from jax import lax
from jax.experimental import pallas as pl
from jax.experimental.pallas import tpu as pltpu

EPS = 1e-6
K_A = 3
K_B = 31
LANES = 128
SUBLANES = 8

F32 = jnp.float32
BF16 = jnp.bfloat16


def _rmsnorm(x, g):
    ms = jnp.mean(x * x, axis=-1, keepdims=True)
    return x * lax.rsqrt(ms + EPS) * g


def _layernorm(x, g, b):
    mu = jnp.mean(x, axis=-1, keepdims=True)
    xc = x - mu
    var = jnp.mean(xc * xc, axis=-1, keepdims=True)
    return xc * lax.rsqrt(var + EPS) * g + b


def _sigmoid(x):
    return 1.0 / (1.0 + jnp.exp(-x))


def _silu(x):
    return x * _sigmoid(x)


def _dot(a, b):
    return jnp.dot(a, b, preferred_element_type=F32)


def _conv_windows(ext_ref, j, starts, nrows, w_ref, n_taps):
    lanes = slice(j * LANES, (j + 1) * LANES)
    accs = [None] * len(starts)
    for k in range(n_taps):
        wk = w_ref[k, :, lanes]
        for i, s in enumerate(starts):
            term = wk * ext_ref[j, pl.ds(s + k, nrows), :]
            accs[i] = term if accs[i] is None else accs[i] + term
    return accs


def _after(lhs, pieces, never):
    if not pieces:
        return lhs
    rows = 2 * SUBLANES
    total = None
    for p in pieces:
        for r in range(0, p.shape[0], rows):
            for c in range(0, p.shape[1], LANES):
                slab = p[r:r + rows, c:c + LANES].astype(lhs.dtype)
                total = slab if total is None else total + slab
    head = jnp.where(never, total, lhs[0:rows, 0:LANES])
    top = jnp.concatenate([head, lhs[0:rows, LANES:]], axis=1)
    return jnp.concatenate([top, lhs[rows:, :]], axis=0)


def _gating(x, g_mix_ref, w_in_ref, gw):
    h = _rmsnorm(x, g_mix_ref[...]).astype(BF16)
    b_gate, c_gate, v, glu_val, glu_gate = [
        _dot(h, w_in_ref[:, k * gw:(k + 1) * gw]) for k in range(5)]
    return b_gate, c_gate * v, glu_val * _sigmoid(glu_gate)


def _mix(b_gate, conv_a, conv_b, bias_ref, ln_g_ref, ln_b_ref):
    y_b = _silu(_layernorm(conv_b + bias_ref[...], ln_g_ref[...], ln_b_ref[...]))
    return jnp.concatenate([b_gate * conv_a, y_b], axis=-1).astype(BF16)


def _ffn(x1, g_ffn_ref, w_gate_ref, w_up_ref, w_down_ref, ff_chunk, hide=None, never=None):
    d_ff = w_gate_ref.shape[1]
    h2 = _rmsnorm(x1, g_ffn_ref[...]).astype(BF16)
    acc = None
    pending = None
    for c, c0 in enumerate(range(0, d_ff, ff_chunk)):
        cols = slice(c0, min(c0 + ff_chunk, d_ff))
        lhs = _after(h2, (hide or {}).get(c), never)
        gate = _dot(lhs, w_gate_ref[:, cols])
        up = _dot(lhs, w_up_ref[:, cols])
        if pending is not None:
            part = _dot(pending[0], w_down_ref[pending[1], :])
            acc = part if acc is None else acc + part
        pending = ((_silu(gate) * up).astype(BF16), cols)
    part = _dot(pending[0], w_down_ref[pending[1], :])
    acc = part if acc is None else acc + part
    return x1 + acc


def _prompt_kernel(x1_ref, x3_ref, g_mix_ref, w_in_ref, wa_ref, wb_ref, bias_ref, ln_g_ref,
                   ln_b_ref, w_out_ref, g_ffn_ref, w_gate_ref, w_up_ref, w_down_ref, g_final_ref,
                   y_ref, new_a_ref, new_b_ref, exta_ref, extb_ref, bg_ref, ycat_ref,
                   *, tile, row_chunk, tiles_per_seq, ff_chunk, stage1=True, stage3=True):
    gw = wa_ref.shape[-1]
    n_lane = gw // LANES
    hist_a = exta_ref.shape[1] - tile
    hist_b = extb_ref.shape[1] - tile
    g = pl.program_id(0)
    never = g < 0

    @pl.when(g == 0)
    def _():
        exta_ref[...] = jnp.zeros(exta_ref.shape, F32)
        extb_ref[...] = jnp.zeros(extb_ref.shape, F32)
        bg_ref[...] = jnp.zeros(bg_ref.shape, F32)
        new_b_ref[...] = jnp.zeros(new_b_ref.shape, F32)

    if stage3:
        x1 = x3_ref[...] + _dot(ycat_ref[...], w_out_ref[...])

    starts = list(range(0, tile, row_chunk))
    units = []
    for j in range(n_lane):
        a = _conv_windows(exta_ref, j, [hist_a - (K_A - 1) + s for s in starts], row_chunk,
                          wa_ref, K_A)
        b = _conv_windows(extb_ref, j, [hist_b - (K_B - 1) + s for s in starts], row_chunk,
                          wb_ref, K_B)
        units.append(a + b)
    conv_a = [jnp.concatenate(u[:len(starts)], axis=0) for u in units]
    conv_b = [jnp.concatenate(u[len(starts):], axis=0) for u in units]
    y_cat = _mix(bg_ref[...], jnp.concatenate(conv_a, axis=1), jnp.concatenate(conv_b, axis=1),
                 bias_ref, ln_g_ref, ln_b_ref)
    tail_a = exta_ref[:, tile:tile + hist_a, :]
    tail_b = extb_ref[:, tile:tile + hist_b, :]

    if stage1:
        h = _rmsnorm(x1_ref[...], g_mix_ref[...]).astype(BF16)
        in_proj = lambda k, wait=None: _dot(_after(h, wait, never),
                                            w_in_ref[:, k * gw:(k + 1) * gw])
        glu_val, glu_gate = in_proj(3), in_proj(4)
    if stage3:
        x2 = _ffn(x1, g_ffn_ref, w_gate_ref, w_up_ref, w_down_ref, ff_chunk,
                  hide={c: units[j] for c, j in CONV_UNIT_BEFORE_FFN_CHUNK.items()}, never=never)
        c_gate, v, b_gate = (in_proj(1), in_proj(2), in_proj(0)) if stage1 else (None,) * 3
    else:
        c_gate, v = in_proj(1, units[0]), in_proj(2, units[1])
        b_gate = in_proj(0, units[2] + units[3])
    if stage3:
        y_ref[...] = _rmsnorm(x2, g_final_ref[...])

    if stage1:
        seq_start = (g % tiles_per_seq) == 0
        cv = c_gate * v
        u_b = glu_val * _sigmoid(glu_gate)
        exta_ref[:, 0:hist_a, :] = jnp.where(seq_start, 0.0, tail_a)
        extb_ref[:, 0:hist_b, :] = jnp.where(seq_start, 0.0, tail_b)
        for j in range(n_lane):
            lanes = slice(j * LANES, (j + 1) * LANES)
            exta_ref[j, hist_a:hist_a + tile, :] = cv[:, lanes]
            extb_ref[j, hist_b:hist_b + tile, :] = u_b[:, lanes]
        bg_ref[...] = b_gate
        new_a_ref[...] = cv[tile - (K_A - 1):, :]
        @pl.when(g % tiles_per_seq == tiles_per_seq - 1)
        def _():
            mine = lax.broadcasted_iota(jnp.int32, new_b_ref.shape[1:2] + (LANES,), 0) == (
                g // tiles_per_seq)
            for j in range(n_lane):
                lanes = slice(j * LANES, (j + 1) * LANES)
                for r in range(K_B - 1):
                    last = extb_ref[j, pl.ds(hist_b + tile - (K_B - 1) + r, 1), :]
                    old = new_b_ref[r, :, lanes]
                    new_b_ref[r, :, lanes] = jnp.where(
                        mine, jnp.broadcast_to(last, old.shape), old)

    ycat_ref[...] = y_cat


def _sample_kernel(x_ref, sa_ref, sb_ref, g_mix_ref, w_in_ref, wa_ref, wb_ref, bias_ref,
                   ln_g_ref, ln_b_ref, w_out_ref, g_ffn_ref, w_gate_ref, w_up_ref, w_down_ref,
                   g_final_ref, y_ref, new_a_ref, new_b_ref, exta_ref, extb_ref,
                   *, n_seq, seq_len, seq_group, ff_chunk):
    gw = wa_ref.shape[-1]
    n_lane = gw // LANES
    ra = exta_ref.shape[1] // n_seq
    rb = extb_ref.shape[1] // n_seq
    off_a = ra - seq_len - (K_A - 1)
    off_b = rb - seq_len - (K_B - 1)

    x = x_ref[...]
    b_gate, cv, u_b = _gating(x, g_mix_ref, w_in_ref, gw)
    for j in range(n_lane):
        lanes = slice(j * LANES, (j + 1) * LANES)
        for r in range(K_B - 1):
            extb_ref[j, pl.ds(off_b + r, n_seq, stride=rb), :] = sb_ref[r, :, lanes]
        for s in range(n_seq):
            rows = slice(s * seq_len, (s + 1) * seq_len)
            exta_ref[j, s * ra + off_a:s * ra + off_a + K_A - 1, :] = sa_ref[s, :, lanes]
            exta_ref[j, (s + 1) * ra - seq_len:(s + 1) * ra, :] = cv[rows, lanes]
            extb_ref[j, (s + 1) * rb - seq_len:(s + 1) * rb, :] = u_b[rows, lanes]

    conv_a, conv_b = [], []
    for j in range(n_lane):
        a, b = [], []
        for s0 in range(0, n_seq, seq_group):
            group = range(s0, s0 + seq_group)
            a += _conv_windows(exta_ref, j, [s * ra + off_a for s in group], seq_len, wa_ref, K_A)
            b += _conv_windows(extb_ref, j, [s * rb + off_b for s in group], seq_len, wb_ref, K_B)
        conv_a.append(jnp.concatenate(a, axis=0))
        conv_b.append(jnp.concatenate(b, axis=0))
    y_cat = _mix(b_gate, jnp.concatenate(conv_a, axis=1), jnp.concatenate(conv_b, axis=1),
                 bias_ref, ln_g_ref, ln_b_ref)
    x1 = x + _dot(y_cat, w_out_ref[...])
    x2 = _ffn(x1, g_ffn_ref, w_gate_ref, w_up_ref, w_down_ref, ff_chunk)
    y_ref[...] = _rmsnorm(x2, g_final_ref[...])

    for j in range(n_lane):
        lanes = slice(j * LANES, (j + 1) * LANES)
        for s in range(n_seq):
            new_a_ref[s, :, lanes] = exta_ref[j, (s + 1) * ra - (K_A - 1):(s + 1) * ra, :]
        for r in range(K_B - 1):
            new_b_ref[r, :, lanes] = extb_ref[j, pl.ds(rb - (K_B - 1) + r, n_seq, stride=rb), :]


N_PARAMS = 13
WEIGHT_SLOTS = (1, 7, 9, 10, 11)
WEIGHT_CHUNK_ROWS = 256
WEIGHT_RING = 4


def _convert_weights(w_hbm_refs, w_vmem_refs, stage_ref, sem_ref):
    ring, chunk = stage_ref.shape[0], stage_ref.shape[1]
    pieces = [(src, dst, r0) for src, dst in zip(w_hbm_refs, w_vmem_refs)
              for r0 in range(0, src.shape[0], chunk)]

    def fetch(p):
        src, _, r0 = pieces[p]
        return pltpu.make_async_copy(src.at[r0:r0 + chunk, :],
                                     stage_ref.at[p % ring, :, 0:src.shape[1]],
                                     sem_ref.at[p % ring])

    for p in range(min(ring - 1, len(pieces))):
        fetch(p).start()
    for p, (src, dst, r0) in enumerate(pieces):
        if p + ring - 1 < len(pieces):
            fetch(p + ring - 1).start()
        fetch(p).wait()
        dst[r0:r0 + chunk, :] = stage_ref[p % ring, :, 0:src.shape[1]].astype(BF16)


def _prompt_entry(*refs, n_steps, step):
    n_w = len(WEIGHT_SLOTS)
    x1_ref, x3_ref = refs[:2]
    params = list(refs[2:2 + N_PARAMS - n_w])
    w_hbm = refs[2 + N_PARAMS - n_w:2 + N_PARAMS]
    outs = refs[2 + N_PARAMS:5 + N_PARAMS]
    w_out_hbm = refs[5 + N_PARAMS:5 + N_PARAMS + n_w]
    scratch = refs[5 + N_PARAMS + n_w:]
    exta_ref, extb_ref, bg_ref, ycat_ref = scratch[:4]
    w_vmem = scratch[4:4 + n_w]
    stage_ref, sem_ref, out_sem_ref = scratch[4 + n_w:]
    g = pl.program_id(0)

    def copy_out(k):
        return pltpu.make_async_copy(w_vmem[k], w_out_hbm[k], out_sem_ref.at[k])

    @pl.when(g == 0)
    def _():
        _convert_weights(w_hbm, w_vmem, stage_ref, sem_ref)
        for k in range(n_w):
            copy_out(k).start()

    for slot, w in zip(WEIGHT_SLOTS, w_vmem):
        params.insert(slot, w)
    run = functools.partial(_prompt_kernel, x1_ref, x3_ref, *params, *outs,
                            exta_ref, extb_ref, bg_ref, ycat_ref, **step)
    skew = 2

    @pl.when(g < skew)
    def _():
        run(stage3=False)

    @pl.when(jnp.logical_and(g >= skew, g < n_steps - skew))
    def _():
        run()

    @pl.when(g >= n_steps - skew)
    def _():
        run(stage1=False)

    @pl.when(g == n_steps - 1)
    def _():
        for k in range(n_w):
            copy_out(k).wait()


def _round_up(n, m):
    return -(-n // m) * m


def _const_spec(shape):
    zeros = (0,) * len(shape)
    return pl.BlockSpec(shape, lambda g: zeros, pipeline_mode=pl.Buffered(1))


PROMPT_TILE = 256
PROMPT_ROW_CHUNK = 64
CONV_UNIT_BEFORE_FFN_CHUNK = {1: 0, 2: 1, 3: 2, 4: 3}
FF_CHUNK = 512
SAMPLE_SEQS = 32
SAMPLE_SEQ_GROUP = 16
VMEM_LIMIT_BYTES = 56 * 1024 * 1024


def kernel(x_prompt, x_sample, state_conv_a, state_conv_b, g_mix, w_in, conv_a_w, conv_b_w,
           conv_b_bias, ln_b_g, ln_b_b, w_out, g_ffn, w_gate, w_up, w_down, g_final):
    depth = g_mix.shape[0]
    assert depth == 1, "single-layer trunk"
    bp, seq, d = x_prompt.shape
    bs, dec_seq, _ = x_sample.shape
    gw = conv_a_w.shape[-1]
    assert conv_a_w.shape[1] == K_A and conv_b_w.shape[1] == K_B
    assert dec_seq == SUBLANES and gw % LANES == 0
    assert seq % PROMPT_TILE == 0 and bs % SAMPLE_SEQS == 0

    row = lambda a: a.reshape(1, -1)
    taps = lambda w: jnp.swapaxes(w, 0, 1)
    small = (row(g_mix[0]), taps(conv_a_w), taps(conv_b_w), row(conv_b_bias[0]), row(ln_b_g[0]),
             row(ln_b_b[0]), row(g_ffn[0]), row(g_final))
    weights = (w_in[0], w_out[0], w_gate[0], w_up[0], w_down[0])
    assert len(small) + len(weights) == N_PARAMS
    assert all(w.shape[0] % WEIGHT_CHUNK_ROWS == 0 for w in weights)

    tile = PROMPT_TILE
    tiles_per_seq = seq // tile
    n_tiles = bp * tiles_per_seq
    n_steps = n_tiles + 2
    hist_a = _round_up(K_A - 1, SUBLANES)
    hist_b = _round_up(K_B - 1, SUBLANES)
    n_lane = gw // LANES
    tile_1 = lambda g: jnp.minimum(g, n_tiles - 1)
    tile_3 = lambda g: jnp.maximum(g - 2, 0)
    any_spec = pl.BlockSpec(memory_space=pl.ANY)
    x_rows = x_prompt.reshape(bp * seq, d)
    y_prompt, pa, pb, *weights_bf16 = pl.pallas_call(
        functools.partial(
            _prompt_entry, n_steps=n_steps,
            step=dict(tile=tile, row_chunk=PROMPT_ROW_CHUNK, tiles_per_seq=tiles_per_seq,
                      ff_chunk=FF_CHUNK)),
        grid=(n_steps,),
        in_specs=[pl.BlockSpec((tile, d), lambda g: (tile_1(g), 0)),
                  pl.BlockSpec((tile, d), lambda g: (tile_3(g), 0))]
        + [_const_spec(p.shape) for p in small] + [any_spec] * len(weights),
        out_specs=[pl.BlockSpec((tile, d), lambda g: (tile_3(g), 0)),
                   pl.BlockSpec((None, K_A - 1, gw), lambda g: (tile_1(g) // tiles_per_seq, 0, 0)),
                   pl.BlockSpec((K_B - 1, bp, gw), lambda g: (0, 0, 0))]
        + [any_spec] * len(weights),
        out_shape=[jax.ShapeDtypeStruct((bp * seq, d), F32),
                   jax.ShapeDtypeStruct((bp, K_A - 1, gw), F32),
                   jax.ShapeDtypeStruct((K_B - 1, bp, gw), F32)]
        + [jax.ShapeDtypeStruct(w.shape, BF16) for w in weights],
        scratch_shapes=[pltpu.VMEM((n_lane, hist_a + tile, LANES), F32),
                        pltpu.VMEM((n_lane, hist_b + tile, LANES), F32),
                        pltpu.VMEM((tile, gw), F32),
                        pltpu.VMEM((tile, 2 * gw), BF16)]
        + [pltpu.VMEM(w.shape, BF16) for w in weights]
        + [pltpu.VMEM((WEIGHT_RING, WEIGHT_CHUNK_ROWS, max(w.shape[1] for w in weights)), F32),
           pltpu.SemaphoreType.DMA((WEIGHT_RING,)),
           pltpu.SemaphoreType.DMA((len(weights),))],
        compiler_params=pltpu.CompilerParams(
            dimension_semantics=("arbitrary",),
            vmem_limit_bytes=VMEM_LIMIT_BYTES),
        name="prompt_layer",
    )(x_rows, x_rows, *small, *weights)
    y_prompt = y_prompt.reshape(bp, seq, d)

    params = list(small)
    for slot, w in zip(WEIGHT_SLOTS, weights_bf16):
        params.insert(slot, w)
    by_row = lambda a: jnp.swapaxes(a, 0, 1)
    n_seq = SAMPLE_SEQS
    rows = n_seq * dec_seq
    ra = _round_up(K_A - 1, SUBLANES) + dec_seq
    rb = _round_up(K_B - 1, SUBLANES) + dec_seq
    y_sample, sa, sb = pl.pallas_call(
        functools.partial(_sample_kernel, n_seq=n_seq, seq_len=dec_seq,
                          seq_group=SAMPLE_SEQ_GROUP, ff_chunk=FF_CHUNK),
        grid=(bs // n_seq,),
        in_specs=[pl.BlockSpec((rows, d), lambda i: (i, 0)),
                  pl.BlockSpec((n_seq, K_A - 1, gw), lambda i: (i, 0, 0)),
                  pl.BlockSpec((K_B - 1, n_seq, gw), lambda i: (0, i, 0))]
        + [_const_spec(p.shape) for p in params],
        out_specs=[pl.BlockSpec((rows, d), lambda i: (i, 0)),
                   pl.BlockSpec((n_seq, K_A - 1, gw), lambda i: (i, 0, 0)),
                   pl.BlockSpec((K_B - 1, n_seq, gw), lambda i: (0, i, 0))],
        out_shape=[jax.ShapeDtypeStruct((bs * dec_seq, d), F32),
                   jax.ShapeDtypeStruct((bs, K_A - 1, gw), F32),
                   jax.ShapeDtypeStruct((K_B - 1, bs, gw), F32)],
        scratch_shapes=[pltpu.VMEM((n_lane, n_seq * ra, LANES), F32),
                        pltpu.VMEM((n_lane, n_seq * rb, LANES), F32)],
        compiler_params=pltpu.CompilerParams(
            dimension_semantics=("arbitrary",),
            vmem_limit_bytes=VMEM_LIMIT_BYTES),
        name="sample_layer",
    )(x_sample.reshape(bs * dec_seq, d), state_conv_a[0], by_row(state_conv_b[0]), *params)

    return (y_prompt, y_sample.reshape(bs, dec_seq, d), pa[None], by_row(pb)[None],
            sa[None], by_row(sb)[None])
```

```python
import functools

import jax
import jax.numpy as jnp
from jax import lax
from jax.experimental import pallas as pl
from jax.experimental.pallas import tpu as pltpu

EPS = 1e-6
K_A = 3
K_B = 31
LANES = 128
SUBLANES = 8

F32 = jnp.float32
BF16 = jnp.bfloat16


def _rmsnorm(x, g):
    ms = jnp.mean(x * x, axis=-1, keepdims=True)
    return x * lax.rsqrt(ms + EPS) * g


def _layernorm(x, g, b):
    mu = jnp.mean(x, axis=-1, keepdims=True)
    xc = x - mu
    var = jnp.mean(xc * xc, axis=-1, keepdims=True)
    return xc * lax.rsqrt(var + EPS) * g + b


def _sigmoid(x):
    return 1.0 / (1.0 + jnp.exp(-x))


def _silu(x):
    return x * _sigmoid(x)


def _dot(a, b):
    return jnp.dot(a, b, preferred_element_type=F32)


def _conv_windows(ext_ref, j, starts, nrows, w_ref, n_taps):
    lanes = slice(j * LANES, (j + 1) * LANES)
    accs = [None] * len(starts)
    for k in range(n_taps):
        wk = w_ref[k, :, lanes]
        for i, s in enumerate(starts):
            term = wk * ext_ref[j, pl.ds(s + k, nrows), :]
            accs[i] = term if accs[i] is None else accs[i] + term
    return accs


def _after(lhs, pieces, never):
    if not pieces:
        return lhs
    rows = 2 * SUBLANES
    total = None
    for p in pieces:
        for r in range(0, p.shape[0], rows):
            for c in range(0, p.shape[1], LANES):
                slab = p[r:r + rows, c:c + LANES].astype(lhs.dtype)
                total = slab if total is None else total + slab
    head = jnp.where(never, total, lhs[0:rows, 0:LANES])
    top = jnp.concatenate([head, lhs[0:rows, LANES:]], axis=1)
    return jnp.concatenate([top, lhs[rows:, :]], axis=0)


def _gating(x, g_mix_ref, w_in_ref, gw):
    h = _rmsnorm(x, g_mix_ref[...]).astype(BF16)
    b_gate, c_gate, v, glu_val, glu_gate = [
        _dot(h, w_in_ref[:, k * gw:(k + 1) * gw]) for k in range(5)]
    return b_gate, c_gate * v, glu_val * _sigmoid(glu_gate)


def _mix(b_gate, conv_a, conv_b, bias_ref, ln_g_ref, ln_b_ref):
    y_b = _silu(_layernorm(conv_b + bias_ref[...], ln_g_ref[...], ln_b_ref[...]))
    return jnp.concatenate([b_gate * conv_a, y_b], axis=-1).astype(BF16)


def _ffn(x1, g_ffn_ref, w_gate_ref, w_up_ref, w_down_ref, ff_chunk, hide=None, never=None):
    d_ff = w_gate_ref.shape[1]
    h2 = _rmsnorm(x1, g_ffn_ref[...]).astype(BF16)
    acc = None
    pending = None
    for c, c0 in enumerate(range(0, d_ff, ff_chunk)):
        cols = slice(c0, min(c0 + ff_chunk, d_ff))
        lhs = _after(h2, (hide or {}).get(c), never)
        gate = _dot(lhs, w_gate_ref[:, cols])
        up = _dot(lhs, w_up_ref[:, cols])
        if pending is not None:
            part = _dot(pending[0], w_down_ref[pending[1], :])
            acc = part if acc is None else acc + part
        pending = ((_silu(gate) * up).astype(BF16), cols)
    part = _dot(pending[0], w_down_ref[pending[1], :])
    acc = part if acc is None else acc + part
    return x1 + acc


def _prompt_kernel(x1_ref, x3_ref, g_mix_ref, w_in_ref, wa_ref, wb_ref, bias_ref, ln_g_ref,
                   ln_b_ref, w_out_ref, g_ffn_ref, w_gate_ref, w_up_ref, w_down_ref, g_final_ref,
                   y_ref, new_a_ref, new_b_ref, exta_ref, extb_ref, bg_ref, ycat_ref,
                   *, tile, row_chunk, tiles_per_seq, ff_chunk, stage1=True, stage3=True):
    gw = wa_ref.shape[-1]
    n_lane = gw // LANES
    hist_a = exta_ref.shape[1] - tile
    hist_b = extb_ref.shape[1] - tile
    g = pl.program_id(0)
    never = g < 0

    @pl.when(g == 0)
    def _():
        exta_ref[...] = jnp.zeros(exta_ref.shape, F32)
        extb_ref[...] = jnp.zeros(extb_ref.shape, F32)
        bg_ref[...] = jnp.zeros(bg_ref.shape, F32)

    if stage3:
        x1 = x3_ref[...] + _dot(ycat_ref[...], w_out_ref[...])

    starts = list(range(0, tile, row_chunk))
    units = []
    for j in range(n_lane):
        a = _conv_windows(exta_ref, j, [hist_a - (K_A - 1) + s for s in starts], row_chunk,
                          wa_ref, K_A)
        b = _conv_windows(extb_ref, j, [hist_b - (K_B - 1) + s for s in starts], row_chunk,
                          wb_ref, K_B)
        units.append(a + b)
    conv_a = [jnp.concatenate(u[:len(starts)], axis=0) for u in units]
    conv_b = [jnp.concatenate(u[len(starts):], axis=0) for u in units]
    y_cat = _mix(bg_ref[...], jnp.concatenate(conv_a, axis=1), jnp.concatenate(conv_b, axis=1),
                 bias_ref, ln_g_ref, ln_b_ref)
    tail_a = exta_ref[:, tile:tile + hist_a, :]
    tail_b = extb_ref[:, tile:tile + hist_b, :]

    if stage1:
        h = _rmsnorm(x1_ref[...], g_mix_ref[...]).astype(BF16)
        in_proj = lambda k, wait=None: _dot(_after(h, wait, never),
                                            w_in_ref[:, k * gw:(k + 1) * gw])
        glu_val, glu_gate = in_proj(3), in_proj(4)
    if stage3:
        x2 = _ffn(x1, g_ffn_ref, w_gate_ref, w_up_ref, w_down_ref, ff_chunk,
                  hide={c: units[j] for c, j in CONV_UNIT_BEFORE_FFN_CHUNK.items()}, never=never)
        c_gate, v, b_gate = (in_proj(1), in_proj(2), in_proj(0)) if stage1 else (None,) * 3
    else:
        c_gate, v = in_proj(1, units[0]), in_proj(2, units[1])
        b_gate = in_proj(0, units[2] + units[3])
    if stage3:
        y_ref[...] = _rmsnorm(x2, g_final_ref[...])

    if stage1:
        seq_start = (g % tiles_per_seq) == 0
        cv = c_gate * v
        u_b = glu_val * _sigmoid(glu_gate)
        exta_ref[:, 0:hist_a, :] = jnp.where(seq_start, 0.0, tail_a)
        extb_ref[:, 0:hist_b, :] = jnp.where(seq_start, 0.0, tail_b)
        for j in range(n_lane):
            lanes = slice(j * LANES, (j + 1) * LANES)
            exta_ref[j, hist_a:hist_a + tile, :] = cv[:, lanes]
            extb_ref[j, hist_b:hist_b + tile, :] = u_b[:, lanes]
        bg_ref[...] = b_gate
        new_a_ref[...] = cv[tile - (K_A - 1):, :]
        new_b_ref[...] = u_b[tile - (K_B - 1):, :]

    ycat_ref[...] = y_cat


def _sample_kernel(x_ref, sa_ref, sb_ref, g_mix_ref, w_in_ref, wa_ref, wb_ref, bias_ref,
                   ln_g_ref, ln_b_ref, w_out_ref, g_ffn_ref, w_gate_ref, w_up_ref, w_down_ref,
                   g_final_ref, y_ref, new_a_ref, new_b_ref, exta_ref, rows_ref, slabs_ref,
                   *, n_seq, seq_len, seq_group, ff_chunk):
    gw = wa_ref.shape[-1]
    n_lane = gw // LANES
    ra = exta_ref.shape[1] // n_seq
    off_a = ra - seq_len - (K_A - 1)
    n_hist = K_B - 1

    x = x_ref[...]
    b_gate, cv, u_b = _gating(x, g_mix_ref, w_in_ref, gw)
    for j in range(n_lane):
        lanes = slice(j * LANES, (j + 1) * LANES)
        rows_ref[j] = u_b[:, lanes]
        for s in range(n_seq):
            rows = slice(s * seq_len, (s + 1) * seq_len)
            exta_ref[j, s * ra + off_a:s * ra + off_a + K_A - 1, :] = sa_ref[s, :, lanes]
            exta_ref[j, (s + 1) * ra - seq_len:(s + 1) * ra, :] = cv[rows, lanes]
    for j in range(n_lane):
        for t in range(seq_len):
            slabs_ref[j, t * n_seq:(t + 1) * n_seq, :] = (
                rows_ref[j, pl.ds(t, n_seq, stride=seq_len), :])

    def slab(j, i):
        if i < n_hist:
            return sb_ref[i, :, j * LANES:(j + 1) * LANES]
        return slabs_ref[j, (i - n_hist) * n_seq:(i - n_hist + 1) * n_seq, :]

    conv_a, conv_b = [], []
    for j in range(n_lane):
        lanes = slice(j * LANES, (j + 1) * LANES)
        a = []
        for s0 in range(0, n_seq, seq_group):
            group = range(s0, s0 + seq_group)
            a += _conv_windows(exta_ref, j, [s * ra + off_a for s in group], seq_len, wa_ref, K_A)
        conv_a.append(jnp.concatenate(a, axis=0))
        accs = [None] * seq_len
        for i in range(n_hist + seq_len):
            xi = slab(j, i)
            for t in range(seq_len):
                k = i - t
                if 0 <= k < K_B:
                    term = wb_ref[k, :, lanes] * xi
                    accs[t] = term if accs[t] is None else accs[t] + term
        for t in range(seq_len):
            rows_ref[j, pl.ds(t, n_seq, stride=seq_len), :] = accs[t]
        conv_b.append(rows_ref[j])
    y_cat = _mix(b_gate, jnp.concatenate(conv_a, axis=1), jnp.concatenate(conv_b, axis=1),
                 bias_ref, ln_g_ref, ln_b_ref)
    x1 = x + _dot(y_cat, w_out_ref[...])
    x2 = _ffn(x1, g_ffn_ref, w_gate_ref, w_up_ref, w_down_ref, ff_chunk)
    y_ref[...] = _rmsnorm(x2, g_final_ref[...])

    for j in range(n_lane):
        lanes = slice(j * LANES, (j + 1) * LANES)
        for s in range(n_seq):
            new_a_ref[s, :, lanes] = exta_ref[j, (s + 1) * ra - (K_A - 1):(s + 1) * ra, :]
        for r in range(n_hist):
            new_b_ref[r, :, lanes] = slab(j, seq_len + r)


N_PARAMS = 13
WEIGHT_SLOTS = (1, 7, 9, 10, 11)
WEIGHT_CHUNK_ROWS = 256
WEIGHT_RING = 4


def _convert_weights(w_hbm_refs, w_vmem_refs, stage_ref, sem_ref):
    ring, chunk = stage_ref.shape[0], stage_ref.shape[1]
    pieces = [(src, dst, r0) for src, dst in zip(w_hbm_refs, w_vmem_refs)
              for r0 in range(0, src.shape[0], chunk)]

    def fetch(p):
        src, _, r0 = pieces[p]
        return pltpu.make_async_copy(src.at[r0:r0 + chunk, :],
                                     stage_ref.at[p % ring, :, 0:src.shape[1]],
                                     sem_ref.at[p % ring])

    for p in range(min(ring - 1, len(pieces))):
        fetch(p).start()
    for p, (src, dst, r0) in enumerate(pieces):
        if p + ring - 1 < len(pieces):
            fetch(p + ring - 1).start()
        fetch(p).wait()
        dst[r0:r0 + chunk, :] = stage_ref[p % ring, :, 0:src.shape[1]].astype(BF16)


def _prompt_entry(*refs, n_steps, step):
    n_w = len(WEIGHT_SLOTS)
    x1_ref, x3_ref = refs[:2]
    params = list(refs[2:2 + N_PARAMS - n_w])
    w_hbm = refs[2 + N_PARAMS - n_w:2 + N_PARAMS]
    outs = refs[2 + N_PARAMS:5 + N_PARAMS]
    w_out_hbm = refs[5 + N_PARAMS:5 + N_PARAMS + n_w]
    scratch = refs[5 + N_PARAMS + n_w:]
    exta_ref, extb_ref, bg_ref, ycat_ref = scratch[:4]
    w_vmem = scratch[4:4 + n_w]
    stage_ref, sem_ref, out_sem_ref = scratch[4 + n_w:]
    g = pl.program_id(0)

    def copy_out(k):
        return pltpu.make_async_copy(w_vmem[k], w_out_hbm[k], out_sem_ref.at[k])

    @pl.when(g == 0)
    def _():
        _convert_weights(w_hbm, w_vmem, stage_ref, sem_ref)
        for k in range(n_w):
            copy_out(k).start()

    for slot, w in zip(WEIGHT_SLOTS, w_vmem):
        params.insert(slot, w)
    run = functools.partial(_prompt_kernel, x1_ref, x3_ref, *params, *outs,
                            exta_ref, extb_ref, bg_ref, ycat_ref, **step)
    skew = 2

    @pl.when(g < skew)
    def _():
        run(stage3=False)

    @pl.when(jnp.logical_and(g >= skew, g < n_steps - skew))
    def _():
        run()

    @pl.when(g >= n_steps - skew)
    def _():
        run(stage1=False)

    @pl.when(g == n_steps - 1)
    def _():
        for k in range(n_w):
            copy_out(k).wait()


def _round_up(n, m):
    return -(-n // m) * m


def _const_spec(shape):
    zeros = (0,) * len(shape)
    return pl.BlockSpec(shape, lambda g: zeros, pipeline_mode=pl.Buffered(1))


PROMPT_TILE = 256
PROMPT_ROW_CHUNK = 64
CONV_UNIT_BEFORE_FFN_CHUNK = {1: 0, 2: 1, 3: 2, 4: 3}
FF_CHUNK = 512
SAMPLE_SEQS = 32
SAMPLE_SEQ_GROUP = 16
VMEM_LIMIT_BYTES = 56 * 1024 * 1024


def kernel(x_prompt, x_sample, state_conv_a, state_conv_b, g_mix, w_in, conv_a_w, conv_b_w,
           conv_b_bias, ln_b_g, ln_b_b, w_out, g_ffn, w_gate, w_up, w_down, g_final):
    depth = g_mix.shape[0]
    assert depth == 1, "single-layer trunk"
    bp, seq, d = x_prompt.shape
    bs, dec_seq, _ = x_sample.shape
    gw = conv_a_w.shape[-1]
    assert conv_a_w.shape[1] == K_A and conv_b_w.shape[1] == K_B
    assert dec_seq == SUBLANES and gw % LANES == 0
    assert seq % PROMPT_TILE == 0 and bs % SAMPLE_SEQS == 0

    row = lambda a: a.reshape(1, -1)
    taps = lambda w: jnp.swapaxes(w, 0, 1)
    small = (row(g_mix[0]), taps(conv_a_w), taps(conv_b_w), row(conv_b_bias[0]), row(ln_b_g[0]),
             row(ln_b_b[0]), row(g_ffn[0]), row(g_final))
    weights = (w_in[0], w_out[0], w_gate[0], w_up[0], w_down[0])
    assert len(small) + len(weights) == N_PARAMS
    assert all(w.shape[0] % WEIGHT_CHUNK_ROWS == 0 for w in weights)

    tile = PROMPT_TILE
    tiles_per_seq = seq // tile
    n_tiles = bp * tiles_per_seq
    n_steps = n_tiles + 2
    hist_a = _round_up(K_A - 1, SUBLANES)
    hist_b = _round_up(K_B - 1, SUBLANES)
    n_lane = gw // LANES
    tile_1 = lambda g: jnp.minimum(g, n_tiles - 1)
    tile_3 = lambda g: jnp.maximum(g - 2, 0)
    any_spec = pl.BlockSpec(memory_space=pl.ANY)
    x_rows = x_prompt.reshape(bp * seq, d)
    y_prompt, pa, pb, *weights_bf16 = pl.pallas_call(
        functools.partial(
            _prompt_entry, n_steps=n_steps,
            step=dict(tile=tile, row_chunk=PROMPT_ROW_CHUNK, tiles_per_seq=tiles_per_seq,
                      ff_chunk=FF_CHUNK)),
        grid=(n_steps,),
        in_specs=[pl.BlockSpec((tile, d), lambda g: (tile_1(g), 0)),
                  pl.BlockSpec((tile, d), lambda g: (tile_3(g), 0))]
        + [_const_spec(p.shape) for p in small] + [any_spec] * len(weights),
        out_specs=[pl.BlockSpec((tile, d), lambda g: (tile_3(g), 0)),
                   pl.BlockSpec((None, K_A - 1, gw), lambda g: (tile_1(g) // tiles_per_seq, 0, 0)),
                   pl.BlockSpec((None, K_B - 1, gw), lambda g: (tile_1(g) // tiles_per_seq, 0, 0))]
        + [any_spec] * len(weights),
        out_shape=[jax.ShapeDtypeStruct((bp * seq, d), F32),
                   jax.ShapeDtypeStruct((bp, K_A - 1, gw), F32),
                   jax.ShapeDtypeStruct((bp, K_B - 1, gw), F32)]
        + [jax.ShapeDtypeStruct(w.shape, BF16) for w in weights],
        scratch_shapes=[pltpu.VMEM((n_lane, hist_a + tile, LANES), F32),
                        pltpu.VMEM((n_lane, hist_b + tile, LANES), F32),
                        pltpu.VMEM((tile, gw), F32),
                        pltpu.VMEM((tile, 2 * gw), BF16)]
        + [pltpu.VMEM(w.shape, BF16) for w in weights]
        + [pltpu.VMEM((WEIGHT_RING, WEIGHT_CHUNK_ROWS, max(w.shape[1] for w in weights)), F32),
           pltpu.SemaphoreType.DMA((WEIGHT_RING,)),
           pltpu.SemaphoreType.DMA((len(weights),))],
        compiler_params=pltpu.CompilerParams(
            dimension_semantics=("arbitrary",),
            vmem_limit_bytes=VMEM_LIMIT_BYTES),
        name="prompt_layer",
    )(x_rows, x_rows, *small, *weights)
    y_prompt = y_prompt.reshape(bp, seq, d)

    params = list(small)
    for slot, w in zip(WEIGHT_SLOTS, weights_bf16):
        params.insert(slot, w)
    by_row = lambda a: jnp.swapaxes(a, 0, 1)
    n_seq = SAMPLE_SEQS
    rows = n_seq * dec_seq
    ra = _round_up(K_A - 1, SUBLANES) + dec_seq
    y_sample, sa, sb = pl.pallas_call(
        functools.partial(_sample_kernel, n_seq=n_seq, seq_len=dec_seq,
                          seq_group=SAMPLE_SEQ_GROUP, ff_chunk=FF_CHUNK),
        grid=(bs // n_seq,),
        in_specs=[pl.BlockSpec((rows, d), lambda i: (i, 0)),
                  pl.BlockSpec((n_seq, K_A - 1, gw), lambda i: (i, 0, 0)),
                  pl.BlockSpec((K_B - 1, n_seq, gw), lambda i: (0, i, 0))]
        + [_const_spec(p.shape) for p in params],
        out_specs=[pl.BlockSpec((rows, d), lambda i: (i, 0)),
                   pl.BlockSpec((n_seq, K_A - 1, gw), lambda i: (i, 0, 0)),
                   pl.BlockSpec((K_B - 1, n_seq, gw), lambda i: (0, i, 0))],
        out_shape=[jax.ShapeDtypeStruct((bs * dec_seq, d), F32),
                   jax.ShapeDtypeStruct((bs, K_A - 1, gw), F32),
                   jax.ShapeDtypeStruct((K_B - 1, bs, gw), F32)],
        scratch_shapes=[pltpu.VMEM((n_lane, n_seq * ra, LANES), F32),
                        pltpu.VMEM((n_lane, rows, LANES), F32),
                        pltpu.VMEM((n_lane, rows, LANES), F32)],
        compiler_params=pltpu.CompilerParams(
            dimension_semantics=("arbitrary",),
            vmem_limit_bytes=VMEM_LIMIT_BYTES),
        name="sample_layer",
    )(x_sample.reshape(bs * dec_seq, d), state_conv_a[0], by_row(state_conv_b[0]), *params)

    return (y_prompt, y_sample.reshape(bs, dec_seq, d), pa[None], pb[None],
            sa[None], by_row(sb)[None])
```

```python
import functools

import jax
import jax.numpy as jnp
from jax import lax
from jax.experimental import pallas as pl
from jax.experimental.pallas import tpu as pltpu

EPS = 1e-6
K_A = 3
K_B = 31
LANES = 128
SUBLANES = 8

F32 = jnp.float32
BF16 = jnp.bfloat16


def _rmsnorm(x, g):
    ms = jnp.mean(x * x, axis=-1, keepdims=True)
    return x * lax.rsqrt(ms + EPS) * g


def _layernorm(x, g, b):
    mu = jnp.mean(x, axis=-1, keepdims=True)
    xc = x - mu
    var = jnp.mean(xc * xc, axis=-1, keepdims=True)
    return xc * lax.rsqrt(var + EPS) * g + b


def _sigmoid(x):
    return 1.0 / (1.0 + jnp.exp(-x))


def _silu(x):
    return x * _sigmoid(x)


def _dot(a, b):
    return jnp.dot(a, b, preferred_element_type=F32)


def _conv_windows(ext_ref, j, starts, nrows, w_ref, n_taps):
    lanes = slice(j * LANES, (j + 1) * LANES)
    accs = [None] * len(starts)
    for k in range(n_taps):
        wk = w_ref[k, :, lanes]
        for i, s in enumerate(starts):
            term = wk * ext_ref[j, pl.ds(s + k, nrows), :]
            accs[i] = term if accs[i] is None else accs[i] + term
    return accs


def _after(lhs, pieces, never):
    if not pieces:
        return lhs
    rows = 2 * SUBLANES
    total = None
    for p in pieces:
        for r in range(0, p.shape[0], rows):
            for c in range(0, p.shape[1], LANES):
                slab = p[r:r + rows, c:c + LANES].astype(lhs.dtype)
                total = slab if total is None else total + slab
    head = jnp.where(never, total, lhs[0:rows, 0:LANES])
    top = jnp.concatenate([head, lhs[0:rows, LANES:]], axis=1)
    return jnp.concatenate([top, lhs[rows:, :]], axis=0)


def _mix(b_gate, conv_a, conv_b, bias_ref, ln_g_ref, ln_b_ref):
    y_b = _silu(_layernorm(conv_b + bias_ref[...], ln_g_ref[...], ln_b_ref[...]))
    return jnp.concatenate([b_gate * conv_a, y_b], axis=-1).astype(BF16)


def _ffn(x1, h2, w_gate_ref, w_up_ref, w_down_ref, ff_chunk, hide=None, never=None):
    d_ff = w_gate_ref.shape[1]
    acc = None
    pending = None
    for c, c0 in enumerate(range(0, d_ff, ff_chunk)):
        cols = slice(c0, min(c0 + ff_chunk, d_ff))
        lhs = _after(h2, (hide or {}).get(c), never)
        gate = _dot(lhs, w_gate_ref[:, cols])
        up = _dot(lhs, w_up_ref[:, cols])
        if pending is not None:
            part = _dot(pending[0], w_down_ref[pending[1], :])
            acc = part if acc is None else acc + part
        pending = ((_silu(gate) * up).astype(BF16), cols)
    part = _dot(pending[0], w_down_ref[pending[1], :])
    acc = part if acc is None else acc + part
    return x1 + acc


def _prompt_kernel(x1_ref, x3_ref, g_mix_ref, w_in_ref, wa_ref, wb_ref, bias_ref, ln_g_ref,
                   ln_b_ref, w_out_ref, g_ffn_ref, w_gate_ref, w_up_ref, w_down_ref, g_final_ref,
                   y_ref, new_a_ref, new_b_ref, exta_ref, extb_ref, bg_ref, ycat_ref,
                   *, tile, row_chunk, tiles_per_seq, ff_chunk, stage1=True, stage3=True):
    assert stage1 or stage3
    gw = wa_ref.shape[-1]
    n_lane = gw // LANES
    hist_a = exta_ref.shape[1] - tile
    hist_b = extb_ref.shape[1] - tile
    g = pl.program_id(0)
    never = g < 0

    @pl.when(g == 0)
    def _():
        exta_ref[...] = jnp.zeros(exta_ref.shape, F32)
        extb_ref[...] = jnp.zeros(extb_ref.shape, F32)
        bg_ref[...] = jnp.zeros(bg_ref.shape, F32)

    if stage3:
        x1 = x3_ref[...] + _dot(ycat_ref[...], w_out_ref[...])

    starts = list(range(0, tile, row_chunk))
    units = []
    for j in range(n_lane):
        a = _conv_windows(exta_ref, j, [hist_a - (K_A - 1) + s for s in starts], row_chunk,
                          wa_ref, K_A)
        b = _conv_windows(extb_ref, j, [hist_b - (K_B - 1) + s for s in starts], row_chunk,
                          wb_ref, K_B)
        units.append(a + b)
    conv_a = [jnp.concatenate(u[:len(starts)], axis=0) for u in units]
    conv_b = [jnp.concatenate(u[len(starts):], axis=0) for u in units]
    y_cat = _mix(bg_ref[...], jnp.concatenate(conv_a, axis=1), jnp.concatenate(conv_b, axis=1),
                 bias_ref, ln_g_ref, ln_b_ref)
    tail_a = exta_ref[:, tile:tile + hist_a, :]
    tail_b = extb_ref[:, tile:tile + hist_b, :]

    if stage1:
        h = _rmsnorm(x1_ref[...], g_mix_ref[...]).astype(BF16)
        in_proj = lambda k, wait=None: _dot(_after(h, wait, never),
                                            w_in_ref[:, k * gw:(k + 1) * gw])
        glu_val, glu_gate = in_proj(3), in_proj(4)
    if stage3:
        h2 = _rmsnorm(x1, g_ffn_ref[...]).astype(BF16)
        x2 = _ffn(x1, h2, w_gate_ref, w_up_ref, w_down_ref, ff_chunk,
                  hide={c: units[j] for c, j in CONV_UNIT_BEFORE_FFN_CHUNK.items()}, never=never)
        c_gate, v, b_gate = (in_proj(1), in_proj(2), in_proj(0)) if stage1 else (None,) * 3
    else:
        c_gate, v = in_proj(1, units[0]), in_proj(2, units[1])
        b_gate = in_proj(0, units[2] + units[3])
    if stage3:
        y_ref[...] = _rmsnorm(x2, g_final_ref[...])

    if stage1:
        seq_start = (g % tiles_per_seq) == 0
        cv = c_gate * v
        u_b = glu_val * _sigmoid(glu_gate)
        exta_ref[:, 0:hist_a, :] = jnp.where(seq_start, 0.0, tail_a)
        extb_ref[:, 0:hist_b, :] = jnp.where(seq_start, 0.0, tail_b)
        for j in range(n_lane):
            lanes = slice(j * LANES, (j + 1) * LANES)
            exta_ref[j, hist_a:hist_a + tile, :] = cv[:, lanes]
            extb_ref[j, hist_b:hist_b + tile, :] = u_b[:, lanes]
        bg_ref[...] = b_gate
        new_a_ref[...] = cv[tile - (K_A - 1):, :]
        new_b_ref[...] = u_b[tile - (K_B - 1):, :]

    ycat_ref[...] = y_cat


def _sample_mixer(x_ref, sa_ref, sb_ref, g_mix_ref, w_in_ref, wa_ref, wb_ref, bias_ref,
                  ln_g_ref, ln_b_ref, new_a_ref, new_b_ref, exta_ref, rows_ref, slabs_ref,
                  *, n_seq, seq_len, seq_group):
    gw = wa_ref.shape[-1]
    n_lane = gw // LANES
    ra = exta_ref.shape[1] // n_seq
    off_a = ra - seq_len - (K_A - 1)
    n_hist = K_B - 1

    h = _rmsnorm(x_ref[...], g_mix_ref[...]).astype(BF16)
    in_proj = lambda k: _dot(h, w_in_ref[:, k * gw:(k + 1) * gw])
    u_b = in_proj(3) * _sigmoid(in_proj(4))
    for j in range(n_lane):
        rows_ref[j] = u_b[:, j * LANES:(j + 1) * LANES]
        for t in range(seq_len):
            slabs_ref[j, t * n_seq:(t + 1) * n_seq, :] = (
                rows_ref[j, pl.ds(t, n_seq, stride=seq_len), :])

    def slab(j, i):
        if i < n_hist:
            return sb_ref[i, :, j * LANES:(j + 1) * LANES]
        return slabs_ref[j, (i - n_hist) * n_seq:(i - n_hist + 1) * n_seq, :]

    conv_b = []
    for j in range(n_lane):
        lanes = slice(j * LANES, (j + 1) * LANES)
        accs = [None] * seq_len
        for i in range(n_hist + seq_len):
            xi = slab(j, i)
            for t in range(seq_len):
                k = i - t
                if 0 <= k < K_B:
                    term = wb_ref[k, :, lanes] * xi
                    accs[t] = term if accs[t] is None else accs[t] + term
        for t in range(seq_len):
            rows_ref[j, pl.ds(t, n_seq, stride=seq_len), :] = accs[t]
        conv_b.append(rows_ref[j])

    cv = in_proj(1) * in_proj(2)
    b_gate = in_proj(0)
    conv_a = []
    for j in range(n_lane):
        lanes = slice(j * LANES, (j + 1) * LANES)
        for s in range(n_seq):
            rows = slice(s * seq_len, (s + 1) * seq_len)
            exta_ref[j, s * ra + off_a:s * ra + off_a + K_A - 1, :] = sa_ref[s, :, lanes]
            exta_ref[j, (s + 1) * ra - seq_len:(s + 1) * ra, :] = cv[rows, lanes]
        a = []
        for s0 in range(0, n_seq, seq_group):
            group = range(s0, s0 + seq_group)
            a += _conv_windows(exta_ref, j, [s * ra + off_a for s in group], seq_len, wa_ref, K_A)
        conv_a.append(jnp.concatenate(a, axis=0))

    for j in range(n_lane):
        lanes = slice(j * LANES, (j + 1) * LANES)
        for s in range(n_seq):
            new_a_ref[s, :, lanes] = exta_ref[j, (s + 1) * ra - (K_A - 1):(s + 1) * ra, :]
        for r in range(n_hist):
            new_b_ref[r, :, lanes] = slab(j, seq_len + r)
    return _mix(b_gate, jnp.concatenate(conv_a, axis=1), jnp.concatenate(conv_b, axis=1),
                bias_ref, ln_g_ref, ln_b_ref)


def _sample_kernel(x_ref, sa_ref, sb_ref, g_mix_ref, w_in_ref, wa_ref, wb_ref, bias_ref,
                   ln_g_ref, ln_b_ref, w_out_ref, g_ffn_ref, w_gate_ref, w_up_ref, w_down_ref,
                   g_final_ref, y_ref, new_a_ref, new_b_ref, exta_ref, rows_ref, slabs_ref,
                   *, ff_chunk, **mixer):
    y_cat = _sample_mixer(x_ref, sa_ref, sb_ref, g_mix_ref, w_in_ref, wa_ref, wb_ref, bias_ref,
                          ln_g_ref, ln_b_ref, new_a_ref, new_b_ref, exta_ref, rows_ref,
                          slabs_ref, **mixer)
    x1 = x_ref[...] + _dot(y_cat, w_out_ref[...])
    h2 = _rmsnorm(x1, g_ffn_ref[...]).astype(BF16)
    x2 = _ffn(x1, h2, w_gate_ref, w_up_ref, w_down_ref, ff_chunk)
    y_ref[...] = _rmsnorm(x2, g_final_ref[...])


N_PARAMS = 13
WEIGHT_SLOTS = (1, 7, 9, 10, 11)
WEIGHT_CHUNK_ROWS = 256
WEIGHT_RING = 4


def _convert_weights(w_hbm_refs, w_vmem_refs, stage_ref, sem_ref):
    ring, chunk = stage_ref.shape[0], stage_ref.shape[1]
    pieces = [(src, dst, r0) for src, dst in zip(w_hbm_refs, w_vmem_refs)
              for r0 in range(0, src.shape[0], chunk)]

    def fetch(p):
        src, _, r0 = pieces[p]
        return pltpu.make_async_copy(src.at[r0:r0 + chunk, :],
                                     stage_ref.at[p % ring, :, 0:src.shape[1]],
                                     sem_ref.at[p % ring])

    for p in range(min(ring - 1, len(pieces))):
        fetch(p).start()
    for p, (src, dst, r0) in enumerate(pieces):
        if p + ring - 1 < len(pieces):
            fetch(p + ring - 1).start()
        fetch(p).wait()
        dst[r0:r0 + chunk, :] = stage_ref[p % ring, :, 0:src.shape[1]].astype(BF16)


def _prompt_entry(*refs, n_steps, step):
    n_w = len(WEIGHT_SLOTS)
    x1_ref, x3_ref = refs[:2]
    params = list(refs[2:2 + N_PARAMS - n_w])
    w_hbm = refs[2 + N_PARAMS - n_w:2 + N_PARAMS]
    outs = refs[2 + N_PARAMS:5 + N_PARAMS]
    w_out_hbm = refs[5 + N_PARAMS:5 + N_PARAMS + n_w]
    scratch = refs[5 + N_PARAMS + n_w:]
    exta_ref, extb_ref, bg_ref, ycat_ref = scratch[:4]
    w_vmem = scratch[4:4 + n_w]
    stage_ref, sem_ref, out_sem_ref = scratch[4 + n_w:]
    g = pl.program_id(0)

    def copy_out(k):
        return pltpu.make_async_copy(w_vmem[k], w_out_hbm[k], out_sem_ref.at[k])

    @pl.when(g == 0)
    def _():
        _convert_weights(w_hbm, w_vmem, stage_ref, sem_ref)
        for k in range(n_w):
            copy_out(k).start()

    for slot, w in zip(WEIGHT_SLOTS, w_vmem):
        params.insert(slot, w)
    run = functools.partial(_prompt_kernel, x1_ref, x3_ref, *params, *outs,
                            exta_ref, extb_ref, bg_ref, ycat_ref, **step)
    skew = 2

    @pl.when(g < skew)
    def _():
        run(stage3=False)

    @pl.when(jnp.logical_and(g >= skew, g < n_steps - skew))
    def _():
        run()

    @pl.when(g >= n_steps - skew)
    def _():
        run(stage1=False)

    @pl.when(g == n_steps - 1)
    def _():
        for k in range(n_w):
            copy_out(k).wait()


def _round_up(n, m):
    return -(-n // m) * m


def _const_spec(shape):
    zeros = (0,) * len(shape)
    return pl.BlockSpec(shape, lambda g: zeros, pipeline_mode=pl.Buffered(1))


PROMPT_TILE = 256
PROMPT_ROW_CHUNK = 64
CONV_UNIT_BEFORE_FFN_CHUNK = {1: 0, 2: 1, 3: 2, 4: 3}
FF_CHUNK = 512
SAMPLE_SEQS = 32
SAMPLE_SEQ_GROUP = 16
VMEM_LIMIT_BYTES = 56 * 1024 * 1024


def kernel(x_prompt, x_sample, state_conv_a, state_conv_b, g_mix, w_in, conv_a_w, conv_b_w,
           conv_b_bias, ln_b_g, ln_b_b, w_out, g_ffn, w_gate, w_up, w_down, g_final):
    depth = g_mix.shape[0]
    assert depth == 1, "single-layer trunk"
    bp, seq, d = x_prompt.shape
    bs, dec_seq, _ = x_sample.shape
    gw = conv_a_w.shape[-1]
    assert conv_a_w.shape[1] == K_A and conv_b_w.shape[1] == K_B
    assert dec_seq == SUBLANES and gw % LANES == 0
    assert seq % PROMPT_TILE == 0 and bs % SAMPLE_SEQS == 0

    row = lambda a: a.reshape(1, -1)
    taps = lambda w: jnp.swapaxes(w, 0, 1)
    small = (row(g_mix[0]), taps(conv_a_w), taps(conv_b_w), row(conv_b_bias[0]), row(ln_b_g[0]),
             row(ln_b_b[0]), row(g_ffn[0]), row(g_final))
    weights = (w_in[0], w_out[0], w_gate[0], w_up[0], w_down[0])
    assert len(small) + len(weights) == N_PARAMS
    assert all(w.shape[0] % WEIGHT_CHUNK_ROWS == 0 for w in weights)

    tile = PROMPT_TILE
    tiles_per_seq = seq // tile
    n_tiles = bp * tiles_per_seq
    n_steps = n_tiles + 2
    hist_a = _round_up(K_A - 1, SUBLANES)
    hist_b = _round_up(K_B - 1, SUBLANES)
    n_lane = gw // LANES
    tile_1 = lambda g: jnp.minimum(g, n_tiles - 1)
    tile_3 = lambda g: jnp.maximum(g - 2, 0)
    any_spec = pl.BlockSpec(memory_space=pl.ANY)
    x_rows = x_prompt.reshape(bp * seq, d)
    y_prompt, pa, pb, *weights_bf16 = pl.pallas_call(
        functools.partial(
            _prompt_entry, n_steps=n_steps,
            step=dict(tile=tile, row_chunk=PROMPT_ROW_CHUNK, tiles_per_seq=tiles_per_seq,
                      ff_chunk=FF_CHUNK)),
        grid=(n_steps,),
        in_specs=[pl.BlockSpec((tile, d), lambda g: (tile_1(g), 0)),
                  pl.BlockSpec((tile, d), lambda g: (tile_3(g), 0))]
        + [_const_spec(p.shape) for p in small] + [any_spec] * len(weights),
        out_specs=[pl.BlockSpec((tile, d), lambda g: (tile_3(g), 0)),
                   pl.BlockSpec((None, K_A - 1, gw), lambda g: (tile_1(g) // tiles_per_seq, 0, 0)),
                   pl.BlockSpec((None, K_B - 1, gw), lambda g: (tile_1(g) // tiles_per_seq, 0, 0))]
        + [any_spec] * len(weights),
        out_shape=[jax.ShapeDtypeStruct((bp * seq, d), F32),
                   jax.ShapeDtypeStruct((bp, K_A - 1, gw), F32),
                   jax.ShapeDtypeStruct((bp, K_B - 1, gw), F32)]
        + [jax.ShapeDtypeStruct(w.shape, BF16) for w in weights],
        scratch_shapes=[pltpu.VMEM((n_lane, hist_a + tile, LANES), F32),
                        pltpu.VMEM((n_lane, hist_b + tile, LANES), F32),
                        pltpu.VMEM((tile, gw), F32),
                        pltpu.VMEM((tile, 2 * gw), BF16)]
        + [pltpu.VMEM(w.shape, BF16) for w in weights]
        + [pltpu.VMEM((WEIGHT_RING, WEIGHT_CHUNK_ROWS, max(w.shape[1] for w in weights)), F32),
           pltpu.SemaphoreType.DMA((WEIGHT_RING,)),
           pltpu.SemaphoreType.DMA((len(weights),))],
        compiler_params=pltpu.CompilerParams(
            dimension_semantics=("arbitrary",),
            vmem_limit_bytes=VMEM_LIMIT_BYTES),
        name="prompt_layer",
    )(x_rows, x_rows, *small, *weights)
    y_prompt = y_prompt.reshape(bp, seq, d)

    params = list(small)
    for slot, w in zip(WEIGHT_SLOTS, weights_bf16):
        params.insert(slot, w)
    by_row = lambda a: jnp.swapaxes(a, 0, 1)
    n_seq = SAMPLE_SEQS
    rows = n_seq * dec_seq
    ra = _round_up(K_A - 1, SUBLANES) + dec_seq
    y_sample, sa, sb = pl.pallas_call(
        functools.partial(_sample_kernel, n_seq=n_seq, seq_len=dec_seq,
                          seq_group=SAMPLE_SEQ_GROUP, ff_chunk=FF_CHUNK),
        grid=(bs // n_seq,),
        in_specs=[pl.BlockSpec((rows, d), lambda i: (i, 0)),
                  pl.BlockSpec((n_seq, K_A - 1, gw), lambda i: (i, 0, 0)),
                  pl.BlockSpec((K_B - 1, n_seq, gw), lambda i: (0, i, 0))]
        + [_const_spec(p.shape) for p in params],
        out_specs=[pl.BlockSpec((rows, d), lambda i: (i, 0)),
                   pl.BlockSpec((n_seq, K_A - 1, gw), lambda i: (i, 0, 0)),
                   pl.BlockSpec((K_B - 1, n_seq, gw), lambda i: (0, i, 0))],
        out_shape=[jax.ShapeDtypeStruct((bs * dec_seq, d), F32),
                   jax.ShapeDtypeStruct((bs, K_A - 1, gw), F32),
                   jax.ShapeDtypeStruct((K_B - 1, bs, gw), F32)],
        scratch_shapes=[pltpu.VMEM((n_lane, n_seq * ra, LANES), F32),
                        pltpu.VMEM((n_lane, rows, LANES), F32),
                        pltpu.VMEM((n_lane, rows, LANES), F32)],
        compiler_params=pltpu.CompilerParams(
            dimension_semantics=("arbitrary",),
            vmem_limit_bytes=VMEM_LIMIT_BYTES),
        name="sample_layer",
    )(x_sample.reshape(bs * dec_seq, d), state_conv_a[0], by_row(state_conv_b[0]), *params)

    return (y_prompt, y_sample.reshape(bs, dec_seq, d), pa[None], pb[None],
            sa[None], by_row(sb)[None])
```

```python
import functools

import jax
import jax.numpy as jnp
from jax import lax
from jax.experimental import pallas as pl
from jax.experimental.pallas import tpu as pltpu

EPS = 1e-6
K_A = 3
K_B = 31
LANES = 128
SUBLANES = 8

F32 = jnp.float32
BF16 = jnp.bfloat16


def _rmsnorm(x, g):
    ms = jnp.mean(x * x, axis=-1, keepdims=True)
    return x * lax.rsqrt(ms + EPS) * g


def _layernorm(x, g, b):
    mu = jnp.mean(x, axis=-1, keepdims=True)
    xc = x - mu
    var = jnp.mean(xc * xc, axis=-1, keepdims=True)
    return xc * lax.rsqrt(var + EPS) * g + b


def _sigmoid(x):
    return 1.0 / (1.0 + jnp.exp(-x))


def _silu(x):
    return x * _sigmoid(x)


def _dot(a, b):
    return jnp.dot(a, b, preferred_element_type=F32)


def _conv_windows(ext_ref, j, starts, nrows, w_ref, n_taps):
    lanes = slice(j * LANES, (j + 1) * LANES)
    accs = [None] * len(starts)
    for k in range(n_taps):
        wk = w_ref[k, :, lanes]
        for i, s in enumerate(starts):
            term = wk * ext_ref[j, pl.ds(s + k, nrows), :]
            accs[i] = term if accs[i] is None else accs[i] + term
    return accs


def _after(lhs, pieces, never):
    if not pieces:
        return lhs
    rows = 2 * SUBLANES
    total = None
    for p in pieces:
        for r in range(0, p.shape[0], rows):
            for c in range(0, p.shape[1], LANES):
                slab = p[r:r + rows, c:c + LANES].astype(lhs.dtype)
                total = slab if total is None else total + slab
    head = jnp.where(never, total, lhs[0:rows, 0:LANES])
    top = jnp.concatenate([head, lhs[0:rows, LANES:]], axis=1)
    return jnp.concatenate([top, lhs[rows:, :]], axis=0)


def _mix(b_gate, conv_a, conv_b, bias_ref, ln_g_ref, ln_b_ref):
    y_b = _silu(_layernorm(conv_b + bias_ref[...], ln_g_ref[...], ln_b_ref[...]))
    return jnp.concatenate([b_gate * conv_a, y_b], axis=-1).astype(BF16)


def _ffn(x1, h2, w_gate_ref, w_up_ref, w_down_ref, ff_chunk, hide=None, never=None):
    d_ff = w_gate_ref.shape[1]
    acc = None
    pending = None
    for c, c0 in enumerate(range(0, d_ff, ff_chunk)):
        cols = slice(c0, min(c0 + ff_chunk, d_ff))
        lhs = _after(h2, (hide or {}).get(c), never)
        gate = _dot(lhs, w_gate_ref[:, cols])
        up = _dot(lhs, w_up_ref[:, cols])
        if pending is not None:
            part = _dot(pending[0], w_down_ref[pending[1], :])
            acc = part if acc is None else acc + part
        pending = ((_silu(gate) * up).astype(BF16), cols)
    part = _dot(pending[0], w_down_ref[pending[1], :])
    acc = part if acc is None else acc + part
    return x1 + acc


def _prompt_kernel(x1_ref, x3_ref, g_mix_ref, w_in_ref, wa_ref, wb_ref, bias_ref, ln_g_ref,
                   ln_b_ref, w_out_ref, g_ffn_ref, w_gate_ref, w_up_ref, w_down_ref, g_final_ref,
                   y_ref, new_a_ref, new_b_ref, exta_ref, extb_ref, bg_ref, ycat_ref,
                   *, tile, row_chunk, tiles_per_seq, ff_chunk, stage1=True, stage3=True):
    assert stage1 or stage3
    gw = wa_ref.shape[-1]
    n_lane = gw // LANES
    hist_a = exta_ref.shape[1] - tile
    hist_b = extb_ref.shape[1] - tile
    g = pl.program_id(0)
    never = g < 0

    @pl.when(g == 0)
    def _():
        exta_ref[...] = jnp.zeros(exta_ref.shape, F32)
        extb_ref[...] = jnp.zeros(extb_ref.shape, F32)
        bg_ref[...] = jnp.zeros(bg_ref.shape, F32)

    if stage3:
        x1 = x3_ref[...] + _dot(ycat_ref[...], w_out_ref[...])

    starts = list(range(0, tile, row_chunk))
    units = []
    for j in range(n_lane):
        a = _conv_windows(exta_ref, j, [hist_a - (K_A - 1) + s for s in starts], row_chunk,
                          wa_ref, K_A)
        b = _conv_windows(extb_ref, j, [hist_b - (K_B - 1) + s for s in starts], row_chunk,
                          wb_ref, K_B)
        units.append(a + b)
    conv_a = [jnp.concatenate(u[:len(starts)], axis=0) for u in units]
    conv_b = [jnp.concatenate(u[len(starts):], axis=0) for u in units]
    y_cat = _mix(bg_ref[...], jnp.concatenate(conv_a, axis=1), jnp.concatenate(conv_b, axis=1),
                 bias_ref, ln_g_ref, ln_b_ref)
    tail_a = exta_ref[:, tile:tile + hist_a, :]
    tail_b = extb_ref[:, tile:tile + hist_b, :]

    if stage1:
        h = _rmsnorm(x1_ref[...], g_mix_ref[...]).astype(BF16)
        in_proj = lambda k, wait=None: _dot(_after(h, wait, never),
                                            w_in_ref[:, k * gw:(k + 1) * gw])
        glu_val, glu_gate = in_proj(3), in_proj(4)
    if stage3:
        h2 = _rmsnorm(x1, g_ffn_ref[...]).astype(BF16)
        x2 = _ffn(x1, h2, w_gate_ref, w_up_ref, w_down_ref, ff_chunk,
                  hide={c: units[j] for c, j in CONV_UNIT_BEFORE_FFN_CHUNK.items()}, never=never)
        c_gate, v, b_gate = (in_proj(1), in_proj(2), in_proj(0)) if stage1 else (None,) * 3
    else:
        c_gate, v = in_proj(1, units[0]), in_proj(2, units[1])
        b_gate = in_proj(0, units[2] + units[3])
    if stage3:
        y_ref[...] = _rmsnorm(x2, g_final_ref[...])

    if stage1:
        seq_start = (g % tiles_per_seq) == 0
        cv = c_gate * v
        u_b = glu_val * _sigmoid(glu_gate)
        exta_ref[:, 0:hist_a, :] = jnp.where(seq_start, 0.0, tail_a)
        extb_ref[:, 0:hist_b, :] = jnp.where(seq_start, 0.0, tail_b)
        for j in range(n_lane):
            lanes = slice(j * LANES, (j + 1) * LANES)
            exta_ref[j, hist_a:hist_a + tile, :] = cv[:, lanes]
            extb_ref[j, hist_b:hist_b + tile, :] = u_b[:, lanes]
        bg_ref[...] = b_gate
        new_a_ref[...] = cv[tile - (K_A - 1):, :]
        new_b_ref[...] = u_b[tile - (K_B - 1):, :]

    ycat_ref[...] = y_cat


def _sample_mixer(x_ref, sa_ref, sb_ref, g_mix_ref, w_in_ref, wa_ref, wb_ref, bias_ref,
                  ln_g_ref, ln_b_ref, new_a_ref, new_b_ref, exta_ref, rows_ref, slabs_ref,
                  *, n_seq, seq_len, seq_group):
    gw = wa_ref.shape[-1]
    n_lane = gw // LANES
    ra = exta_ref.shape[1] // n_seq
    off_a = ra - seq_len - (K_A - 1)
    n_hist = K_B - 1

    h = _rmsnorm(x_ref[...], g_mix_ref[...]).astype(BF16)
    in_proj = lambda k: _dot(h, w_in_ref[:, k * gw:(k + 1) * gw])
    u_b = in_proj(3) * _sigmoid(in_proj(4))
    for j in range(n_lane):
        rows_ref[j] = u_b[:, j * LANES:(j + 1) * LANES]
        for t in range(seq_len):
            slabs_ref[j, t * n_seq:(t + 1) * n_seq, :] = (
                rows_ref[j, pl.ds(t, n_seq, stride=seq_len), :])

    def slab(j, i):
        if i < n_hist:
            return sb_ref[i, :, j * LANES:(j + 1) * LANES]
        return slabs_ref[j, (i - n_hist) * n_seq:(i - n_hist + 1) * n_seq, :]

    conv_b = []
    for j in range(n_lane):
        lanes = slice(j * LANES, (j + 1) * LANES)
        accs = [None] * seq_len
        for i in range(n_hist + seq_len):
            xi = slab(j, i)
            for t in range(seq_len):
                k = i - t
                if 0 <= k < K_B:
                    term = wb_ref[k, :, lanes] * xi
                    accs[t] = term if accs[t] is None else accs[t] + term
        for t in range(seq_len):
            rows_ref[j, pl.ds(t, n_seq, stride=seq_len), :] = accs[t]
        conv_b.append(rows_ref[j])

    cv = in_proj(1) * in_proj(2)
    b_gate = in_proj(0)
    conv_a = []
    for j in range(n_lane):
        lanes = slice(j * LANES, (j + 1) * LANES)
        for s in range(n_seq):
            rows = slice(s * seq_len, (s + 1) * seq_len)
            exta_ref[j, s * ra + off_a:s * ra + off_a + K_A - 1, :] = sa_ref[s, :, lanes]
            exta_ref[j, (s + 1) * ra - seq_len:(s + 1) * ra, :] = cv[rows, lanes]
        a = []
        for s0 in range(0, n_seq, seq_group):
            group = range(s0, s0 + seq_group)
            a += _conv_windows(exta_ref, j, [s * ra + off_a for s in group], seq_len, wa_ref, K_A)
        conv_a.append(jnp.concatenate(a, axis=0))

    for j in range(n_lane):
        lanes = slice(j * LANES, (j + 1) * LANES)
        for s in range(n_seq):
            new_a_ref[s, :, lanes] = exta_ref[j, (s + 1) * ra - (K_A - 1):(s + 1) * ra, :]
        for r in range(n_hist):
            new_b_ref[r, :, lanes] = slab(j, seq_len + r)
    return _mix(b_gate, jnp.concatenate(conv_a, axis=1), jnp.concatenate(conv_b, axis=1),
                bias_ref, ln_g_ref, ln_b_ref)


def _sample_kernel(x_ref, sa_ref, sb_ref, g_mix_ref, w_in_ref, wa_ref, wb_ref, bias_ref,
                   ln_g_ref, ln_b_ref, w_out_ref, g_ffn_ref, w_gate_ref, w_up_ref, w_down_ref,
                   g_final_ref, y_ref, new_a_ref, new_b_ref, exta_ref, rows_ref, slabs_ref,
                   *, ff_chunk, **mixer):
    y_cat = _sample_mixer(x_ref, sa_ref, sb_ref, g_mix_ref, w_in_ref, wa_ref, wb_ref, bias_ref,
                          ln_g_ref, ln_b_ref, new_a_ref, new_b_ref, exta_ref, rows_ref,
                          slabs_ref, **mixer)
    x1 = x_ref[...] + _dot(y_cat, w_out_ref[...])
    h2 = _rmsnorm(x1, g_ffn_ref[...]).astype(BF16)
    x2 = _ffn(x1, h2, w_gate_ref, w_up_ref, w_down_ref, ff_chunk)
    y_ref[...] = _rmsnorm(x2, g_final_ref[...])


N_PARAMS = 13
WEIGHT_SLOTS = (1, 7, 9, 10, 11)
CONVERT_ORDER = (0, 2, 3, 4, 1)
WEIGHT_CHUNK_ROWS = 256
WEIGHT_RING = 4


def _weight_pieces(w_hbm_refs, w_vmem_refs, chunk):
    return [(w_hbm_refs[k], w_vmem_refs[k], r0) for k in CONVERT_ORDER
            for r0 in range(0, w_hbm_refs[k].shape[0], chunk)]


def _convert_weights(pieces, first, last, stage_ref, sem_ref):
    ring, chunk = stage_ref.shape[0], stage_ref.shape[1]

    def fetch(p):
        src, _, r0 = pieces[p]
        return pltpu.make_async_copy(src.at[r0:r0 + chunk, :],
                                     stage_ref.at[p % ring, :, 0:src.shape[1]],
                                     sem_ref.at[p % ring])

    if first == 0:
        for p in range(min(ring - 1, len(pieces))):
            fetch(p).start()
    for p in range(first, last):
        src, dst, r0 = pieces[p]
        if p + ring - 1 < len(pieces):
            fetch(p + ring - 1).start()
        fetch(p).wait()
        dst[r0:r0 + chunk, :] = stage_ref[p % ring, :, 0:src.shape[1]].astype(BF16)


def _prompt_entry(*refs, n_steps, step):
    n_w = len(WEIGHT_SLOTS)
    x1_ref, x3_ref = refs[:2]
    params = list(refs[2:2 + N_PARAMS - n_w])
    w_hbm = refs[2 + N_PARAMS - n_w:2 + N_PARAMS]
    outs = refs[2 + N_PARAMS:5 + N_PARAMS]
    w_out_hbm = refs[5 + N_PARAMS:5 + N_PARAMS + n_w]
    scratch = refs[5 + N_PARAMS + n_w:]
    exta_ref, extb_ref, bg_ref, ycat_ref = scratch[:4]
    w_vmem = scratch[4:4 + n_w]
    stage_ref, sem_ref, out_sem_ref = scratch[4 + n_w:]
    g = pl.program_id(0)

    def copy_out(k):
        return pltpu.make_async_copy(w_vmem[k], w_out_hbm[k], out_sem_ref.at[k])

    pieces = _weight_pieces(w_hbm, w_vmem, stage_ref.shape[1])
    n_first = w_hbm[CONVERT_ORDER[0]].shape[0] // stage_ref.shape[1]
    n_second = n_first + stage_ref.shape[0] - 1
    for at_step, (first, last) in enumerate(
            [(0, n_first), (n_first, n_second), (n_second, len(pieces))]):
        @pl.when(g == at_step)
        def _(first=first, last=last):
            _convert_weights(pieces, first, last, stage_ref, sem_ref)
            if last == len(pieces):
                for k in range(n_w):
                    copy_out(k).start()

    for slot, w in zip(WEIGHT_SLOTS, w_vmem):
        params.insert(slot, w)
    run = functools.partial(_prompt_kernel, x1_ref, x3_ref, *params, *outs,
                            exta_ref, extb_ref, bg_ref, ycat_ref, **step)
    skew = 2

    @pl.when(g < skew)
    def _():
        run(stage3=False)

    @pl.when(jnp.logical_and(g >= skew, g < n_steps - skew))
    def _():
        run()

    @pl.when(g >= n_steps - skew)
    def _():
        run(stage1=False)

    @pl.when(g == n_steps - 1)
    def _():
        for k in range(n_w):
            copy_out(k).wait()


def _round_up(n, m):
    return -(-n // m) * m


def _const_spec(shape):
    zeros = (0,) * len(shape)
    return pl.BlockSpec(shape, lambda g: zeros, pipeline_mode=pl.Buffered(1))


PROMPT_TILE = 256
PROMPT_ROW_CHUNK = 64
CONV_UNIT_BEFORE_FFN_CHUNK = {1: 0, 2: 1, 3: 2, 4: 3}
FF_CHUNK = 512
SAMPLE_SEQS = 32
SAMPLE_SEQ_GROUP = 16
VMEM_LIMIT_BYTES = 56 * 1024 * 1024


def kernel(x_prompt, x_sample, state_conv_a, state_conv_b, g_mix, w_in, conv_a_w, conv_b_w,
           conv_b_bias, ln_b_g, ln_b_b, w_out, g_ffn, w_gate, w_up, w_down, g_final):
    depth = g_mix.shape[0]
    assert depth == 1, "single-layer trunk"
    bp, seq, d = x_prompt.shape
    bs, dec_seq, _ = x_sample.shape
    gw = conv_a_w.shape[-1]
    assert conv_a_w.shape[1] == K_A and conv_b_w.shape[1] == K_B
    assert dec_seq == SUBLANES and gw % LANES == 0
    assert seq % PROMPT_TILE == 0 and bs % SAMPLE_SEQS == 0

    row = lambda a: a.reshape(1, -1)
    taps = lambda w: jnp.swapaxes(w, 0, 1)
    small = (row(g_mix[0]), taps(conv_a_w), taps(conv_b_w), row(conv_b_bias[0]), row(ln_b_g[0]),
             row(ln_b_b[0]), row(g_ffn[0]), row(g_final))
    weights = (w_in[0], w_out[0], w_gate[0], w_up[0], w_down[0])
    assert len(small) + len(weights) == N_PARAMS
    assert all(w.shape[0] % WEIGHT_CHUNK_ROWS == 0 for w in weights)

    tile = PROMPT_TILE
    tiles_per_seq = seq // tile
    n_tiles = bp * tiles_per_seq
    n_steps = n_tiles + 2
    hist_a = _round_up(K_A - 1, SUBLANES)
    hist_b = _round_up(K_B - 1, SUBLANES)
    n_lane = gw // LANES
    tile_1 = lambda g: jnp.minimum(g, n_tiles - 1)
    tile_3 = lambda g: jnp.maximum(g - 2, 0)
    any_spec = pl.BlockSpec(memory_space=pl.ANY)
    x_rows = x_prompt.reshape(bp * seq, d)
    y_prompt, pa, pb, *weights_bf16 = pl.pallas_call(
        functools.partial(
            _prompt_entry, n_steps=n_steps,
            step=dict(tile=tile, row_chunk=PROMPT_ROW_CHUNK, tiles_per_seq=tiles_per_seq,
                      ff_chunk=FF_CHUNK)),
        grid=(n_steps,),
        in_specs=[pl.BlockSpec((tile, d), lambda g: (tile_1(g), 0)),
                  pl.BlockSpec((tile, d), lambda g: (tile_3(g), 0))]
        + [_const_spec(p.shape) for p in small] + [any_spec] * len(weights),
        out_specs=[pl.BlockSpec((tile, d), lambda g: (tile_3(g), 0)),
                   pl.BlockSpec((None, K_A - 1, gw), lambda g: (tile_1(g) // tiles_per_seq, 0, 0)),
                   pl.BlockSpec((None, K_B - 1, gw), lambda g: (tile_1(g) // tiles_per_seq, 0, 0))]
        + [any_spec] * len(weights),
        out_shape=[jax.ShapeDtypeStruct((bp * seq, d), F32),
                   jax.ShapeDtypeStruct((bp, K_A - 1, gw), F32),
                   jax.ShapeDtypeStruct((bp, K_B - 1, gw), F32)]
        + [jax.ShapeDtypeStruct(w.shape, BF16) for w in weights],
        scratch_shapes=[pltpu.VMEM((n_lane, hist_a + tile, LANES), F32),
                        pltpu.VMEM((n_lane, hist_b + tile, LANES), F32),
                        pltpu.VMEM((tile, gw), F32),
                        pltpu.VMEM((tile, 2 * gw), BF16)]
        + [pltpu.VMEM(w.shape, BF16) for w in weights]
        + [pltpu.VMEM((WEIGHT_RING, WEIGHT_CHUNK_ROWS, max(w.shape[1] for w in weights)), F32),
           pltpu.SemaphoreType.DMA((WEIGHT_RING,)),
           pltpu.SemaphoreType.DMA((len(weights),))],
        compiler_params=pltpu.CompilerParams(
            dimension_semantics=("arbitrary",),
            vmem_limit_bytes=VMEM_LIMIT_BYTES),
        name="prompt_layer",
    )(x_rows, x_rows, *small, *weights)
    y_prompt = y_prompt.reshape(bp, seq, d)

    params = list(small)
    for slot, w in zip(WEIGHT_SLOTS, weights_bf16):
        params.insert(slot, w)
    by_row = lambda a: jnp.swapaxes(a, 0, 1)
    n_seq = SAMPLE_SEQS
    rows = n_seq * dec_seq
    ra = _round_up(K_A - 1, SUBLANES) + dec_seq
    y_sample, sa, sb = pl.pallas_call(
        functools.partial(_sample_kernel, n_seq=n_seq, seq_len=dec_seq,
                          seq_group=SAMPLE_SEQ_GROUP, ff_chunk=FF_CHUNK),
        grid=(bs // n_seq,),
        in_specs=[pl.BlockSpec((rows, d), lambda i: (i, 0)),
                  pl.BlockSpec((n_seq, K_A - 1, gw), lambda i: (i, 0, 0)),
                  pl.BlockSpec((K_B - 1, n_seq, gw), lambda i: (0, i, 0))]
        + [_const_spec(p.shape) for p in params],
        out_specs=[pl.BlockSpec((rows, d), lambda i: (i, 0)),
                   pl.BlockSpec((n_seq, K_A - 1, gw), lambda i: (i, 0, 0)),
                   pl.BlockSpec((K_B - 1, n_seq, gw), lambda i: (0, i, 0))],
        out_shape=[jax.ShapeDtypeStruct((bs * dec_seq, d), F32),
                   jax.ShapeDtypeStruct((bs, K_A - 1, gw), F32),
                   jax.ShapeDtypeStruct((K_B - 1, bs, gw), F32)],
        scratch_shapes=[pltpu.VMEM((n_lane, n_seq * ra, LANES), F32),
                        pltpu.VMEM((n_lane, rows, LANES), F32),
                        pltpu.VMEM((n_lane, rows, LANES), F32)],
        compiler_params=pltpu.CompilerParams(
            dimension_semantics=("arbitrary",),
            vmem_limit_bytes=VMEM_LIMIT_BYTES),
        name="sample_layer",
    )(x_sample.reshape(bs * dec_seq, d), state_conv_a[0], by_row(state_conv_b[0]), *params)

    return (y_prompt, y_sample.reshape(bs, dec_seq, d), pa[None], pb[None],
            sa[None], by_row(sb)[None])
```

```python
import functools

import jax
import jax.numpy as jnp
from jax import lax
from jax.experimental import pallas as pl
from jax.experimental.pallas import tpu as pltpu

EPS = 1e-6
K_A = 3
K_B = 31
LANES = 128
SUBLANES = 8

F32 = jnp.float32
BF16 = jnp.bfloat16


def _rmsnorm(x, g):
    ms = jnp.mean(x * x, axis=-1, keepdims=True)
    return x * lax.rsqrt(ms + EPS) * g


def _layernorm(x, g, b):
    mu = jnp.mean(x, axis=-1, keepdims=True)
    xc = x - mu
    var = jnp.mean(xc * xc, axis=-1, keepdims=True)
    return xc * lax.rsqrt(var + EPS) * g + b


def _sigmoid(x):
    return 1.0 / (1.0 + jnp.exp(-x))


def _silu(x):
    return x * _sigmoid(x)


def _dot(a, b):
    return jnp.dot(a, b, preferred_element_type=F32)


def _conv_windows(ext_ref, j, starts, nrows, w_ref, n_taps):
    lanes = slice(j * LANES, (j + 1) * LANES)
    accs = [None] * len(starts)
    for k in range(n_taps):
        wk = w_ref[k, :, lanes]
        for i, s in enumerate(starts):
            term = wk * ext_ref[j, pl.ds(s + k, nrows), :]
            accs[i] = term if accs[i] is None else accs[i] + term
    return accs


def _after(lhs, pieces, never):
    if not pieces:
        return lhs
    rows = 2 * SUBLANES
    total = None
    for p in pieces:
        for r in range(0, p.shape[0], rows):
            for c in range(0, p.shape[1], LANES):
                slab = p[r:r + rows, c:c + LANES].astype(lhs.dtype)
                total = slab if total is None else total + slab
    head = jnp.where(never, total, lhs[0:rows, 0:LANES])
    top = jnp.concatenate([head, lhs[0:rows, LANES:]], axis=1)
    return jnp.concatenate([top, lhs[rows:, :]], axis=0)


def _mix(b_gate, conv_a, conv_b, bias_ref, ln_g_ref, ln_b_ref):
    y_b = _silu(_layernorm(conv_b + bias_ref[...], ln_g_ref[...], ln_b_ref[...]))
    return jnp.concatenate([b_gate * conv_a, y_b], axis=-1).astype(BF16)


def _ffn(x1, h2, w_gate_ref, w_up_ref, w_down_ref, ff_chunk, hide=None, never=None):
    d_ff = w_gate_ref.shape[1]
    acc = None
    pending = None
    for c, c0 in enumerate(range(0, d_ff, ff_chunk)):
        cols = slice(c0, min(c0 + ff_chunk, d_ff))
        lhs = _after(h2, (hide or {}).get(c), never)
        gate = _dot(lhs, w_gate_ref[:, cols])
        up = _dot(lhs, w_up_ref[:, cols])
        if pending is not None:
            part = _dot(pending[0], w_down_ref[pending[1], :])
            acc = part if acc is None else acc + part
        pending = ((_silu(gate) * up).astype(BF16), cols)
    part = _dot(pending[0], w_down_ref[pending[1], :])
    acc = part if acc is None else acc + part
    return x1 + acc


def _prompt_kernel(x1_ref, x3_ref, g_mix_ref, w_in_ref, wa_ref, wb_ref, bias_ref, ln_g_ref,
                   ln_b_ref, w_out_ref, g_ffn_ref, w_gate_ref, w_up_ref, w_down_ref, g_final_ref,
                   y_ref, new_a_ref, new_b_ref, exta_ref, extb_ref, bg_ref, ycat_ref,
                   *, t, tile, row_chunk, tiles_per_seq, ff_chunk, stage1=True, stage3=True):
    assert stage1 or stage3
    gw = wa_ref.shape[-1]
    n_lane = gw // LANES
    hist_a = exta_ref.shape[1] - tile
    hist_b = extb_ref.shape[1] - tile
    never = t < 0

    if stage3:
        x1 = x3_ref[...] + _dot(ycat_ref[...], w_out_ref[...])

    starts = list(range(0, tile, row_chunk))
    units = []
    for j in range(n_lane):
        a = _conv_windows(exta_ref, j, [hist_a - (K_A - 1) + s for s in starts], row_chunk,
                          wa_ref, K_A)
        b = _conv_windows(extb_ref, j, [hist_b - (K_B - 1) + s for s in starts], row_chunk,
                          wb_ref, K_B)
        units.append(a + b)
    conv_a = [jnp.concatenate(u[:len(starts)], axis=0) for u in units]
    conv_b = [jnp.concatenate(u[len(starts):], axis=0) for u in units]
    y_cat = _mix(bg_ref[...], jnp.concatenate(conv_a, axis=1), jnp.concatenate(conv_b, axis=1),
                 bias_ref, ln_g_ref, ln_b_ref)
    tail_a = exta_ref[:, tile:tile + hist_a, :]
    tail_b = extb_ref[:, tile:tile + hist_b, :]

    if stage1:
        h = _rmsnorm(x1_ref[...], g_mix_ref[...]).astype(BF16)
        in_proj = lambda k, wait=None: _dot(_after(h, wait, never),
                                            w_in_ref[:, k * gw:(k + 1) * gw])
        glu_val, glu_gate = in_proj(3), in_proj(4)
    if stage3:
        h2 = _rmsnorm(x1, g_ffn_ref[...]).astype(BF16)
        x2 = _ffn(x1, h2, w_gate_ref, w_up_ref, w_down_ref, ff_chunk,
                  hide={c: units[j] for c, j in CONV_UNIT_BEFORE_FFN_CHUNK.items()}, never=never)
        c_gate, v, b_gate = (in_proj(1), in_proj(2), in_proj(0)) if stage1 else (None,) * 3
    else:
        c_gate, v = in_proj(1, units[0]), in_proj(2, units[1])
        b_gate = in_proj(0, units[2] + units[3])
    if stage3:
        y_ref[...] = _rmsnorm(x2, g_final_ref[...])

    if stage1:
        seq_start = (t % tiles_per_seq) == 0
        cv = c_gate * v
        u_b = glu_val * _sigmoid(glu_gate)
        exta_ref[:, 0:hist_a, :] = jnp.where(seq_start, 0.0, tail_a)
        extb_ref[:, 0:hist_b, :] = jnp.where(seq_start, 0.0, tail_b)
        for j in range(n_lane):
            lanes = slice(j * LANES, (j + 1) * LANES)
            exta_ref[j, hist_a:hist_a + tile, :] = cv[:, lanes]
            extb_ref[j, hist_b:hist_b + tile, :] = u_b[:, lanes]
        bg_ref[...] = b_gate
        new_a_ref[...] = cv[tile - (K_A - 1):, :]
        new_b_ref[...] = u_b[tile - (K_B - 1):, :]

    ycat_ref[...] = y_cat


def _sample_mixer(x_ref, sa_ref, sb_ref, g_mix_ref, w_in_ref, wa_ref, wb_ref, bias_ref,
                  ln_g_ref, ln_b_ref, new_a_ref, new_b_ref, exta_ref, rows_ref, slabs_ref,
                  *, n_seq, seq_len, seq_group):
    gw = wa_ref.shape[-1]
    n_lane = gw // LANES
    ra = exta_ref.shape[1] // n_seq
    off_a = ra - seq_len - (K_A - 1)
    n_hist = K_B - 1

    h = _rmsnorm(x_ref[...], g_mix_ref[...]).astype(BF16)
    in_proj = lambda k: _dot(h, w_in_ref[:, k * gw:(k + 1) * gw])
    u_b = in_proj(3) * _sigmoid(in_proj(4))
    for j in range(n_lane):
        rows_ref[j] = u_b[:, j * LANES:(j + 1) * LANES]
        for t in range(seq_len):
            slabs_ref[j, t * n_seq:(t + 1) * n_seq, :] = (
                rows_ref[j, pl.ds(t, n_seq, stride=seq_len), :])

    def slab(j, i):
        if i < n_hist:
            return sb_ref[i, :, j * LANES:(j + 1) * LANES]
        return slabs_ref[j, (i - n_hist) * n_seq:(i - n_hist + 1) * n_seq, :]

    conv_b = []
    for j in range(n_lane):
        lanes = slice(j * LANES, (j + 1) * LANES)
        accs = [None] * seq_len
        for i in range(n_hist + seq_len):
            xi = slab(j, i)
            for t in range(seq_len):
                k = i - t
                if 0 <= k < K_B:
                    term = wb_ref[k, :, lanes] * xi
                    accs[t] = term if accs[t] is None else accs[t] + term
        for t in range(seq_len):
            rows_ref[j, pl.ds(t, n_seq, stride=seq_len), :] = accs[t]
        conv_b.append(rows_ref[j])

    cv = in_proj(1) * in_proj(2)
    b_gate = in_proj(0)
    conv_a = []
    for j in range(n_lane):
        lanes = slice(j * LANES, (j + 1) * LANES)
        for s in range(n_seq):
            rows = slice(s * seq_len, (s + 1) * seq_len)
            exta_ref[j, s * ra + off_a:s * ra + off_a + K_A - 1, :] = sa_ref[s, :, lanes]
            exta_ref[j, (s + 1) * ra - seq_len:(s + 1) * ra, :] = cv[rows, lanes]
        a = []
        for s0 in range(0, n_seq, seq_group):
            group = range(s0, s0 + seq_group)
            a += _conv_windows(exta_ref, j, [s * ra + off_a for s in group], seq_len, wa_ref, K_A)
        conv_a.append(jnp.concatenate(a, axis=0))

    for j in range(n_lane):
        lanes = slice(j * LANES, (j + 1) * LANES)
        for s in range(n_seq):
            new_a_ref[s, :, lanes] = exta_ref[j, (s + 1) * ra - (K_A - 1):(s + 1) * ra, :]
        for r in range(n_hist):
            new_b_ref[r, :, lanes] = slab(j, seq_len + r)
    return _mix(b_gate, jnp.concatenate(conv_a, axis=1), jnp.concatenate(conv_b, axis=1),
                bias_ref, ln_g_ref, ln_b_ref)


def _sample_kernel(x_ref, sa_ref, sb_ref, g_mix_ref, w_in_ref, wa_ref, wb_ref, bias_ref,
                   ln_g_ref, ln_b_ref, w_out_ref, g_ffn_ref, w_gate_ref, w_up_ref, w_down_ref,
                   g_final_ref, y_ref, new_a_ref, new_b_ref, exta_ref, rows_ref, slabs_ref,
                   *, ff_chunk, **mixer):
    y_cat = _sample_mixer(x_ref, sa_ref, sb_ref, g_mix_ref, w_in_ref, wa_ref, wb_ref, bias_ref,
                          ln_g_ref, ln_b_ref, new_a_ref, new_b_ref, exta_ref, rows_ref,
                          slabs_ref, **mixer)
    x1 = x_ref[...] + _dot(y_cat, w_out_ref[...])
    h2 = _rmsnorm(x1, g_ffn_ref[...]).astype(BF16)
    x2 = _ffn(x1, h2, w_gate_ref, w_up_ref, w_down_ref, ff_chunk)
    y_ref[...] = _rmsnorm(x2, g_final_ref[...])


N_PARAMS = 13
WEIGHT_SLOTS = (1, 7, 9, 10, 11)
CONVERT_ORDER = (0, 2, 3, 4, 1)
WEIGHT_CHUNK_ROWS = 256
WEIGHT_RING = 4


def _weight_pieces(w_hbm_refs, w_vmem_refs, chunk):
    return [(w_hbm_refs[k], w_vmem_refs[k], r0) for k in CONVERT_ORDER
            for r0 in range(0, w_hbm_refs[k].shape[0], chunk)]


def _convert_weights(pieces, first, last, stage_ref, sem_ref):
    ring, chunk = stage_ref.shape[0], stage_ref.shape[1]

    def fetch(p):
        src, _, r0 = pieces[p]
        return pltpu.make_async_copy(src.at[r0:r0 + chunk, :],
                                     stage_ref.at[p % ring, :, 0:src.shape[1]],
                                     sem_ref.at[p % ring])

    if first == 0:
        for p in range(min(ring - 1, len(pieces))):
            fetch(p).start()
    for p in range(first, last):
        src, dst, r0 = pieces[p]
        if p + ring - 1 < len(pieces):
            fetch(p + ring - 1).start()
        fetch(p).wait()
        dst[r0:r0 + chunk, :] = stage_ref[p % ring, :, 0:src.shape[1]].astype(BF16)


def _prompt_entry(*refs, n_steps, step):
    n_w = len(WEIGHT_SLOTS)
    x1_ref, x3_ref = refs[:2]
    params = list(refs[2:2 + N_PARAMS - n_w])
    w_hbm = refs[2 + N_PARAMS - n_w:2 + N_PARAMS]
    outs = refs[2 + N_PARAMS:5 + N_PARAMS]
    w_out_hbm = refs[5 + N_PARAMS:5 + N_PARAMS + n_w]
    scratch = refs[5 + N_PARAMS + n_w:]
    exta_ref, extb_ref, bg_ref, ycat_ref = scratch[:4]
    w_vmem = scratch[4:4 + n_w]
    stage_ref, sem_ref, out_sem_ref = scratch[4 + n_w:]
    g = pl.program_id(0)

    def copy_out(k):
        return pltpu.make_async_copy(w_vmem[k], w_out_hbm[k], out_sem_ref.at[k])

    pieces = _weight_pieces(w_hbm, w_vmem, stage_ref.shape[1])
    n_first = w_hbm[CONVERT_ORDER[0]].shape[0] // stage_ref.shape[1]
    n_second = n_first + stage_ref.shape[0] - 1
    convert = lambda first, last: _convert_weights(pieces, first, last, stage_ref, sem_ref)

    for slot, w in zip(WEIGHT_SLOTS, w_vmem):
        params.insert(slot, w)
    y_ref, new_a_ref, new_b_ref = outs
    tile = step["tile"]

    def run(sub, **stages):
        rows = pl.ds(sub * tile, tile)
        _prompt_kernel(x1_ref.at[rows], x3_ref.at[rows], *params, y_ref.at[rows], new_a_ref,
                       new_b_ref, exta_ref, extb_ref, bg_ref, ycat_ref, t=SUBSTEPS * g + sub,
                       **step, **stages)

    @pl.when(g == 0)
    def _():
        exta_ref[...] = jnp.zeros(exta_ref.shape, F32)
        extb_ref[...] = jnp.zeros(extb_ref.shape, F32)
        bg_ref[...] = jnp.zeros(bg_ref.shape, F32)
        convert(0, n_first)
        run(0, stage3=False)
        convert(n_first, n_second)
        run(1, stage3=False)

    @pl.when(g == 1)
    def _():
        convert(n_second, len(pieces))
        for k in range(n_w):
            copy_out(k).start()

    @pl.when(jnp.logical_and(g > 0, g < n_steps - 1))
    def _():
        run(0)
        run(1)

    @pl.when(g == n_steps - 1)
    def _():
        run(0, stage1=False)
        run(1, stage1=False)
        for k in range(n_w):
            copy_out(k).wait()


def _round_up(n, m):
    return -(-n // m) * m


def _const_spec(shape):
    zeros = (0,) * len(shape)
    return pl.BlockSpec(shape, lambda g: zeros, pipeline_mode=pl.Buffered(1))


PROMPT_TILE = 256
SUBSTEPS = 2
PROMPT_ROW_CHUNK = 64
CONV_UNIT_BEFORE_FFN_CHUNK = {1: 0, 2: 1, 3: 2, 4: 3}
FF_CHUNK = 512
SAMPLE_SEQS = 32
SAMPLE_SEQ_GROUP = 16
VMEM_LIMIT_BYTES = 56 * 1024 * 1024


def kernel(x_prompt, x_sample, state_conv_a, state_conv_b, g_mix, w_in, conv_a_w, conv_b_w,
           conv_b_bias, ln_b_g, ln_b_b, w_out, g_ffn, w_gate, w_up, w_down, g_final):
    depth = g_mix.shape[0]
    assert depth == 1, "single-layer trunk"
    bp, seq, d = x_prompt.shape
    bs, dec_seq, _ = x_sample.shape
    gw = conv_a_w.shape[-1]
    assert conv_a_w.shape[1] == K_A and conv_b_w.shape[1] == K_B
    assert dec_seq == SUBLANES and gw % LANES == 0
    assert seq % PROMPT_TILE == 0 and bs % SAMPLE_SEQS == 0

    row = lambda a: a.reshape(1, -1)
    taps = lambda w: jnp.swapaxes(w, 0, 1)
    small = (row(g_mix[0]), taps(conv_a_w), taps(conv_b_w), row(conv_b_bias[0]), row(ln_b_g[0]),
             row(ln_b_b[0]), row(g_ffn[0]), row(g_final))
    weights = (w_in[0], w_out[0], w_gate[0], w_up[0], w_down[0])
    assert len(small) + len(weights) == N_PARAMS
    assert all(w.shape[0] % WEIGHT_CHUNK_ROWS == 0 for w in weights)

    tile = PROMPT_TILE
    tiles_per_seq = seq // tile
    assert tiles_per_seq % SUBSTEPS == 0
    n_blocks = bp * tiles_per_seq // SUBSTEPS
    n_steps = n_blocks + 1
    block = SUBSTEPS * tile
    hist_a = _round_up(K_A - 1, SUBLANES)
    hist_b = _round_up(K_B - 1, SUBLANES)
    n_lane = gw // LANES
    tile_1 = lambda g: jnp.minimum(g, n_blocks - 1)
    tile_3 = lambda g: jnp.maximum(g - 1, 0)
    tiles_per_seq_blocks = tiles_per_seq // SUBSTEPS
    any_spec = pl.BlockSpec(memory_space=pl.ANY)
    x_rows = x_prompt.reshape(bp * seq, d)
    y_prompt, pa, pb, *weights_bf16 = pl.pallas_call(
        functools.partial(
            _prompt_entry, n_steps=n_steps,
            step=dict(tile=tile, row_chunk=PROMPT_ROW_CHUNK, tiles_per_seq=tiles_per_seq,
                      ff_chunk=FF_CHUNK)),
        grid=(n_steps,),
        in_specs=[pl.BlockSpec((block, d), lambda g: (tile_1(g), 0)),
                  pl.BlockSpec((block, d), lambda g: (tile_3(g), 0))]
        + [_const_spec(p.shape) for p in small] + [any_spec] * len(weights),
        out_specs=[pl.BlockSpec((block, d), lambda g: (tile_3(g), 0)),
                   pl.BlockSpec((None, K_A - 1, gw),
                                lambda g: (tile_1(g) // tiles_per_seq_blocks, 0, 0)),
                   pl.BlockSpec((None, K_B - 1, gw),
                                lambda g: (tile_1(g) // tiles_per_seq_blocks, 0, 0))]
        + [any_spec] * len(weights),
        out_shape=[jax.ShapeDtypeStruct((bp * seq, d), F32),
                   jax.ShapeDtypeStruct((bp, K_A - 1, gw), F32),
                   jax.ShapeDtypeStruct((bp, K_B - 1, gw), F32)]
        + [jax.ShapeDtypeStruct(w.shape, BF16) for w in weights],
        scratch_shapes=[pltpu.VMEM((n_lane, hist_a + tile, LANES), F32),
                        pltpu.VMEM((n_lane, hist_b + tile, LANES), F32),
                        pltpu.VMEM((tile, gw), F32),
                        pltpu.VMEM((tile, 2 * gw), BF16)]
        + [pltpu.VMEM(w.shape, BF16) for w in weights]
        + [pltpu.VMEM((WEIGHT_RING, WEIGHT_CHUNK_ROWS, max(w.shape[1] for w in weights)), F32),
           pltpu.SemaphoreType.DMA((WEIGHT_RING,)),
           pltpu.SemaphoreType.DMA((len(weights),))],
        compiler_params=pltpu.CompilerParams(
            dimension_semantics=("arbitrary",),
            vmem_limit_bytes=VMEM_LIMIT_BYTES),
        name="prompt_layer",
    )(x_rows, x_rows, *small, *weights)
    y_prompt = y_prompt.reshape(bp, seq, d)

    params = list(small)
    for slot, w in zip(WEIGHT_SLOTS, weights_bf16):
        params.insert(slot, w)
    by_row = lambda a: jnp.swapaxes(a, 0, 1)
    n_seq = SAMPLE_SEQS
    rows = n_seq * dec_seq
    ra = _round_up(K_A - 1, SUBLANES) + dec_seq
    y_sample, sa, sb = pl.pallas_call(
        functools.partial(_sample_kernel, n_seq=n_seq, seq_len=dec_seq,
                          seq_group=SAMPLE_SEQ_GROUP, ff_chunk=FF_CHUNK),
        grid=(bs // n_seq,),
        in_specs=[pl.BlockSpec((rows, d), lambda i: (i, 0)),
                  pl.BlockSpec((n_seq, K_A - 1, gw), lambda i: (i, 0, 0)),
                  pl.BlockSpec((K_B - 1, n_seq, gw), lambda i: (0, i, 0))]
        + [_const_spec(p.shape) for p in params],
        out_specs=[pl.BlockSpec((rows, d), lambda i: (i, 0)),
                   pl.BlockSpec((n_seq, K_A - 1, gw), lambda i: (i, 0, 0)),
                   pl.BlockSpec((K_B - 1, n_seq, gw), lambda i: (0, i, 0))],
        out_shape=[jax.ShapeDtypeStruct((bs * dec_seq, d), F32),
                   jax.ShapeDtypeStruct((bs, K_A - 1, gw), F32),
                   jax.ShapeDtypeStruct((K_B - 1, bs, gw), F32)],
        scratch_shapes=[pltpu.VMEM((n_lane, n_seq * ra, LANES), F32),
                        pltpu.VMEM((n_lane, rows, LANES), F32),
                        pltpu.VMEM((n_lane, rows, LANES), F32)],
        compiler_params=pltpu.CompilerParams(
            dimension_semantics=("arbitrary",),
            vmem_limit_bytes=VMEM_LIMIT_BYTES),
        name="sample_layer",
    )(x_sample.reshape(bs * dec_seq, d), state_conv_a[0], by_row(state_conv_b[0]), *params)

    return (y_prompt, y_sample.reshape(bs, dec_seq, d), pa[None], pb[None],
            sa[None], by_row(sb)[None])
```

```python
import functools

import jax
import jax.numpy as jnp
from jax import lax
from jax.experimental import pallas as pl
from jax.experimental.pallas import tpu as pltpu

EPS = 1e-6
K_A = 3
K_B = 31
LANES = 128
SUBLANES = 8

F32 = jnp.float32
BF16 = jnp.bfloat16


def _rmsnorm(x, g):
    ms = jnp.mean(x * x, axis=-1, keepdims=True)
    return x * lax.rsqrt(ms + EPS) * g


def _layernorm(x, g, b):
    mu = jnp.mean(x, axis=-1, keepdims=True)
    xc = x - mu
    var = jnp.mean(xc * xc, axis=-1, keepdims=True)
    return xc * lax.rsqrt(var + EPS) * g + b


def _sigmoid(x):
    return 1.0 / (1.0 + jnp.exp(-x))


def _silu(x):
    return x * _sigmoid(x)


def _dot(a, b):
    return jnp.dot(a, b, preferred_element_type=F32)


def _conv_windows(ext_ref, j, starts, nrows, w_ref, n_taps):
    lanes = slice(j * LANES, (j + 1) * LANES)
    accs = [None] * len(starts)
    for k in range(n_taps):
        wk = w_ref[k, :, lanes]
        for i, s in enumerate(starts):
            term = wk * ext_ref[j, pl.ds(s + k, nrows), :]
            accs[i] = term if accs[i] is None else accs[i] + term
    return accs


def _after(lhs, pieces, never):
    if not pieces:
        return lhs
    rows = 2 * SUBLANES
    total = None
    for p in pieces:
        for r in range(0, p.shape[0], rows):
            for c in range(0, p.shape[1], LANES):
                slab = p[r:r + rows, c:c + LANES].astype(lhs.dtype)
                total = slab if total is None else total + slab
    head = jnp.where(never, total, lhs[0:rows, 0:LANES])
    top = jnp.concatenate([head, lhs[0:rows, LANES:]], axis=1)
    return jnp.concatenate([top, lhs[rows:, :]], axis=0)


def _mix(b_gate, conv_a, conv_b, bias_ref, ln_g_ref, ln_b_ref):
    y_b = _silu(_layernorm(conv_b + bias_ref[...], ln_g_ref[...], ln_b_ref[...]))
    return jnp.concatenate([b_gate * conv_a, y_b], axis=-1).astype(BF16)


def _ffn(x1, h2, w_gate_ref, w_up_ref, w_down_ref, ff_chunk, hide=None, never=None):
    d_ff = w_gate_ref.shape[1]
    acc = None
    pending = None
    for c, c0 in enumerate(range(0, d_ff, ff_chunk)):
        cols = slice(c0, min(c0 + ff_chunk, d_ff))
        lhs = _after(h2, (hide or {}).get(c), never)
        gate = _dot(lhs, w_gate_ref[:, cols])
        up = _dot(lhs, w_up_ref[:, cols])
        if pending is not None:
            part = _dot(pending[0], w_down_ref[pending[1], :])
            acc = part if acc is None else acc + part
        pending = ((_silu(gate) * up).astype(BF16), cols)
    part = _dot(pending[0], w_down_ref[pending[1], :])
    acc = part if acc is None else acc + part
    return x1 + acc


def _prompt_kernel(x1_ref, x3_ref, g_mix_ref, w_in_ref, wa_ref, wb_ref, bias_ref, ln_g_ref,
                   ln_b_ref, w_out_ref, g_ffn_ref, w_gate_ref, w_up_ref, w_down_ref, g_final_ref,
                   y_ref, new_a_ref, new_b_ref, exta_ref, extb_ref, bg_ref, ycat_ref,
                   *, t, tile, row_chunk, tiles_per_seq, ff_chunk, stage1=True, stage3=True):
    assert stage1 or stage3
    gw = wa_ref.shape[-1]
    n_lane = gw // LANES
    hist_a = exta_ref.shape[1] - tile
    hist_b = extb_ref.shape[1] - tile
    never = t < 0

    if stage3:
        x1 = x3_ref[...] + _dot(ycat_ref[...], w_out_ref[...])

    starts = list(range(0, tile, row_chunk))
    units = []
    for j in range(n_lane):
        a = _conv_windows(exta_ref, j, [hist_a - (K_A - 1) + s for s in starts], row_chunk,
                          wa_ref, K_A)
        b = _conv_windows(extb_ref, j, [hist_b - (K_B - 1) + s for s in starts], row_chunk,
                          wb_ref, K_B)
        units.append(a + b)
    conv_a = [jnp.concatenate(u[:len(starts)], axis=0) for u in units]
    conv_b = [jnp.concatenate(u[len(starts):], axis=0) for u in units]
    y_cat = _mix(bg_ref[...], jnp.concatenate(conv_a, axis=1), jnp.concatenate(conv_b, axis=1),
                 bias_ref, ln_g_ref, ln_b_ref)
    tail_a = exta_ref[:, tile:tile + hist_a, :]
    tail_b = extb_ref[:, tile:tile + hist_b, :]

    if stage1:
        h = _rmsnorm(x1_ref[...], g_mix_ref[...]).astype(BF16)
        in_proj = lambda k, wait=None: _dot(_after(h, wait, never),
                                            w_in_ref[:, k * gw:(k + 1) * gw])
        glu_val, glu_gate = in_proj(3), in_proj(4)
    if stage3:
        h2 = _rmsnorm(x1, g_ffn_ref[...]).astype(BF16)
        x2 = _ffn(x1, h2, w_gate_ref, w_up_ref, w_down_ref, ff_chunk,
                  hide={c: units[j] for c, j in CONV_UNIT_BEFORE_FFN_CHUNK.items()}, never=never)
        c_gate, v, b_gate = (in_proj(1), in_proj(2), in_proj(0)) if stage1 else (None,) * 3
    else:
        c_gate, v = in_proj(1, units[0]), in_proj(2, units[1])
        b_gate = in_proj(0, units[2] + units[3])
    if stage3:
        y_ref[...] = _rmsnorm(x2, g_final_ref[...])

    if stage1:
        seq_start = (t % tiles_per_seq) == 0
        cv = c_gate * v
        u_b = glu_val * _sigmoid(glu_gate)
        exta_ref[:, 0:hist_a, :] = jnp.where(seq_start, 0.0, tail_a)
        extb_ref[:, 0:hist_b, :] = jnp.where(seq_start, 0.0, tail_b)
        for j in range(n_lane):
            lanes = slice(j * LANES, (j + 1) * LANES)
            exta_ref[j, hist_a:hist_a + tile, :] = cv[:, lanes]
            extb_ref[j, hist_b:hist_b + tile, :] = u_b[:, lanes]
        bg_ref[...] = b_gate
        new_a_ref[...] = cv[tile - (K_A - 1):, :]
        new_b_ref[...] = u_b[tile - (K_B - 1):, :]

    ycat_ref[...] = y_cat


def _sample_mixer(x_ref, sa_ref, sb_ref, g_mix_ref, w_in_ref, wa_ref, wb_ref, bias_ref,
                  ln_g_ref, ln_b_ref, new_a_ref, new_b_ref, exta_ref, rows_ref, slabs_ref,
                  *, n_seq, seq_len, seq_group):
    gw = wa_ref.shape[-1]
    n_lane = gw // LANES
    ra = exta_ref.shape[1] // n_seq
    off_a = ra - seq_len - (K_A - 1)
    n_hist = K_B - 1

    h = _rmsnorm(x_ref[...], g_mix_ref[...]).astype(BF16)
    in_proj = lambda k: _dot(h, w_in_ref[:, k * gw:(k + 1) * gw])
    u_b = in_proj(3) * _sigmoid(in_proj(4))
    for j in range(n_lane):
        rows_ref[j] = u_b[:, j * LANES:(j + 1) * LANES]
        for t in range(seq_len):
            slabs_ref[j, t * n_seq:(t + 1) * n_seq, :] = (
                rows_ref[j, pl.ds(t, n_seq, stride=seq_len), :])

    def slab(j, i):
        if i < n_hist:
            return sb_ref[i, :, j * LANES:(j + 1) * LANES]
        return slabs_ref[j, (i - n_hist) * n_seq:(i - n_hist + 1) * n_seq, :]

    conv_b = []
    for j in range(n_lane):
        lanes = slice(j * LANES, (j + 1) * LANES)
        accs = [None] * seq_len
        for i in range(n_hist + seq_len):
            xi = slab(j, i)
            for t in range(seq_len):
                k = i - t
                if 0 <= k < K_B:
                    term = wb_ref[k, :, lanes] * xi
                    accs[t] = term if accs[t] is None else accs[t] + term
        for t in range(seq_len):
            rows_ref[j, pl.ds(t, n_seq, stride=seq_len), :] = accs[t]
        conv_b.append(rows_ref[j])

    cv = in_proj(1) * in_proj(2)
    b_gate = in_proj(0)
    conv_a = []
    for j in range(n_lane):
        lanes = slice(j * LANES, (j + 1) * LANES)
        for s in range(n_seq):
            rows = slice(s * seq_len, (s + 1) * seq_len)
            exta_ref[j, s * ra + off_a:s * ra + off_a + K_A - 1, :] = sa_ref[s, :, lanes]
            exta_ref[j, (s + 1) * ra - seq_len:(s + 1) * ra, :] = cv[rows, lanes]
        a = []
        for s0 in range(0, n_seq, seq_group):
            group = range(s0, s0 + seq_group)
            a += _conv_windows(exta_ref, j, [s * ra + off_a for s in group], seq_len, wa_ref, K_A)
        conv_a.append(jnp.concatenate(a, axis=0))

    for j in range(n_lane):
        lanes = slice(j * LANES, (j + 1) * LANES)
        for s in range(n_seq):
            new_a_ref[s, :, lanes] = exta_ref[j, (s + 1) * ra - (K_A - 1):(s + 1) * ra, :]
        for r in range(n_hist):
            new_b_ref[r, :, lanes] = slab(j, seq_len + r)
    return _mix(b_gate, jnp.concatenate(conv_a, axis=1), jnp.concatenate(conv_b, axis=1),
                bias_ref, ln_g_ref, ln_b_ref)


def _sample_kernel(x_ref, sa_ref, sb_ref, g_mix_ref, w_in_ref, wa_ref, wb_ref, bias_ref,
                   ln_g_ref, ln_b_ref, w_out_ref, g_ffn_ref, w_gate_ref, w_up_ref, w_down_ref,
                   g_final_ref, y_ref, new_a_ref, new_b_ref, exta_ref, rows_ref, slabs_ref,
                   *, ff_chunk, **mixer):
    y_cat = _sample_mixer(x_ref, sa_ref, sb_ref, g_mix_ref, w_in_ref, wa_ref, wb_ref, bias_ref,
                          ln_g_ref, ln_b_ref, new_a_ref, new_b_ref, exta_ref, rows_ref,
                          slabs_ref, **mixer)
    x1 = x_ref[...] + _dot(y_cat, w_out_ref[...])
    h2 = _rmsnorm(x1, g_ffn_ref[...]).astype(BF16)
    x2 = _ffn(x1, h2, w_gate_ref, w_up_ref, w_down_ref, ff_chunk)
    y_ref[...] = _rmsnorm(x2, g_final_ref[...])


N_PARAMS = 13
WEIGHT_SLOTS = (1, 7, 9, 10, 11)
CONVERT_ORDER = (0, 2, 3, 4, 1)
WEIGHT_CHUNK_ROWS = 256
WEIGHT_RING = 4


def _weight_pieces(w_hbm_refs, w_vmem_refs, chunk):
    return [(w_hbm_refs[k], w_vmem_refs[k], r0) for k in CONVERT_ORDER
            for r0 in range(0, w_hbm_refs[k].shape[0], chunk)]


def _convert_weights(pieces, first, last, stage_ref, sem_ref):
    ring, chunk = stage_ref.shape[0], stage_ref.shape[1]

    def fetch(p):
        src, _, r0 = pieces[p]
        return pltpu.make_async_copy(src.at[r0:r0 + chunk, :],
                                     stage_ref.at[p % ring, :, 0:src.shape[1]],
                                     sem_ref.at[p % ring])

    if first == 0:
        for p in range(min(ring - 1, len(pieces))):
            fetch(p).start()
    for p in range(first, last):
        src, dst, r0 = pieces[p]
        if p + ring - 1 < len(pieces):
            fetch(p + ring - 1).start()
        fetch(p).wait()
        dst[r0:r0 + chunk, :] = stage_ref[p % ring, :, 0:src.shape[1]].astype(BF16)


def _prompt_entry(*refs, n_steps, step):
    n_w = len(WEIGHT_SLOTS)
    x1_ref, x3_ref = refs[:2]
    params = list(refs[2:2 + N_PARAMS - n_w])
    w_hbm = refs[2 + N_PARAMS - n_w:2 + N_PARAMS]
    outs = refs[2 + N_PARAMS:5 + N_PARAMS]
    w_out_hbm = refs[5 + N_PARAMS:5 + N_PARAMS + n_w]
    scratch = refs[5 + N_PARAMS + n_w:]
    exta_ref, extb_ref, bg_ref, ycat_ref = scratch[:4]
    w_vmem = scratch[4:4 + n_w]
    stage_ref, sem_ref, out_sem_ref = scratch[4 + n_w:]
    g = pl.program_id(0)

    def copy_out(k):
        return pltpu.make_async_copy(w_vmem[k], w_out_hbm[k], out_sem_ref.at[k])

    pieces = _weight_pieces(w_hbm, w_vmem, stage_ref.shape[1])
    n_first = w_hbm[CONVERT_ORDER[0]].shape[0] // stage_ref.shape[1]
    n_second = n_first + stage_ref.shape[0] - 1
    convert = lambda first, last: _convert_weights(pieces, first, last, stage_ref, sem_ref)

    for slot, w in zip(WEIGHT_SLOTS, w_vmem):
        params.insert(slot, w)
    y_ref, new_a_ref, new_b_ref = outs
    tile = step["tile"]

    def run(sub, **stages):
        rows = pl.ds(sub * tile, tile)
        _prompt_kernel(x1_ref.at[rows], x3_ref.at[rows], *params, y_ref.at[rows], new_a_ref,
                       new_b_ref, exta_ref, extb_ref, bg_ref, ycat_ref, t=SUBSTEPS * g + sub,
                       **step, **stages)

    @pl.when(g == 0)
    def _():
        exta_ref[...] = jnp.zeros(exta_ref.shape, F32)
        extb_ref[...] = jnp.zeros(extb_ref.shape, F32)
        bg_ref[...] = jnp.zeros(bg_ref.shape, F32)
        convert(0, n_first)
        run(0, stage3=False)
        convert(n_first, n_second)
        run(1, stage3=False)

    @pl.when(g == 1)
    def _():
        convert(n_second, len(pieces))
        for k in range(n_w):
            copy_out(k).start()

    @pl.when(jnp.logical_and(g > 0, g < n_steps - 1))
    def _():
        run(0)
        run(1)

    @pl.when(g == n_steps - 1)
    def _():
        run(0, stage1=False)
        run(1, stage1=False)
        for k in range(n_w):
            copy_out(k).wait()


def _round_up(n, m):
    return -(-n // m) * m


def _const_spec(shape):
    zeros = (0,) * len(shape)
    return pl.BlockSpec(shape, lambda g: zeros, pipeline_mode=pl.Buffered(1))


PROMPT_TILE = 256
SUBSTEPS = 2
PROMPT_ROW_CHUNK = 64
CONV_UNIT_BEFORE_FFN_CHUNK = {1: 0, 2: 1, 3: 2, 4: 3}
FF_CHUNK = 512
SAMPLE_SEQS = 32
SAMPLE_SEQ_GROUP = 16
VMEM_LIMIT_BYTES = 58 * 1024 * 1024


def kernel(x_prompt, x_sample, state_conv_a, state_conv_b, g_mix, w_in, conv_a_w, conv_b_w,
           conv_b_bias, ln_b_g, ln_b_b, w_out, g_ffn, w_gate, w_up, w_down, g_final):
    depth = g_mix.shape[0]
    assert depth == 1, "single-layer trunk"
    bp, seq, d = x_prompt.shape
    bs, dec_seq, _ = x_sample.shape
    gw = conv_a_w.shape[-1]
    assert conv_a_w.shape[1] == K_A and conv_b_w.shape[1] == K_B
    assert dec_seq == SUBLANES and gw % LANES == 0
    assert seq % PROMPT_TILE == 0 and bs % SAMPLE_SEQS == 0

    row = lambda a: a.reshape(1, -1)
    taps = lambda w: jnp.swapaxes(w, 0, 1)
    small = (row(g_mix[0]), taps(conv_a_w), taps(conv_b_w), row(conv_b_bias[0]), row(ln_b_g[0]),
             row(ln_b_b[0]), row(g_ffn[0]), row(g_final))
    weights = (w_in[0], w_out[0], w_gate[0], w_up[0], w_down[0])
    assert len(small) + len(weights) == N_PARAMS
    assert all(w.shape[0] % WEIGHT_CHUNK_ROWS == 0 for w in weights)

    tile = PROMPT_TILE
    tiles_per_seq = seq // tile
    assert tiles_per_seq % SUBSTEPS == 0
    n_blocks = bp * tiles_per_seq // SUBSTEPS
    n_steps = n_blocks + 1
    block = SUBSTEPS * tile
    hist_a = _round_up(K_A - 1, SUBLANES)
    hist_b = _round_up(K_B - 1, SUBLANES)
    n_lane = gw // LANES
    tile_1 = lambda g: jnp.minimum(g, n_blocks - 1)
    tile_3 = lambda g: jnp.maximum(g - 1, 0)
    tiles_per_seq_blocks = tiles_per_seq // SUBSTEPS
    any_spec = pl.BlockSpec(memory_space=pl.ANY)
    x_rows = x_prompt.reshape(bp * seq, d)
    y_prompt, pa, pb, *weights_bf16 = pl.pallas_call(
        functools.partial(
            _prompt_entry, n_steps=n_steps,
            step=dict(tile=tile, row_chunk=PROMPT_ROW_CHUNK, tiles_per_seq=tiles_per_seq,
                      ff_chunk=FF_CHUNK)),
        grid=(n_steps,),
        in_specs=[pl.BlockSpec((block, d), lambda g: (tile_1(g), 0)),
                  pl.BlockSpec((block, d), lambda g: (tile_3(g), 0))]
        + [_const_spec(p.shape) for p in small] + [any_spec] * len(weights),
        out_specs=[pl.BlockSpec((block, d), lambda g: (tile_3(g), 0)),
                   pl.BlockSpec((None, K_A - 1, gw),
                                lambda g: (tile_1(g) // tiles_per_seq_blocks, 0, 0)),
                   pl.BlockSpec((None, K_B - 1, gw),
                                lambda g: (tile_1(g) // tiles_per_seq_blocks, 0, 0))]
        + [any_spec] * len(weights),
        out_shape=[jax.ShapeDtypeStruct((bp * seq, d), F32),
                   jax.ShapeDtypeStruct((bp, K_A - 1, gw), F32),
                   jax.ShapeDtypeStruct((bp, K_B - 1, gw), F32)]
        + [jax.ShapeDtypeStruct(w.shape, BF16) for w in weights],
        scratch_shapes=[pltpu.VMEM((n_lane, hist_a + tile, LANES), F32),
                        pltpu.VMEM((n_lane, hist_b + tile, LANES), F32),
                        pltpu.VMEM((tile, gw), F32),
                        pltpu.VMEM((tile, 2 * gw), BF16)]
        + [pltpu.VMEM(w.shape, BF16) for w in weights]
        + [pltpu.VMEM((WEIGHT_RING, WEIGHT_CHUNK_ROWS, max(w.shape[1] for w in weights)), F32),
           pltpu.SemaphoreType.DMA((WEIGHT_RING,)),
           pltpu.SemaphoreType.DMA((len(weights),))],
        compiler_params=pltpu.CompilerParams(
            dimension_semantics=("arbitrary",),
            vmem_limit_bytes=VMEM_LIMIT_BYTES),
        name="prompt_layer",
    )(x_rows, x_rows, *small, *weights)
    y_prompt = y_prompt.reshape(bp, seq, d)

    params = list(small)
    for slot, w in zip(WEIGHT_SLOTS, weights_bf16):
        params.insert(slot, w)
    by_row = lambda a: jnp.swapaxes(a, 0, 1)
    n_seq = SAMPLE_SEQS
    rows = n_seq * dec_seq
    ra = _round_up(K_A - 1, SUBLANES) + dec_seq
    y_sample, sa, sb = pl.pallas_call(
        functools.partial(_sample_kernel, n_seq=n_seq, seq_len=dec_seq,
                          seq_group=SAMPLE_SEQ_GROUP, ff_chunk=FF_CHUNK),
        grid=(bs // n_seq,),
        in_specs=[pl.BlockSpec((rows, d), lambda i: (i, 0)),
                  pl.BlockSpec((n_seq, K_A - 1, gw), lambda i: (i, 0, 0)),
                  pl.BlockSpec((K_B - 1, n_seq, gw), lambda i: (0, i, 0))]
        + [_const_spec(p.shape) for p in params],
        out_specs=[pl.BlockSpec((rows, d), lambda i: (i, 0)),
                   pl.BlockSpec((n_seq, K_A - 1, gw), lambda i: (i, 0, 0)),
                   pl.BlockSpec((K_B - 1, n_seq, gw), lambda i: (0, i, 0))],
        out_shape=[jax.ShapeDtypeStruct((bs * dec_seq, d), F32),
                   jax.ShapeDtypeStruct((bs, K_A - 1, gw), F32),
                   jax.ShapeDtypeStruct((K_B - 1, bs, gw), F32)],
        scratch_shapes=[pltpu.VMEM((n_lane, n_seq * ra, LANES), F32),
                        pltpu.VMEM((n_lane, rows, LANES), F32),
                        pltpu.VMEM((n_lane, rows, LANES), F32)],
        compiler_params=pltpu.CompilerParams(
            dimension_semantics=("arbitrary",),
            vmem_limit_bytes=VMEM_LIMIT_BYTES),
        name="sample_layer",
    )(x_sample.reshape(bs * dec_seq, d), state_conv_a[0], by_row(state_conv_b[0]), *params)

    return (y_prompt, y_sample.reshape(bs, dec_seq, d), pa[None], pb[None],
            sa[None], by_row(sb)[None])
```

```python
import functools

import jax
import jax.numpy as jnp
from jax import lax
from jax.experimental import pallas as pl
from jax.experimental.pallas import tpu as pltpu

EPS = 1e-6
K_A = 3
K_B = 31
LANES = 128
SUBLANES = 8

F32 = jnp.float32
BF16 = jnp.bfloat16


def _rmsnorm(x, g):
    ms = jnp.mean(x * x, axis=-1, keepdims=True)
    return x * lax.rsqrt(ms + EPS) * g


def _layernorm(x, g, b):
    mu = jnp.mean(x, axis=-1, keepdims=True)
    xc = x - mu
    var = jnp.mean(xc * xc, axis=-1, keepdims=True)
    return xc * lax.rsqrt(var + EPS) * g + b


def _sigmoid(x):
    return 1.0 / (1.0 + jnp.exp(-x))


def _silu(x):
    return x * _sigmoid(x)


def _dot(a, b):
    return jnp.dot(a, b, preferred_element_type=F32)


def _conv_windows(ext_ref, j, starts, nrows, w_ref, n_taps):
    lanes = slice(j * LANES, (j + 1) * LANES)
    accs = [None] * len(starts)
    for k in range(n_taps):
        wk = w_ref[k, :, lanes]
        for i, s in enumerate(starts):
            term = wk * ext_ref[j, pl.ds(s + k, nrows), :]
            accs[i] = term if accs[i] is None else accs[i] + term
    return accs


def _after(lhs, pieces, never):
    if not pieces:
        return lhs
    rows = 2 * SUBLANES
    total = None
    for p in pieces:
        for r in range(0, p.shape[0], rows):
            for c in range(0, p.shape[1], LANES):
                slab = p[r:r + rows, c:c + LANES].astype(lhs.dtype)
                total = slab if total is None else total + slab
    head = jnp.where(never, total, lhs[0:rows, 0:LANES])
    top = jnp.concatenate([head, lhs[0:rows, LANES:]], axis=1)
    return jnp.concatenate([top, lhs[rows:, :]], axis=0)


def _mix(b_gate, conv_a, conv_b, bias_ref, ln_g_ref, ln_b_ref):
    y_b = _silu(_layernorm(conv_b + bias_ref[...], ln_g_ref[...], ln_b_ref[...]))
    return jnp.concatenate([b_gate * conv_a, y_b], axis=-1).astype(BF16)


def _ffn(x1, h2, w_gate_ref, w_up_ref, w_down_ref, ff_chunk, hide=None, never=None):
    d_ff = w_gate_ref.shape[1]
    acc = None
    pending = None
    for c, c0 in enumerate(range(0, d_ff, ff_chunk)):
        cols = slice(c0, min(c0 + ff_chunk, d_ff))
        lhs = _after(h2, (hide or {}).get(c), never)
        gate = _dot(lhs, w_gate_ref[:, cols])
        up = _dot(lhs, w_up_ref[:, cols])
        if pending is not None:
            part = _dot(pending[0], w_down_ref[pending[1], :])
            acc = part if acc is None else acc + part
        pending = ((_silu(gate) * up).astype(BF16), cols)
    part = _dot(pending[0], w_down_ref[pending[1], :])
    acc = part if acc is None else acc + part
    return x1 + acc


def _prompt_kernel(x1_ref, x3_ref, g_mix_ref, w_in_ref, wa_ref, wb_ref, bias_ref, ln_g_ref,
                   ln_b_ref, w_out_ref, g_ffn_ref, w_gate_ref, w_up_ref, w_down_ref, g_final_ref,
                   y_ref, new_a_ref, new_b_ref, exta_ref, extb_ref, bg_ref, ycat_ref,
                   *, t, tile, row_chunk, tiles_per_seq, ff_chunk, stage1=True, stage3=True):
    assert stage1 or stage3
    gw = wa_ref.shape[-1]
    n_lane = gw // LANES
    hist_a = exta_ref.shape[1] - tile
    hist_b = extb_ref.shape[1] - tile
    never = t < 0

    if stage3:
        x1 = x3_ref[...] + _dot(ycat_ref[...], w_out_ref[...])

    starts = list(range(0, tile, row_chunk))
    units = []
    for j in range(n_lane):
        a = _conv_windows(exta_ref, j, [hist_a - (K_A - 1) + s for s in starts], row_chunk,
                          wa_ref, K_A)
        b = _conv_windows(extb_ref, j, [hist_b - (K_B - 1) + s for s in starts], row_chunk,
                          wb_ref, K_B)
        units.append(a + b)
    conv_a = [jnp.concatenate(u[:len(starts)], axis=0) for u in units]
    conv_b = [jnp.concatenate(u[len(starts):], axis=0) for u in units]
    y_cat = _mix(bg_ref[...], jnp.concatenate(conv_a, axis=1), jnp.concatenate(conv_b, axis=1),
                 bias_ref, ln_g_ref, ln_b_ref)
    tail_a = exta_ref[:, tile:tile + hist_a, :]
    tail_b = extb_ref[:, tile:tile + hist_b, :]

    if stage1:
        h = _rmsnorm(x1_ref[...], g_mix_ref[...]).astype(BF16)
        in_proj = lambda k, wait=None: _dot(_after(h, wait, never),
                                            w_in_ref[:, k * gw:(k + 1) * gw])
        glu_val, glu_gate = in_proj(3), in_proj(4)
    if stage3:
        h2 = _rmsnorm(x1, g_ffn_ref[...]).astype(BF16)
        x2 = _ffn(x1, h2, w_gate_ref, w_up_ref, w_down_ref, ff_chunk,
                  hide={c: units[j] for c, j in CONV_UNIT_BEFORE_FFN_CHUNK.items()}, never=never)
        c_gate, v, b_gate = (in_proj(1), in_proj(2), in_proj(0)) if stage1 else (None,) * 3
    else:
        c_gate, v = in_proj(1, units[0]), in_proj(2, units[1])
        b_gate = in_proj(0, units[2] + units[3])
    if stage3:
        y_ref[...] = _rmsnorm(x2, g_final_ref[...])

    if stage1:
        seq_start = (t % tiles_per_seq) == 0
        cv = c_gate * v
        u_b = glu_val * _sigmoid(glu_gate)
        exta_ref[:, 0:hist_a, :] = jnp.where(seq_start, 0.0, tail_a)
        extb_ref[:, 0:hist_b, :] = jnp.where(seq_start, 0.0, tail_b)
        for j in range(n_lane):
            lanes = slice(j * LANES, (j + 1) * LANES)
            exta_ref[j, hist_a:hist_a + tile, :] = cv[:, lanes]
            extb_ref[j, hist_b:hist_b + tile, :] = u_b[:, lanes]
        bg_ref[...] = b_gate
        new_a_ref[...] = cv[tile - (K_A - 1):, :]
        new_b_ref[...] = u_b[tile - (K_B - 1):, :]

    ycat_ref[...] = y_cat


def _sample_mixer(x_ref, sa_ref, sb_ref, g_mix_ref, w_in_ref, wa_ref, wb_ref, bias_ref,
                  ln_g_ref, ln_b_ref, new_a_ref, new_b_ref, exta_ref, rows_ref, slabs_ref,
                  *, n_seq, seq_len, seq_group):
    gw = wa_ref.shape[-1]
    n_lane = gw // LANES
    ra = exta_ref.shape[1] // n_seq
    off_a = ra - seq_len - (K_A - 1)
    n_hist = K_B - 1

    h = _rmsnorm(x_ref[...], g_mix_ref[...]).astype(BF16)
    in_proj = lambda k: _dot(h, w_in_ref[:, k * gw:(k + 1) * gw])
    u_b = in_proj(3) * _sigmoid(in_proj(4))
    for j in range(n_lane):
        rows_ref[j] = u_b[:, j * LANES:(j + 1) * LANES]
        for t in range(seq_len):
            slabs_ref[j, t * n_seq:(t + 1) * n_seq, :] = (
                rows_ref[j, pl.ds(t, n_seq, stride=seq_len), :])

    def slab(j, i):
        if i < n_hist:
            return sb_ref[i, :, j * LANES:(j + 1) * LANES]
        return slabs_ref[j, (i - n_hist) * n_seq:(i - n_hist + 1) * n_seq, :]

    conv_b = []
    for j in range(n_lane):
        lanes = slice(j * LANES, (j + 1) * LANES)
        accs = [None] * seq_len
        for i in range(n_hist + seq_len):
            xi = slab(j, i)
            for t in range(seq_len):
                k = i - t
                if 0 <= k < K_B:
                    term = wb_ref[k, :, lanes] * xi
                    accs[t] = term if accs[t] is None else accs[t] + term
        for t in range(seq_len):
            rows_ref[j, pl.ds(t, n_seq, stride=seq_len), :] = accs[t]
        conv_b.append(rows_ref[j])

    cv = in_proj(1) * in_proj(2)
    b_gate = in_proj(0)
    conv_a = []
    for j in range(n_lane):
        lanes = slice(j * LANES, (j + 1) * LANES)
        for s in range(n_seq):
            rows = slice(s * seq_len, (s + 1) * seq_len)
            exta_ref[j, s * ra + off_a:s * ra + off_a + K_A - 1, :] = sa_ref[s, :, lanes]
            exta_ref[j, (s + 1) * ra - seq_len:(s + 1) * ra, :] = cv[rows, lanes]
        a = []
        for s0 in range(0, n_seq, seq_group):
            group = range(s0, s0 + seq_group)
            a += _conv_windows(exta_ref, j, [s * ra + off_a for s in group], seq_len, wa_ref, K_A)
        conv_a.append(jnp.concatenate(a, axis=0))

    for j in range(n_lane):
        lanes = slice(j * LANES, (j + 1) * LANES)
        for s in range(n_seq):
            new_a_ref[s, :, lanes] = exta_ref[j, (s + 1) * ra - (K_A - 1):(s + 1) * ra, :]
        for r in range(n_hist):
            new_b_ref[r, :, lanes] = slab(j, seq_len + r)
    return _mix(b_gate, jnp.concatenate(conv_a, axis=1), jnp.concatenate(conv_b, axis=1),
                bias_ref, ln_g_ref, ln_b_ref)


def _sample_kernel(x_ref, sa_ref, sb_ref, g_mix_ref, wa_ref, wb_ref, bias_ref, ln_g_ref,
                   ln_b_ref, g_ffn_ref, g_final_ref, w_in_hbm, w_out_hbm, w_gate_hbm, w_up_hbm,
                   w_down_hbm, y_ref, new_a_ref, new_b_ref, exta_ref, rows_ref, slabs_ref,
                   w_in_ref, w_out_ref, w_gate_ref, w_up_ref, w_down_ref, sem_ref,
                   *, ff_chunk, **mixer):
    first = pl.program_id(0) == 0
    loads = [pltpu.make_async_copy(src, dst, sem_ref.at[k]) for k, (src, dst) in enumerate(
        [(w_in_hbm, w_in_ref), (w_out_hbm, w_out_ref), (w_gate_hbm, w_gate_ref),
         (w_up_hbm, w_up_ref), (w_down_hbm, w_down_ref)])]

    @pl.when(first)
    def _():
        for load in loads:
            load.start()
        loads[0].wait()

    y_cat = _sample_mixer(x_ref, sa_ref, sb_ref, g_mix_ref, w_in_ref, wa_ref, wb_ref, bias_ref,
                          ln_g_ref, ln_b_ref, new_a_ref, new_b_ref, exta_ref, rows_ref,
                          slabs_ref, **mixer)

    @pl.when(first)
    def _():
        for load in loads[1:]:
            load.wait()

    x1 = x_ref[...] + _dot(y_cat, w_out_ref[...])
    h2 = _rmsnorm(x1, g_ffn_ref[...]).astype(BF16)
    x2 = _ffn(x1, h2, w_gate_ref, w_up_ref, w_down_ref, ff_chunk)
    y_ref[...] = _rmsnorm(x2, g_final_ref[...])


N_PARAMS = 13
WEIGHT_SLOTS = (1, 7, 9, 10, 11)
CONVERT_ORDER = (0, 2, 3, 4, 1)
WEIGHT_CHUNK_ROWS = 256
WEIGHT_RING = 4


def _weight_pieces(w_hbm_refs, w_vmem_refs, chunk):
    return [(w_hbm_refs[k], w_vmem_refs[k], r0) for k in CONVERT_ORDER
            for r0 in range(0, w_hbm_refs[k].shape[0], chunk)]


def _convert_weights(pieces, first, last, stage_ref, sem_ref):
    ring, chunk = stage_ref.shape[0], stage_ref.shape[1]

    def fetch(p):
        src, _, r0 = pieces[p]
        return pltpu.make_async_copy(src.at[r0:r0 + chunk, :],
                                     stage_ref.at[p % ring, :, 0:src.shape[1]],
                                     sem_ref.at[p % ring])

    if first == 0:
        for p in range(min(ring - 1, len(pieces))):
            fetch(p).start()
    for p in range(first, last):
        src, dst, r0 = pieces[p]
        if p + ring - 1 < len(pieces):
            fetch(p + ring - 1).start()
        fetch(p).wait()
        dst[r0:r0 + chunk, :] = stage_ref[p % ring, :, 0:src.shape[1]].astype(BF16)


def _prompt_entry(*refs, n_steps, step):
    n_w = len(WEIGHT_SLOTS)
    x1_ref, x3_ref = refs[:2]
    params = list(refs[2:2 + N_PARAMS - n_w])
    w_hbm = refs[2 + N_PARAMS - n_w:2 + N_PARAMS]
    outs = refs[2 + N_PARAMS:5 + N_PARAMS]
    w_out_hbm = refs[5 + N_PARAMS:5 + N_PARAMS + n_w]
    scratch = refs[5 + N_PARAMS + n_w:]
    exta_ref, extb_ref, bg_ref, ycat_ref = scratch[:4]
    w_vmem = scratch[4:4 + n_w]
    stage_ref, sem_ref, out_sem_ref = scratch[4 + n_w:]
    g = pl.program_id(0)

    def copy_out(k):
        return pltpu.make_async_copy(w_vmem[k], w_out_hbm[k], out_sem_ref.at[k])

    pieces = _weight_pieces(w_hbm, w_vmem, stage_ref.shape[1])
    n_first = w_hbm[CONVERT_ORDER[0]].shape[0] // stage_ref.shape[1]
    n_second = n_first + stage_ref.shape[0] - 1
    convert = lambda first, last: _convert_weights(pieces, first, last, stage_ref, sem_ref)

    for slot, w in zip(WEIGHT_SLOTS, w_vmem):
        params.insert(slot, w)
    y_ref, new_a_ref, new_b_ref = outs
    tile = step["tile"]

    def run(sub, **stages):
        rows = pl.ds(sub * tile, tile)
        _prompt_kernel(x1_ref.at[rows], x3_ref.at[rows], *params, y_ref.at[rows], new_a_ref,
                       new_b_ref, exta_ref, extb_ref, bg_ref, ycat_ref, t=SUBSTEPS * g + sub,
                       **step, **stages)

    @pl.when(g == 0)
    def _():
        exta_ref[...] = jnp.zeros(exta_ref.shape, F32)
        extb_ref[...] = jnp.zeros(extb_ref.shape, F32)
        bg_ref[...] = jnp.zeros(bg_ref.shape, F32)
        convert(0, n_first)
        run(0, stage3=False)
        convert(n_first, n_second)
        run(1, stage3=False)

    @pl.when(g == 1)
    def _():
        convert(n_second, len(pieces))
        for k in range(n_w):
            copy_out(k).start()

    @pl.when(jnp.logical_and(g > 0, g < n_steps - 1))
    def _():
        run(0)
        run(1)

    @pl.when(g == n_steps - 1)
    def _():
        run(0, stage1=False)
        run(1, stage1=False)
        for k in range(n_w):
            copy_out(k).wait()


def _round_up(n, m):
    return -(-n // m) * m


def _const_spec(shape):
    zeros = (0,) * len(shape)
    return pl.BlockSpec(shape, lambda g: zeros, pipeline_mode=pl.Buffered(1))


PROMPT_TILE = 256
SUBSTEPS = 2
PROMPT_ROW_CHUNK = 64
CONV_UNIT_BEFORE_FFN_CHUNK = {1: 0, 2: 1, 3: 2, 4: 3}
FF_CHUNK = 512
SAMPLE_SEQS = 32
SAMPLE_SEQ_GROUP = 16
VMEM_LIMIT_BYTES = 56 * 1024 * 1024


def kernel(x_prompt, x_sample, state_conv_a, state_conv_b, g_mix, w_in, conv_a_w, conv_b_w,
           conv_b_bias, ln_b_g, ln_b_b, w_out, g_ffn, w_gate, w_up, w_down, g_final):
    depth = g_mix.shape[0]
    assert depth == 1, "single-layer trunk"
    bp, seq, d = x_prompt.shape
    bs, dec_seq, _ = x_sample.shape
    gw = conv_a_w.shape[-1]
    assert conv_a_w.shape[1] == K_A and conv_b_w.shape[1] == K_B
    assert dec_seq == SUBLANES and gw % LANES == 0
    assert seq % PROMPT_TILE == 0 and bs % SAMPLE_SEQS == 0

    row = lambda a: a.reshape(1, -1)
    taps = lambda w: jnp.swapaxes(w, 0, 1)
    small = (row(g_mix[0]), taps(conv_a_w), taps(conv_b_w), row(conv_b_bias[0]), row(ln_b_g[0]),
             row(ln_b_b[0]), row(g_ffn[0]), row(g_final))
    weights = (w_in[0], w_out[0], w_gate[0], w_up[0], w_down[0])
    assert len(small) + len(weights) == N_PARAMS
    assert all(w.shape[0] % WEIGHT_CHUNK_ROWS == 0 for w in weights)

    tile = PROMPT_TILE
    tiles_per_seq = seq // tile
    assert tiles_per_seq % SUBSTEPS == 0
    n_blocks = bp * tiles_per_seq // SUBSTEPS
    n_steps = n_blocks + 1
    block = SUBSTEPS * tile
    hist_a = _round_up(K_A - 1, SUBLANES)
    hist_b = _round_up(K_B - 1, SUBLANES)
    n_lane = gw // LANES
    tile_1 = lambda g: jnp.minimum(g, n_blocks - 1)
    tile_3 = lambda g: jnp.maximum(g - 1, 0)
    tiles_per_seq_blocks = tiles_per_seq // SUBSTEPS
    any_spec = pl.BlockSpec(memory_space=pl.ANY)
    x_rows = x_prompt.reshape(bp * seq, d)
    y_prompt, pa, pb, *weights_bf16 = pl.pallas_call(
        functools.partial(
            _prompt_entry, n_steps=n_steps,
            step=dict(tile=tile, row_chunk=PROMPT_ROW_CHUNK, tiles_per_seq=tiles_per_seq,
                      ff_chunk=FF_CHUNK)),
        grid=(n_steps,),
        in_specs=[pl.BlockSpec((block, d), lambda g: (tile_1(g), 0)),
                  pl.BlockSpec((block, d), lambda g: (tile_3(g), 0))]
        + [_const_spec(p.shape) for p in small] + [any_spec] * len(weights),
        out_specs=[pl.BlockSpec((block, d), lambda g: (tile_3(g), 0)),
                   pl.BlockSpec((None, K_A - 1, gw),
                                lambda g: (tile_1(g) // tiles_per_seq_blocks, 0, 0)),
                   pl.BlockSpec((None, K_B - 1, gw),
                                lambda g: (tile_1(g) // tiles_per_seq_blocks, 0, 0))]
        + [any_spec] * len(weights),
        out_shape=[jax.ShapeDtypeStruct((bp * seq, d), F32),
                   jax.ShapeDtypeStruct((bp, K_A - 1, gw), F32),
                   jax.ShapeDtypeStruct((bp, K_B - 1, gw), F32)]
        + [jax.ShapeDtypeStruct(w.shape, BF16) for w in weights],
        scratch_shapes=[pltpu.VMEM((n_lane, hist_a + tile, LANES), F32),
                        pltpu.VMEM((n_lane, hist_b + tile, LANES), F32),
                        pltpu.VMEM((tile, gw), F32),
                        pltpu.VMEM((tile, 2 * gw), BF16)]
        + [pltpu.VMEM(w.shape, BF16) for w in weights]
        + [pltpu.VMEM((WEIGHT_RING, WEIGHT_CHUNK_ROWS, max(w.shape[1] for w in weights)), F32),
           pltpu.SemaphoreType.DMA((WEIGHT_RING,)),
           pltpu.SemaphoreType.DMA((len(weights),))],
        compiler_params=pltpu.CompilerParams(
            dimension_semantics=("arbitrary",),
            vmem_limit_bytes=VMEM_LIMIT_BYTES),
        name="prompt_layer",
    )(x_rows, x_rows, *small, *weights)
    y_prompt = y_prompt.reshape(bp, seq, d)

    by_row = lambda a: jnp.swapaxes(a, 0, 1)
    n_seq = SAMPLE_SEQS
    rows = n_seq * dec_seq
    ra = _round_up(K_A - 1, SUBLANES) + dec_seq
    y_sample, sa, sb = pl.pallas_call(
        functools.partial(_sample_kernel, n_seq=n_seq, seq_len=dec_seq,
                          seq_group=SAMPLE_SEQ_GROUP, ff_chunk=FF_CHUNK),
        grid=(bs // n_seq,),
        in_specs=[pl.BlockSpec((rows, d), lambda i: (i, 0)),
                  pl.BlockSpec((n_seq, K_A - 1, gw), lambda i: (i, 0, 0)),
                  pl.BlockSpec((K_B - 1, n_seq, gw), lambda i: (0, i, 0))]
        + [_const_spec(p.shape) for p in small]
        + [pl.BlockSpec(memory_space=pl.ANY) for _ in weights_bf16],
        out_specs=[pl.BlockSpec((rows, d), lambda i: (i, 0)),
                   pl.BlockSpec((n_seq, K_A - 1, gw), lambda i: (i, 0, 0)),
                   pl.BlockSpec((K_B - 1, n_seq, gw), lambda i: (0, i, 0))],
        out_shape=[jax.ShapeDtypeStruct((bs * dec_seq, d), F32),
                   jax.ShapeDtypeStruct((bs, K_A - 1, gw), F32),
                   jax.ShapeDtypeStruct((K_B - 1, bs, gw), F32)],
        scratch_shapes=[pltpu.VMEM((n_lane, n_seq * ra, LANES), F32),
                        pltpu.VMEM((n_lane, rows, LANES), F32),
                        pltpu.VMEM((n_lane, rows, LANES), F32)]
        + [pltpu.VMEM(w.shape, BF16) for w in weights_bf16]
        + [pltpu.SemaphoreType.DMA((len(weights_bf16),))],
        compiler_params=pltpu.CompilerParams(
            dimension_semantics=("arbitrary",),
            vmem_limit_bytes=VMEM_LIMIT_BYTES),
        name="sample_layer",
    )(x_sample.reshape(bs * dec_seq, d), state_conv_a[0], by_row(state_conv_b[0]),
      *small, *weights_bf16)

    return (y_prompt, y_sample.reshape(bs, dec_seq, d), pa[None], pb[None],
            sa[None], by_row(sb)[None])
```

```python
import functools

import jax
import jax.numpy as jnp
from jax import lax
from jax.experimental import pallas as pl
from jax.experimental.pallas import tpu as pltpu

EPS = 1e-6
K_A = 3
K_B = 31
LANES = 128
SUBLANES = 8

F32 = jnp.float32
BF16 = jnp.bfloat16


def _rmsnorm(x, g):
    ms = jnp.mean(x * x, axis=-1, keepdims=True)
    return x * lax.rsqrt(ms + EPS) * g


def _layernorm(x, g, b):
    mu = jnp.mean(x, axis=-1, keepdims=True)
    xc = x - mu
    var = jnp.mean(xc * xc, axis=-1, keepdims=True)
    return xc * lax.rsqrt(var + EPS) * g + b


def _sigmoid(x):
    return 1.0 / (1.0 + jnp.exp(-x))


def _silu(x):
    return x * _sigmoid(x)


def _dot(a, b):
    return jnp.dot(a, b, preferred_element_type=F32)


def _conv_windows(ext_ref, j, starts, nrows, w_ref, n_taps):
    lanes = slice(j * LANES, (j + 1) * LANES)
    accs = [None] * len(starts)
    for k in range(n_taps):
        wk = w_ref[k, :, lanes]
        for i, s in enumerate(starts):
            term = wk * ext_ref[j, pl.ds(s + k, nrows), :]
            accs[i] = term if accs[i] is None else accs[i] + term
    return accs


def _after(lhs, pieces, never):
    if not pieces:
        return lhs
    rows = 2 * SUBLANES
    total = None
    for p in pieces:
        for r in range(0, p.shape[0], rows):
            for c in range(0, p.shape[1], LANES):
                slab = p[r:r + rows, c:c + LANES].astype(lhs.dtype)
                total = slab if total is None else total + slab
    head = jnp.where(never, total, lhs[0:rows, 0:LANES])
    top = jnp.concatenate([head, lhs[0:rows, LANES:]], axis=1)
    return jnp.concatenate([top, lhs[rows:, :]], axis=0)


def _mix(b_gate, conv_a, conv_b, bias_ref, ln_g_ref, ln_b_ref):
    y_b = _silu(_layernorm(conv_b + bias_ref[...], ln_g_ref[...], ln_b_ref[...]))
    return jnp.concatenate([b_gate * conv_a, y_b], axis=-1).astype(BF16)


def _ffn(x1, h2, w_gate_ref, w_up_ref, w_down_ref, ff_chunk, hide=None, never=None,
         before_chunk=None):
    d_ff = w_gate_ref.shape[1]
    acc = None
    pending = None
    for c, c0 in enumerate(range(0, d_ff, ff_chunk)):
        cols = slice(c0, min(c0 + ff_chunk, d_ff))
        if before_chunk and c in before_chunk:
            before_chunk[c]()
        lhs = _after(h2, (hide or {}).get(c), never)
        gate = _dot(lhs, w_gate_ref[:, cols])
        up = _dot(lhs, w_up_ref[:, cols])
        if pending is not None:
            part = _dot(pending[0], w_down_ref[pending[1], :])
            acc = part if acc is None else acc + part
        pending = ((_silu(gate) * up).astype(BF16), cols)
    part = _dot(pending[0], w_down_ref[pending[1], :])
    acc = part if acc is None else acc + part
    return x1 + acc


def _prompt_kernel(x1_ref, x3_ref, g_mix_ref, w_in_ref, wa_ref, wb_ref, bias_ref, ln_g_ref,
                   ln_b_ref, w_out_ref, g_ffn_ref, w_gate_ref, w_up_ref, w_down_ref, g_final_ref,
                   y_ref, new_a_ref, new_b_ref, exta_ref, extb_ref, bg_ref, ycat_ref,
                   *, t, tile, row_chunk, tiles_per_seq, ff_chunk, stage1=True, stage3=True):
    assert stage1 or stage3
    gw = wa_ref.shape[-1]
    n_lane = gw // LANES
    hist_a = exta_ref.shape[1] - tile
    hist_b = extb_ref.shape[1] - tile
    never = t < 0

    if stage3:
        x1 = x3_ref[...] + _dot(ycat_ref[...], w_out_ref[...])

    starts = list(range(0, tile, row_chunk))
    units = []
    for j in range(n_lane):
        a = _conv_windows(exta_ref, j, [hist_a - (K_A - 1) + s for s in starts], row_chunk,
                          wa_ref, K_A)
        b = _conv_windows(extb_ref, j, [hist_b - (K_B - 1) + s for s in starts], row_chunk,
                          wb_ref, K_B)
        units.append(a + b)
    conv_a = [jnp.concatenate(u[:len(starts)], axis=0) for u in units]
    conv_b = [jnp.concatenate(u[len(starts):], axis=0) for u in units]
    y_cat = _mix(bg_ref[...], jnp.concatenate(conv_a, axis=1), jnp.concatenate(conv_b, axis=1),
                 bias_ref, ln_g_ref, ln_b_ref)
    tail_a = exta_ref[:, tile:tile + hist_a, :]
    tail_b = extb_ref[:, tile:tile + hist_b, :]

    if stage1:
        h = _rmsnorm(x1_ref[...], g_mix_ref[...]).astype(BF16)
        in_proj = lambda k, wait=None: _dot(_after(h, wait, never),
                                            w_in_ref[:, k * gw:(k + 1) * gw])
        glu_val, glu_gate = in_proj(3), in_proj(4)
    if stage3:
        h2 = _rmsnorm(x1, g_ffn_ref[...]).astype(BF16)
        x2 = _ffn(x1, h2, w_gate_ref, w_up_ref, w_down_ref, ff_chunk,
                  hide={c: units[j] for c, j in CONV_UNIT_BEFORE_FFN_CHUNK.items()}, never=never)
        c_gate, v, b_gate = (in_proj(1), in_proj(2), in_proj(0)) if stage1 else (None,) * 3
    else:
        c_gate, v = in_proj(1, units[0]), in_proj(2, units[1])
        b_gate = in_proj(0, units[2] + units[3])
    if stage3:
        y_ref[...] = _rmsnorm(x2, g_final_ref[...])

    if stage1:
        seq_start = (t % tiles_per_seq) == 0
        cv = c_gate * v
        u_b = glu_val * _sigmoid(glu_gate)
        exta_ref[:, 0:hist_a, :] = jnp.where(seq_start, 0.0, tail_a)
        extb_ref[:, 0:hist_b, :] = jnp.where(seq_start, 0.0, tail_b)
        for j in range(n_lane):
            lanes = slice(j * LANES, (j + 1) * LANES)
            exta_ref[j, hist_a:hist_a + tile, :] = cv[:, lanes]
            extb_ref[j, hist_b:hist_b + tile, :] = u_b[:, lanes]
        bg_ref[...] = b_gate
        new_a_ref[...] = cv[tile - (K_A - 1):, :]
        new_b_ref[...] = u_b[tile - (K_B - 1):, :]

    ycat_ref[...] = y_cat


def _sample_mixer(x_ref, sa_ref, sb_ref, g_mix_ref, w_in_ref, wa_ref, wb_ref, bias_ref,
                  ln_g_ref, ln_b_ref, new_a_ref, new_b_ref, exta_ref, rows_ref, slabs_ref,
                  *, n_seq, seq_len, seq_group):
    gw = wa_ref.shape[-1]
    n_lane = gw // LANES
    ra = exta_ref.shape[1] // n_seq
    off_a = ra - seq_len - (K_A - 1)
    n_hist = K_B - 1

    h = _rmsnorm(x_ref[...], g_mix_ref[...]).astype(BF16)
    in_proj = lambda k: _dot(h, w_in_ref[:, k * gw:(k + 1) * gw])
    u_b = in_proj(3) * _sigmoid(in_proj(4))
    for j in range(n_lane):
        rows_ref[j] = u_b[:, j * LANES:(j + 1) * LANES]
        for t in range(seq_len):
            slabs_ref[j, t * n_seq:(t + 1) * n_seq, :] = (
                rows_ref[j, pl.ds(t, n_seq, stride=seq_len), :])

    def slab(j, i):
        if i < n_hist:
            return sb_ref[i, :, j * LANES:(j + 1) * LANES]
        return slabs_ref[j, (i - n_hist) * n_seq:(i - n_hist + 1) * n_seq, :]

    conv_b = []
    for j in range(n_lane):
        lanes = slice(j * LANES, (j + 1) * LANES)
        accs = [None] * seq_len
        for i in range(n_hist + seq_len):
            xi = slab(j, i)
            for t in range(seq_len):
                k = i - t
                if 0 <= k < K_B:
                    term = wb_ref[k, :, lanes] * xi
                    accs[t] = term if accs[t] is None else accs[t] + term
        for t in range(seq_len):
            rows_ref[j, pl.ds(t, n_seq, stride=seq_len), :] = accs[t]
        conv_b.append(rows_ref[j])

    cv = in_proj(1) * in_proj(2)
    b_gate = in_proj(0)
    conv_a = []
    for j in range(n_lane):
        lanes = slice(j * LANES, (j + 1) * LANES)
        for s in range(n_seq):
            rows = slice(s * seq_len, (s + 1) * seq_len)
            exta_ref[j, s * ra + off_a:s * ra + off_a + K_A - 1, :] = sa_ref[s, :, lanes]
            exta_ref[j, (s + 1) * ra - seq_len:(s + 1) * ra, :] = cv[rows, lanes]
        a = []
        for s0 in range(0, n_seq, seq_group):
            group = range(s0, s0 + seq_group)
            a += _conv_windows(exta_ref, j, [s * ra + off_a for s in group], seq_len, wa_ref, K_A)
        conv_a.append(jnp.concatenate(a, axis=0))

    for j in range(n_lane):
        lanes = slice(j * LANES, (j + 1) * LANES)
        for s in range(n_seq):
            new_a_ref[s, :, lanes] = exta_ref[j, (s + 1) * ra - (K_A - 1):(s + 1) * ra, :]
        for r in range(n_hist):
            new_b_ref[r, :, lanes] = slab(j, seq_len + r)
    return _mix(b_gate, jnp.concatenate(conv_a, axis=1), jnp.concatenate(conv_b, axis=1),
                bias_ref, ln_g_ref, ln_b_ref)


def _sample_kernel(x_ref, sa_ref, sb_ref, g_mix_ref, wa_ref, wb_ref, bias_ref, ln_g_ref,
                   ln_b_ref, g_ffn_ref, g_final_ref, w_in_hbm, w_out_hbm, w_gate_hbm, w_up_hbm,
                   w_down_hbm, y_ref, new_a_ref, new_b_ref, exta_ref, rows_ref, slabs_ref,
                   w_in_ref, w_out_ref, w_gate_ref, w_up_ref, w_down_ref, sem_ref,
                   *, ff_chunk, **mixer):
    d_ff = w_gate_ref.shape[1]
    chunks = [slice(c0, min(c0 + ff_chunk, d_ff)) for c0 in range(0, d_ff, ff_chunk)]
    split = len(chunks) // 2
    pieces = [(w_in_hbm, w_in_ref), (w_out_hbm, w_out_ref)]
    for cols in chunks:
        pieces += [(w_gate_hbm.at[:, cols], w_gate_ref.at[:, cols]),
                   (w_up_hbm.at[:, cols], w_up_ref.at[:, cols]),
                   (w_down_hbm.at[cols, :], w_down_ref.at[cols, :])]
    loads = [pltpu.make_async_copy(src, dst, sem_ref.at[k])
             for k, (src, dst) in enumerate(pieces)]
    early, late = loads[1:2 + 3 * split], loads[2 + 3 * split:]

    def wait_late():
        for load in late:
            load.wait()

    def step(first):
        if first:
            loads[0].start()
            loads[0].wait()
            for load in early:
                load.start()
        y_cat = _sample_mixer(x_ref, sa_ref, sb_ref, g_mix_ref, w_in_ref, wa_ref, wb_ref,
                              bias_ref, ln_g_ref, ln_b_ref, new_a_ref, new_b_ref, exta_ref,
                              rows_ref, slabs_ref, **mixer)
        if first:
            for load in early:
                load.wait()
            for load in late:
                load.start()
        x1 = x_ref[...] + _dot(y_cat, w_out_ref[...])
        h2 = _rmsnorm(x1, g_ffn_ref[...]).astype(BF16)
        x2 = _ffn(x1, h2, w_gate_ref, w_up_ref, w_down_ref, ff_chunk,
                  before_chunk={split: wait_late} if first else None)
        y_ref[...] = _rmsnorm(x2, g_final_ref[...])

    is_first = pl.program_id(0) == 0
    pl.when(is_first)(lambda: step(True))
    pl.when(jnp.logical_not(is_first))(lambda: step(False))


N_PARAMS = 13
WEIGHT_SLOTS = (1, 7, 9, 10, 11)
CONVERT_ORDER = (0, 2, 3, 4, 1)
WEIGHT_CHUNK_ROWS = 256
WEIGHT_RING = 4


def _weight_pieces(w_hbm_refs, w_vmem_refs, chunk):
    return [(w_hbm_refs[k], w_vmem_refs[k], r0) for k in CONVERT_ORDER
            for r0 in range(0, w_hbm_refs[k].shape[0], chunk)]


def _convert_weights(pieces, first, last, stage_ref, sem_ref):
    ring, chunk = stage_ref.shape[0], stage_ref.shape[1]

    def fetch(p):
        src, _, r0 = pieces[p]
        return pltpu.make_async_copy(src.at[r0:r0 + chunk, :],
                                     stage_ref.at[p % ring, :, 0:src.shape[1]],
                                     sem_ref.at[p % ring])

    if first == 0:
        for p in range(min(ring - 1, len(pieces))):
            fetch(p).start()
    for p in range(first, last):
        src, dst, r0 = pieces[p]
        if p + ring - 1 < len(pieces):
            fetch(p + ring - 1).start()
        fetch(p).wait()
        dst[r0:r0 + chunk, :] = stage_ref[p % ring, :, 0:src.shape[1]].astype(BF16)


def _prompt_entry(*refs, n_steps, step):
    n_w = len(WEIGHT_SLOTS)
    x1_ref, x3_ref = refs[:2]
    params = list(refs[2:2 + N_PARAMS - n_w])
    w_hbm = refs[2 + N_PARAMS - n_w:2 + N_PARAMS]
    outs = refs[2 + N_PARAMS:5 + N_PARAMS]
    w_out_hbm = refs[5 + N_PARAMS:5 + N_PARAMS + n_w]
    scratch = refs[5 + N_PARAMS + n_w:]
    exta_ref, extb_ref, bg_ref, ycat_ref = scratch[:4]
    w_vmem = scratch[4:4 + n_w]
    stage_ref, sem_ref, out_sem_ref = scratch[4 + n_w:]
    g = pl.program_id(0)

    def copy_out(k):
        return pltpu.make_async_copy(w_vmem[k], w_out_hbm[k], out_sem_ref.at[k])

    pieces = _weight_pieces(w_hbm, w_vmem, stage_ref.shape[1])
    n_first = w_hbm[CONVERT_ORDER[0]].shape[0] // stage_ref.shape[1]
    n_second = n_first + stage_ref.shape[0] - 1
    convert = lambda first, last: _convert_weights(pieces, first, last, stage_ref, sem_ref)

    for slot, w in zip(WEIGHT_SLOTS, w_vmem):
        params.insert(slot, w)
    y_ref, new_a_ref, new_b_ref = outs
    tile = step["tile"]

    def run(sub, **stages):
        rows = pl.ds(sub * tile, tile)
        _prompt_kernel(x1_ref.at[rows], x3_ref.at[rows], *params, y_ref.at[rows], new_a_ref,
                       new_b_ref, exta_ref, extb_ref, bg_ref, ycat_ref, t=SUBSTEPS * g + sub,
                       **step, **stages)

    @pl.when(g == 0)
    def _():
        exta_ref[...] = jnp.zeros(exta_ref.shape, F32)
        extb_ref[...] = jnp.zeros(extb_ref.shape, F32)
        bg_ref[...] = jnp.zeros(bg_ref.shape, F32)
        convert(0, n_first)
        run(0, stage3=False)
        convert(n_first, n_second)
        run(1, stage3=False)

    @pl.when(g == 1)
    def _():
        convert(n_second, len(pieces))
        for k in range(n_w):
            copy_out(k).start()

    @pl.when(jnp.logical_and(g > 0, g < n_steps - 1))
    def _():
        run(0)
        run(1)

    @pl.when(g == n_steps - 1)
    def _():
        run(0, stage1=False)
        run(1, stage1=False)
        for k in range(n_w):
            copy_out(k).wait()


def _round_up(n, m):
    return -(-n // m) * m


def _const_spec(shape):
    zeros = (0,) * len(shape)
    return pl.BlockSpec(shape, lambda g: zeros, pipeline_mode=pl.Buffered(1))


PROMPT_TILE = 256
SUBSTEPS = 2
PROMPT_ROW_CHUNK = 64
CONV_UNIT_BEFORE_FFN_CHUNK = {1: 0, 2: 1, 3: 2, 4: 3}
FF_CHUNK = 512
SAMPLE_SEQS = 32
SAMPLE_SEQ_GROUP = 16
VMEM_LIMIT_BYTES = 56 * 1024 * 1024


def kernel(x_prompt, x_sample, state_conv_a, state_conv_b, g_mix, w_in, conv_a_w, conv_b_w,
           conv_b_bias, ln_b_g, ln_b_b, w_out, g_ffn, w_gate, w_up, w_down, g_final):
    depth = g_mix.shape[0]
    assert depth == 1, "single-layer trunk"
    bp, seq, d = x_prompt.shape
    bs, dec_seq, _ = x_sample.shape
    gw = conv_a_w.shape[-1]
    assert conv_a_w.shape[1] == K_A and conv_b_w.shape[1] == K_B
    assert dec_seq == SUBLANES and gw % LANES == 0
    assert seq % PROMPT_TILE == 0 and bs % SAMPLE_SEQS == 0

    row = lambda a: a.reshape(1, -1)
    taps = lambda w: jnp.swapaxes(w, 0, 1)
    small = (row(g_mix[0]), taps(conv_a_w), taps(conv_b_w), row(conv_b_bias[0]), row(ln_b_g[0]),
             row(ln_b_b[0]), row(g_ffn[0]), row(g_final))
    weights = (w_in[0], w_out[0], w_gate[0], w_up[0], w_down[0])
    assert len(small) + len(weights) == N_PARAMS
    assert all(w.shape[0] % WEIGHT_CHUNK_ROWS == 0 for w in weights)

    tile = PROMPT_TILE
    tiles_per_seq = seq // tile
    assert tiles_per_seq % SUBSTEPS == 0
    n_blocks = bp * tiles_per_seq // SUBSTEPS
    n_steps = n_blocks + 1
    block = SUBSTEPS * tile
    hist_a = _round_up(K_A - 1, SUBLANES)
    hist_b = _round_up(K_B - 1, SUBLANES)
    n_lane = gw // LANES
    tile_1 = lambda g: jnp.minimum(g, n_blocks - 1)
    tile_3 = lambda g: jnp.maximum(g - 1, 0)
    tiles_per_seq_blocks = tiles_per_seq // SUBSTEPS
    any_spec = pl.BlockSpec(memory_space=pl.ANY)
    x_rows = x_prompt.reshape(bp * seq, d)
    y_prompt, pa, pb, *weights_bf16 = pl.pallas_call(
        functools.partial(
            _prompt_entry, n_steps=n_steps,
            step=dict(tile=tile, row_chunk=PROMPT_ROW_CHUNK, tiles_per_seq=tiles_per_seq,
                      ff_chunk=FF_CHUNK)),
        grid=(n_steps,),
        in_specs=[pl.BlockSpec((block, d), lambda g: (tile_1(g), 0)),
                  pl.BlockSpec((block, d), lambda g: (tile_3(g), 0))]
        + [_const_spec(p.shape) for p in small] + [any_spec] * len(weights),
        out_specs=[pl.BlockSpec((block, d), lambda g: (tile_3(g), 0)),
                   pl.BlockSpec((None, K_A - 1, gw),
                                lambda g: (tile_1(g) // tiles_per_seq_blocks, 0, 0)),
                   pl.BlockSpec((None, K_B - 1, gw),
                                lambda g: (tile_1(g) // tiles_per_seq_blocks, 0, 0))]
        + [any_spec] * len(weights),
        out_shape=[jax.ShapeDtypeStruct((bp * seq, d), F32),
                   jax.ShapeDtypeStruct((bp, K_A - 1, gw), F32),
                   jax.ShapeDtypeStruct((bp, K_B - 1, gw), F32)]
        + [jax.ShapeDtypeStruct(w.shape, BF16) for w in weights],
        scratch_shapes=[pltpu.VMEM((n_lane, hist_a + tile, LANES), F32),
                        pltpu.VMEM((n_lane, hist_b + tile, LANES), F32),
                        pltpu.VMEM((tile, gw), F32),
                        pltpu.VMEM((tile, 2 * gw), BF16)]
        + [pltpu.VMEM(w.shape, BF16) for w in weights]
        + [pltpu.VMEM((WEIGHT_RING, WEIGHT_CHUNK_ROWS, max(w.shape[1] for w in weights)), F32),
           pltpu.SemaphoreType.DMA((WEIGHT_RING,)),
           pltpu.SemaphoreType.DMA((len(weights),))],
        compiler_params=pltpu.CompilerParams(
            dimension_semantics=("arbitrary",),
            vmem_limit_bytes=VMEM_LIMIT_BYTES),
        name="prompt_layer",
    )(x_rows, x_rows, *small, *weights)
    y_prompt = y_prompt.reshape(bp, seq, d)

    by_row = lambda a: jnp.swapaxes(a, 0, 1)
    n_seq = SAMPLE_SEQS
    rows = n_seq * dec_seq
    ra = _round_up(K_A - 1, SUBLANES) + dec_seq
    y_sample, sa, sb = pl.pallas_call(
        functools.partial(_sample_kernel, n_seq=n_seq, seq_len=dec_seq,
                          seq_group=SAMPLE_SEQ_GROUP, ff_chunk=FF_CHUNK),
        grid=(bs // n_seq,),
        in_specs=[pl.BlockSpec((rows, d), lambda i: (i, 0)),
                  pl.BlockSpec((n_seq, K_A - 1, gw), lambda i: (i, 0, 0)),
                  pl.BlockSpec((K_B - 1, n_seq, gw), lambda i: (0, i, 0))]
        + [_const_spec(p.shape) for p in small]
        + [pl.BlockSpec(memory_space=pl.ANY) for _ in weights_bf16],
        out_specs=[pl.BlockSpec((rows, d), lambda i: (i, 0)),
                   pl.BlockSpec((n_seq, K_A - 1, gw), lambda i: (i, 0, 0)),
                   pl.BlockSpec((K_B - 1, n_seq, gw), lambda i: (0, i, 0))],
        out_shape=[jax.ShapeDtypeStruct((bs * dec_seq, d), F32),
                   jax.ShapeDtypeStruct((bs, K_A - 1, gw), F32),
                   jax.ShapeDtypeStruct((K_B - 1, bs, gw), F32)],
        scratch_shapes=[pltpu.VMEM((n_lane, n_seq * ra, LANES), F32),
                        pltpu.VMEM((n_lane, rows, LANES), F32),
                        pltpu.VMEM((n_lane, rows, LANES), F32)]
        + [pltpu.VMEM(w.shape, BF16) for w in weights_bf16]
        + [pltpu.SemaphoreType.DMA((2 + 3 * pl.cdiv(w_gate.shape[-1], FF_CHUNK),))],
        compiler_params=pltpu.CompilerParams(
            dimension_semantics=("arbitrary",),
            vmem_limit_bytes=VMEM_LIMIT_BYTES),
        name="sample_layer",
    )(x_sample.reshape(bs * dec_seq, d), state_conv_a[0], by_row(state_conv_b[0]),
      *small, *weights_bf16)

    return (y_prompt, y_sample.reshape(bs, dec_seq, d), pa[None], pb[None],
            sa[None], by_row(sb)[None])
```

```python
import functools

import jax
import jax.numpy as jnp
from jax import lax
from jax.experimental import pallas as pl
from jax.experimental.pallas import tpu as pltpu

EPS = 1e-6
K_A = 3
K_B = 31
LANES = 128
SUBLANES = 8

F32 = jnp.float32
BF16 = jnp.bfloat16


def _rmsnorm(x, g):
    ms = jnp.mean(x * x, axis=-1, keepdims=True)
    return x * lax.rsqrt(ms + EPS) * g


def _layernorm(x, g, b):
    mu = jnp.mean(x, axis=-1, keepdims=True)
    xc = x - mu
    var = jnp.mean(xc * xc, axis=-1, keepdims=True)
    return xc * lax.rsqrt(var + EPS) * g + b


def _sigmoid(x):
    return 1.0 / (1.0 + jnp.exp(-x))


def _silu(x):
    return x * _sigmoid(x)


def _dot(a, b):
    return jnp.dot(a, b, preferred_element_type=F32)


def _conv_windows(ext_ref, j, starts, nrows, w_ref, n_taps):
    lanes = slice(j * LANES, (j + 1) * LANES)
    accs = [None] * len(starts)
    for k in range(n_taps):
        wk = w_ref[k, :, lanes]
        for i, s in enumerate(starts):
            term = wk * ext_ref[j, pl.ds(s + k, nrows), :]
            accs[i] = term if accs[i] is None else accs[i] + term
    return accs


def _after(lhs, pieces, never):
    if not pieces:
        return lhs
    rows = 2 * SUBLANES
    total = None
    for p in pieces:
        for r in range(0, p.shape[0], rows):
            for c in range(0, p.shape[1], LANES):
                slab = p[r:r + rows, c:c + LANES].astype(lhs.dtype)
                total = slab if total is None else total + slab
    head = jnp.where(never, total, lhs[0:rows, 0:LANES])
    top = jnp.concatenate([head, lhs[0:rows, LANES:]], axis=1)
    return jnp.concatenate([top, lhs[rows:, :]], axis=0)


def _mix(b_gate, conv_a, conv_b, bias_ref, ln_g_ref, ln_b_ref):
    y_b = _silu(_layernorm(conv_b + bias_ref[...], ln_g_ref[...], ln_b_ref[...]))
    return jnp.concatenate([b_gate * conv_a, y_b], axis=-1).astype(BF16)


def _ffn(x1, h2, w_gate_ref, w_up_ref, w_down_ref, ff_chunk, hide=None, never=None,
         before_chunk=None):
    d_ff = w_gate_ref.shape[1]
    acc = None
    pending = None
    for c, c0 in enumerate(range(0, d_ff, ff_chunk)):
        cols = slice(c0, min(c0 + ff_chunk, d_ff))
        if before_chunk and c in before_chunk:
            before_chunk[c]()
        lhs = _after(h2, (hide or {}).get(c), never)
        gate = _dot(lhs, w_gate_ref[:, cols])
        up = _dot(lhs, w_up_ref[:, cols])
        if pending is not None:
            part = _dot(pending[0], w_down_ref[pending[1], :])
            acc = part if acc is None else acc + part
        pending = ((_silu(gate) * up).astype(BF16), cols)
    part = _dot(pending[0], w_down_ref[pending[1], :])
    acc = part if acc is None else acc + part
    return x1 + acc


def _prompt_kernel(x1_ref, x3_ref, g_mix_ref, w_in_ref, wa_ref, wb_ref, bias_ref, ln_g_ref,
                   ln_b_ref, w_out_ref, g_ffn_ref, w_gate_ref, w_up_ref, w_down_ref, g_final_ref,
                   y_ref, new_a_ref, new_b_ref, exta_ref, extb_ref, bg_ref, ycat_ref,
                   *, t, tile, row_chunk, tiles_per_seq, ff_chunk, stage1=True, stage3=True):
    assert stage1 or stage3
    gw = wa_ref.shape[-1]
    n_lane = gw // LANES
    hist_a = exta_ref.shape[1] - tile
    hist_b = extb_ref.shape[1] - tile
    never = t < 0

    if stage3:
        x1 = x3_ref[...] + _dot(ycat_ref[...], w_out_ref[...])

    starts = list(range(0, tile, row_chunk))
    units = []
    for j in range(n_lane):
        a = _conv_windows(exta_ref, j, [hist_a - (K_A - 1) + s for s in starts], row_chunk,
                          wa_ref, K_A)
        b = _conv_windows(extb_ref, j, [hist_b - (K_B - 1) + s for s in starts], row_chunk,
                          wb_ref, K_B)
        units.append(a + b)
    conv_a = [jnp.concatenate(u[:len(starts)], axis=0) for u in units]
    conv_b = [jnp.concatenate(u[len(starts):], axis=0) for u in units]
    y_cat = _mix(bg_ref[...], jnp.concatenate(conv_a, axis=1), jnp.concatenate(conv_b, axis=1),
                 bias_ref, ln_g_ref, ln_b_ref)
    tail_a = exta_ref[:, tile:tile + hist_a, :]
    tail_b = extb_ref[:, tile:tile + hist_b, :]

    if stage1:
        h = _rmsnorm(x1_ref[...], g_mix_ref[...]).astype(BF16)
        in_proj = lambda k, wait=None: _dot(_after(h, wait, never),
                                            w_in_ref[:, k * gw:(k + 1) * gw])
        glu_val, glu_gate = in_proj(3), in_proj(4)
    if stage3:
        h2 = _rmsnorm(x1, g_ffn_ref[...]).astype(BF16)
        x2 = _ffn(x1, h2, w_gate_ref, w_up_ref, w_down_ref, ff_chunk,
                  hide={c: units[j] for c, j in CONV_UNIT_BEFORE_FFN_CHUNK.items()}, never=never)
        c_gate, v, b_gate = (in_proj(1), in_proj(2), in_proj(0)) if stage1 else (None,) * 3
    else:
        c_gate, v = in_proj(1, units[0]), in_proj(2, units[1])
        b_gate = in_proj(0, units[2] + units[3])
    if stage3:
        y_ref[...] = _rmsnorm(x2, g_final_ref[...])

    if stage1:
        seq_start = (t % tiles_per_seq) == 0
        cv = c_gate * v
        u_b = glu_val * _sigmoid(glu_gate)
        exta_ref[:, 0:hist_a, :] = jnp.where(seq_start, 0.0, tail_a)
        extb_ref[:, 0:hist_b, :] = jnp.where(seq_start, 0.0, tail_b)
        for j in range(n_lane):
            lanes = slice(j * LANES, (j + 1) * LANES)
            exta_ref[j, hist_a:hist_a + tile, :] = cv[:, lanes]
            extb_ref[j, hist_b:hist_b + tile, :] = u_b[:, lanes]
        bg_ref[...] = b_gate
        new_a_ref[...] = cv[tile - (K_A - 1):, :]
        new_b_ref[...] = u_b[tile - (K_B - 1):, :]

    ycat_ref[...] = y_cat


def _sample_mixer(x_ref, sa_ref, sb_ref, g_mix_ref, w_in_ref, wa_ref, wb_ref, bias_ref,
                  ln_g_ref, ln_b_ref, new_a_ref, new_b_ref, exta_ref, rows_ref, slabs_ref,
                  *, n_seq, seq_len, seq_group, after_glu=None):
    gw = wa_ref.shape[-1]
    n_lane = gw // LANES
    ra = exta_ref.shape[1] // n_seq
    off_a = ra - seq_len - (K_A - 1)
    n_hist = K_B - 1

    h = _rmsnorm(x_ref[...], g_mix_ref[...]).astype(BF16)
    in_proj = lambda k: _dot(h, w_in_ref[:, k * gw:(k + 1) * gw])
    u_b = in_proj(3) * _sigmoid(in_proj(4))
    if after_glu is not None:
        after_glu()
    for j in range(n_lane):
        rows_ref[j] = u_b[:, j * LANES:(j + 1) * LANES]
        for t in range(seq_len):
            slabs_ref[j, t * n_seq:(t + 1) * n_seq, :] = (
                rows_ref[j, pl.ds(t, n_seq, stride=seq_len), :])

    def slab(j, i):
        if i < n_hist:
            return sb_ref[i, :, j * LANES:(j + 1) * LANES]
        return slabs_ref[j, (i - n_hist) * n_seq:(i - n_hist + 1) * n_seq, :]

    conv_b = []
    for j in range(n_lane):
        lanes = slice(j * LANES, (j + 1) * LANES)
        accs = [None] * seq_len
        for i in range(n_hist + seq_len):
            xi = slab(j, i)
            for t in range(seq_len):
                k = i - t
                if 0 <= k < K_B:
                    term = wb_ref[k, :, lanes] * xi
                    accs[t] = term if accs[t] is None else accs[t] + term
        for t in range(seq_len):
            rows_ref[j, pl.ds(t, n_seq, stride=seq_len), :] = accs[t]
        conv_b.append(rows_ref[j])

    cv = in_proj(1) * in_proj(2)
    b_gate = in_proj(0)
    conv_a = []
    for j in range(n_lane):
        lanes = slice(j * LANES, (j + 1) * LANES)
        for s in range(n_seq):
            rows = slice(s * seq_len, (s + 1) * seq_len)
            exta_ref[j, s * ra + off_a:s * ra + off_a + K_A - 1, :] = sa_ref[s, :, lanes]
            exta_ref[j, (s + 1) * ra - seq_len:(s + 1) * ra, :] = cv[rows, lanes]
        a = []
        for s0 in range(0, n_seq, seq_group):
            group = range(s0, s0 + seq_group)
            a += _conv_windows(exta_ref, j, [s * ra + off_a for s in group], seq_len, wa_ref, K_A)
        conv_a.append(jnp.concatenate(a, axis=0))

    for j in range(n_lane):
        lanes = slice(j * LANES, (j + 1) * LANES)
        for s in range(n_seq):
            new_a_ref[s, :, lanes] = exta_ref[j, (s + 1) * ra - (K_A - 1):(s + 1) * ra, :]
        for r in range(n_hist):
            new_b_ref[r, :, lanes] = slab(j, seq_len + r)
    return _mix(b_gate, jnp.concatenate(conv_a, axis=1), jnp.concatenate(conv_b, axis=1),
                bias_ref, ln_g_ref, ln_b_ref)


def _sample_kernel(x_ref, sa_ref, sb_ref, g_mix_ref, wa_ref, wb_ref, bias_ref, ln_g_ref,
                   ln_b_ref, g_ffn_ref, g_final_ref, w_in_hbm, w_out_hbm, w_gate_hbm, w_up_hbm,
                   w_down_hbm, y_ref, new_a_ref, new_b_ref, exta_ref, rows_ref, slabs_ref,
                   w_in_ref, w_out_ref, w_gate_ref, w_up_ref, w_down_ref, sem_ref,
                   *, ff_chunk, **mixer):
    d_ff = w_gate_ref.shape[1]
    chunks = [slice(c0, min(c0 + ff_chunk, d_ff)) for c0 in range(0, d_ff, ff_chunk)]
    split = len(chunks) // 2
    glu = slice(3 * wa_ref.shape[-1], 5 * wa_ref.shape[-1])
    rest = slice(0, 3 * wa_ref.shape[-1])
    pieces = [(w_in_hbm.at[:, glu], w_in_ref.at[:, glu]),
              (w_in_hbm.at[:, rest], w_in_ref.at[:, rest]), (w_out_hbm, w_out_ref)]
    for cols in chunks:
        pieces += [(w_gate_hbm.at[:, cols], w_gate_ref.at[:, cols]),
                   (w_up_hbm.at[:, cols], w_up_ref.at[:, cols]),
                   (w_down_hbm.at[cols, :], w_down_ref.at[cols, :])]
    loads = [pltpu.make_async_copy(src, dst, sem_ref.at[k])
             for k, (src, dst) in enumerate(pieces)]
    early, late = loads[2:3 + 3 * split], loads[3 + 3 * split:]

    def wait_late():
        for load in late:
            load.wait()

    def step(first):
        if first:
            loads[0].start()
            loads[1].start()
            loads[0].wait()
            for load in early:
                load.start()
        y_cat = _sample_mixer(x_ref, sa_ref, sb_ref, g_mix_ref, w_in_ref, wa_ref, wb_ref,
                              bias_ref, ln_g_ref, ln_b_ref, new_a_ref, new_b_ref, exta_ref,
                              rows_ref, slabs_ref, after_glu=loads[1].wait if first else None,
                              **mixer)
        if first:
            for load in early:
                load.wait()
            for load in late:
                load.start()
        x1 = x_ref[...] + _dot(y_cat, w_out_ref[...])
        h2 = _rmsnorm(x1, g_ffn_ref[...]).astype(BF16)
        x2 = _ffn(x1, h2, w_gate_ref, w_up_ref, w_down_ref, ff_chunk,
                  before_chunk={split: wait_late} if first else None)
        y_ref[...] = _rmsnorm(x2, g_final_ref[...])

    is_first = pl.program_id(0) == 0
    pl.when(is_first)(lambda: step(True))
    pl.when(jnp.logical_not(is_first))(lambda: step(False))


N_PARAMS = 13
WEIGHT_SLOTS = (1, 7, 9, 10, 11)
CONVERT_ORDER = (0, 2, 3, 4, 1)
WEIGHT_CHUNK_ROWS = 256
WEIGHT_RING = 4


def _weight_pieces(w_hbm_refs, w_vmem_refs, chunk):
    return [(w_hbm_refs[k], w_vmem_refs[k], r0) for k in CONVERT_ORDER
            for r0 in range(0, w_hbm_refs[k].shape[0], chunk)]


def _convert_weights(pieces, first, last, stage_ref, sem_ref):
    ring, chunk = stage_ref.shape[0], stage_ref.shape[1]

    def fetch(p):
        src, _, r0 = pieces[p]
        return pltpu.make_async_copy(src.at[r0:r0 + chunk, :],
                                     stage_ref.at[p % ring, :, 0:src.shape[1]],
                                     sem_ref.at[p % ring])

    if first == 0:
        for p in range(min(ring - 1, len(pieces))):
            fetch(p).start()
    for p in range(first, last):
        src, dst, r0 = pieces[p]
        if p + ring - 1 < len(pieces):
            fetch(p + ring - 1).start()
        fetch(p).wait()
        dst[r0:r0 + chunk, :] = stage_ref[p % ring, :, 0:src.shape[1]].astype(BF16)


def _prompt_entry(*refs, n_steps, step):
    n_w = len(WEIGHT_SLOTS)
    x1_ref, x3_ref = refs[:2]
    params = list(refs[2:2 + N_PARAMS - n_w])
    w_hbm = refs[2 + N_PARAMS - n_w:2 + N_PARAMS]
    outs = refs[2 + N_PARAMS:5 + N_PARAMS]
    w_out_hbm = refs[5 + N_PARAMS:5 + N_PARAMS + n_w]
    scratch = refs[5 + N_PARAMS + n_w:]
    exta_ref, extb_ref, bg_ref, ycat_ref = scratch[:4]
    w_vmem = scratch[4:4 + n_w]
    stage_ref, sem_ref, out_sem_ref = scratch[4 + n_w:]
    g = pl.program_id(0)

    def copy_out(k):
        return pltpu.make_async_copy(w_vmem[k], w_out_hbm[k], out_sem_ref.at[k])

    pieces = _weight_pieces(w_hbm, w_vmem, stage_ref.shape[1])
    n_first = w_hbm[CONVERT_ORDER[0]].shape[0] // stage_ref.shape[1]
    n_second = n_first + stage_ref.shape[0] - 1
    convert = lambda first, last: _convert_weights(pieces, first, last, stage_ref, sem_ref)

    for slot, w in zip(WEIGHT_SLOTS, w_vmem):
        params.insert(slot, w)
    y_ref, new_a_ref, new_b_ref = outs
    tile = step["tile"]

    def run(sub, **stages):
        rows = pl.ds(sub * tile, tile)
        _prompt_kernel(x1_ref.at[rows], x3_ref.at[rows], *params, y_ref.at[rows], new_a_ref,
                       new_b_ref, exta_ref, extb_ref, bg_ref, ycat_ref, t=SUBSTEPS * g + sub,
                       **step, **stages)

    @pl.when(g == 0)
    def _():
        exta_ref[...] = jnp.zeros(exta_ref.shape, F32)
        extb_ref[...] = jnp.zeros(extb_ref.shape, F32)
        bg_ref[...] = jnp.zeros(bg_ref.shape, F32)
        convert(0, n_first)
        run(0, stage3=False)
        convert(n_first, n_second)
        run(1, stage3=False)

    @pl.when(g == 1)
    def _():
        convert(n_second, len(pieces))
        for k in range(n_w):
            copy_out(k).start()

    @pl.when(jnp.logical_and(g > 0, g < n_steps - 1))
    def _():
        run(0)
        run(1)

    @pl.when(g == n_steps - 1)
    def _():
        run(0, stage1=False)
        run(1, stage1=False)
        for k in range(n_w):
            copy_out(k).wait()


def _round_up(n, m):
    return -(-n // m) * m


def _const_spec(shape):
    zeros = (0,) * len(shape)
    return pl.BlockSpec(shape, lambda g: zeros, pipeline_mode=pl.Buffered(1))


PROMPT_TILE = 256
SUBSTEPS = 2
PROMPT_ROW_CHUNK = 64
CONV_UNIT_BEFORE_FFN_CHUNK = {1: 0, 2: 1, 3: 2, 4: 3}
FF_CHUNK = 512
SAMPLE_SEQS = 32
SAMPLE_SEQ_GROUP = 16
VMEM_LIMIT_BYTES = 56 * 1024 * 1024


def kernel(x_prompt, x_sample, state_conv_a, state_conv_b, g_mix, w_in, conv_a_w, conv_b_w,
           conv_b_bias, ln_b_g, ln_b_b, w_out, g_ffn, w_gate, w_up, w_down, g_final):
    depth = g_mix.shape[0]
    assert depth == 1, "single-layer trunk"
    bp, seq, d = x_prompt.shape
    bs, dec_seq, _ = x_sample.shape
    gw = conv_a_w.shape[-1]
    assert conv_a_w.shape[1] == K_A and conv_b_w.shape[1] == K_B
    assert dec_seq == SUBLANES and gw % LANES == 0
    assert seq % PROMPT_TILE == 0 and bs % SAMPLE_SEQS == 0

    row = lambda a: a.reshape(1, -1)
    taps = lambda w: jnp.swapaxes(w, 0, 1)
    small = (row(g_mix[0]), taps(conv_a_w), taps(conv_b_w), row(conv_b_bias[0]), row(ln_b_g[0]),
             row(ln_b_b[0]), row(g_ffn[0]), row(g_final))
    weights = (w_in[0], w_out[0], w_gate[0], w_up[0], w_down[0])
    assert len(small) + len(weights) == N_PARAMS
    assert all(w.shape[0] % WEIGHT_CHUNK_ROWS == 0 for w in weights)

    tile = PROMPT_TILE
    tiles_per_seq = seq // tile
    assert tiles_per_seq % SUBSTEPS == 0
    n_blocks = bp * tiles_per_seq // SUBSTEPS
    n_steps = n_blocks + 1
    block = SUBSTEPS * tile
    hist_a = _round_up(K_A - 1, SUBLANES)
    hist_b = _round_up(K_B - 1, SUBLANES)
    n_lane = gw // LANES
    tile_1 = lambda g: jnp.minimum(g, n_blocks - 1)
    tile_3 = lambda g: jnp.maximum(g - 1, 0)
    tiles_per_seq_blocks = tiles_per_seq // SUBSTEPS
    any_spec = pl.BlockSpec(memory_space=pl.ANY)
    x_rows = x_prompt.reshape(bp * seq, d)
    y_prompt, pa, pb, *weights_bf16 = pl.pallas_call(
        functools.partial(
            _prompt_entry, n_steps=n_steps,
            step=dict(tile=tile, row_chunk=PROMPT_ROW_CHUNK, tiles_per_seq=tiles_per_seq,
                      ff_chunk=FF_CHUNK)),
        grid=(n_steps,),
        in_specs=[pl.BlockSpec((block, d), lambda g: (tile_1(g), 0)),
                  pl.BlockSpec((block, d), lambda g: (tile_3(g), 0))]
        + [_const_spec(p.shape) for p in small] + [any_spec] * len(weights),
        out_specs=[pl.BlockSpec((block, d), lambda g: (tile_3(g), 0)),
                   pl.BlockSpec((None, K_A - 1, gw),
                                lambda g: (tile_1(g) // tiles_per_seq_blocks, 0, 0)),
                   pl.BlockSpec((None, K_B - 1, gw),
                                lambda g: (tile_1(g) // tiles_per_seq_blocks, 0, 0))]
        + [any_spec] * len(weights),
        out_shape=[jax.ShapeDtypeStruct((bp * seq, d), F32),
                   jax.ShapeDtypeStruct((bp, K_A - 1, gw), F32),
                   jax.ShapeDtypeStruct((bp, K_B - 1, gw), F32)]
        + [jax.ShapeDtypeStruct(w.shape, BF16) for w in weights],
        scratch_shapes=[pltpu.VMEM((n_lane, hist_a + tile, LANES), F32),
                        pltpu.VMEM((n_lane, hist_b + tile, LANES), F32),
                        pltpu.VMEM((tile, gw), F32),
                        pltpu.VMEM((tile, 2 * gw), BF16)]
        + [pltpu.VMEM(w.shape, BF16) for w in weights]
        + [pltpu.VMEM((WEIGHT_RING, WEIGHT_CHUNK_ROWS, max(w.shape[1] for w in weights)), F32),
           pltpu.SemaphoreType.DMA((WEIGHT_RING,)),
           pltpu.SemaphoreType.DMA((len(weights),))],
        compiler_params=pltpu.CompilerParams(
            dimension_semantics=("arbitrary",),
            vmem_limit_bytes=VMEM_LIMIT_BYTES),
        name="prompt_layer",
    )(x_rows, x_rows, *small, *weights)
    y_prompt = y_prompt.reshape(bp, seq, d)

    by_row = lambda a: jnp.swapaxes(a, 0, 1)
    n_seq = SAMPLE_SEQS
    rows = n_seq * dec_seq
    ra = _round_up(K_A - 1, SUBLANES) + dec_seq
    y_sample, sa, sb = pl.pallas_call(
        functools.partial(_sample_kernel, n_seq=n_seq, seq_len=dec_seq,
                          seq_group=SAMPLE_SEQ_GROUP, ff_chunk=FF_CHUNK),
        grid=(bs // n_seq,),
        in_specs=[pl.BlockSpec((rows, d), lambda i: (i, 0)),
                  pl.BlockSpec((n_seq, K_A - 1, gw), lambda i: (i, 0, 0)),
                  pl.BlockSpec((K_B - 1, n_seq, gw), lambda i: (0, i, 0))]
        + [_const_spec(p.shape) for p in small]
        + [pl.BlockSpec(memory_space=pl.ANY) for _ in weights_bf16],
        out_specs=[pl.BlockSpec((rows, d), lambda i: (i, 0)),
                   pl.BlockSpec((n_seq, K_A - 1, gw), lambda i: (i, 0, 0)),
                   pl.BlockSpec((K_B - 1, n_seq, gw), lambda i: (0, i, 0))],
        out_shape=[jax.ShapeDtypeStruct((bs * dec_seq, d), F32),
                   jax.ShapeDtypeStruct((bs, K_A - 1, gw), F32),
                   jax.ShapeDtypeStruct((K_B - 1, bs, gw), F32)],
        scratch_shapes=[pltpu.VMEM((n_lane, n_seq * ra, LANES), F32),
                        pltpu.VMEM((n_lane, rows, LANES), F32),
                        pltpu.VMEM((n_lane, rows, LANES), F32)]
        + [pltpu.VMEM(w.shape, BF16) for w in weights_bf16]
        + [pltpu.SemaphoreType.DMA((3 + 3 * pl.cdiv(w_gate.shape[-1], FF_CHUNK),))],
        compiler_params=pltpu.CompilerParams(
            dimension_semantics=("arbitrary",),
            vmem_limit_bytes=VMEM_LIMIT_BYTES),
        name="sample_layer",
    )(x_sample.reshape(bs * dec_seq, d), state_conv_a[0], by_row(state_conv_b[0]),
      *small, *weights_bf16)

    return (y_prompt, y_sample.reshape(bs, dec_seq, d), pa[None], pb[None],
            sa[None], by_row(sb)[None])
```

```python
import functools

import jax
import jax.numpy as jnp
from jax import lax
from jax.experimental import pallas as pl
from jax.experimental.pallas import tpu as pltpu

EPS = 1e-6
K_A = 3
K_B = 31
LANES = 128
SUBLANES = 8

F32 = jnp.float32
BF16 = jnp.bfloat16


def _rmsnorm(x, g):
    ms = jnp.mean(x * x, axis=-1, keepdims=True)
    return x * lax.rsqrt(ms + EPS) * g


def _layernorm(x, g, b):
    mu = jnp.mean(x, axis=-1, keepdims=True)
    xc = x - mu
    var = jnp.mean(xc * xc, axis=-1, keepdims=True)
    return xc * lax.rsqrt(var + EPS) * g + b


def _sigmoid(x):
    return 1.0 / (1.0 + jnp.exp(-x))


def _silu(x):
    return x * _sigmoid(x)


def _dot(a, b):
    return jnp.dot(a, b, preferred_element_type=F32)


def _conv_windows(ext_ref, j, starts, nrows, w_ref, n_taps):
    lanes = slice(j * LANES, (j + 1) * LANES)
    accs = [None] * len(starts)
    for k in range(n_taps):
        wk = w_ref[k, :, lanes]
        for i, s in enumerate(starts):
            term = wk * ext_ref[j, pl.ds(s + k, nrows), :]
            accs[i] = term if accs[i] is None else accs[i] + term
    return accs


def _after(lhs, pieces, never):
    if not pieces:
        return lhs
    rows = 2 * SUBLANES
    total = None
    for p in pieces:
        for r in range(0, p.shape[0], rows):
            for c in range(0, p.shape[1], LANES):
                slab = p[r:r + rows, c:c + LANES].astype(lhs.dtype)
                total = slab if total is None else total + slab
    head = jnp.where(never, total, lhs[0:rows, 0:LANES])
    top = jnp.concatenate([head, lhs[0:rows, LANES:]], axis=1)
    return jnp.concatenate([top, lhs[rows:, :]], axis=0)


def _mix(b_gate, conv_a, conv_b, bias_ref, ln_g_ref, ln_b_ref):
    y_b = _silu(_layernorm(conv_b + bias_ref[...], ln_g_ref[...], ln_b_ref[...]))
    return jnp.concatenate([b_gate * conv_a, y_b], axis=-1).astype(BF16)


def _ffn(x1, h2, w_gate_ref, w_up_ref, w_down_ref, ff_chunk, hide=None, never=None,
         before_chunk=None):
    d_ff = w_gate_ref.shape[1]
    acc = None
    pending = None
    for c, c0 in enumerate(range(0, d_ff, ff_chunk)):
        cols = slice(c0, min(c0 + ff_chunk, d_ff))
        if before_chunk and c in before_chunk:
            before_chunk[c]()
        lhs = _after(h2, (hide or {}).get(c), never)
        gate = _dot(lhs, w_gate_ref[:, cols])
        up = _dot(lhs, w_up_ref[:, cols])
        if pending is not None:
            part = _dot(pending[0], w_down_ref[pending[1], :])
            acc = part if acc is None else acc + part
        pending = ((_silu(gate) * up).astype(BF16), cols)
    part = _dot(pending[0], w_down_ref[pending[1], :])
    acc = part if acc is None else acc + part
    return x1 + acc


def _prompt_kernel(x1_ref, x3_ref, g_mix_ref, w_in_ref, wa_ref, wb_ref, bias_ref, ln_g_ref,
                   ln_b_ref, w_out_ref, g_ffn_ref, w_gate_ref, w_up_ref, w_down_ref, g_final_ref,
                   y_ref, new_a_ref, new_b_ref, exta_ref, extb_ref, bg_ref, ycat_ref,
                   *, t, tile, row_chunk, tiles_per_seq, ff_chunk, stage1=True, stage3=True):
    assert stage1 or stage3
    gw = wa_ref.shape[-1]
    n_lane = gw // LANES
    hist_a = exta_ref.shape[1] - tile
    hist_b = extb_ref.shape[1] - tile
    never = t < 0

    if stage3:
        x1 = x3_ref[...] + _dot(ycat_ref[...], w_out_ref[...])

    starts = list(range(0, tile, row_chunk))
    units = []
    for j in range(n_lane):
        a = _conv_windows(exta_ref, j, [hist_a - (K_A - 1) + s for s in starts], row_chunk,
                          wa_ref, K_A)
        b = _conv_windows(extb_ref, j, [hist_b - (K_B - 1) + s for s in starts], row_chunk,
                          wb_ref, K_B)
        units.append(a + b)
    conv_a = [jnp.concatenate(u[:len(starts)], axis=0) for u in units]
    conv_b = [jnp.concatenate(u[len(starts):], axis=0) for u in units]
    y_cat = _mix(bg_ref[...], jnp.concatenate(conv_a, axis=1), jnp.concatenate(conv_b, axis=1),
                 bias_ref, ln_g_ref, ln_b_ref)
    tail_a = exta_ref[:, tile:tile + hist_a, :]
    tail_b = extb_ref[:, tile:tile + hist_b, :]

    if stage1:
        h = _rmsnorm(x1_ref[...], g_mix_ref[...]).astype(BF16)
        in_proj = lambda k, wait=None: _dot(_after(h, wait, never),
                                            w_in_ref[:, k * gw:(k + 1) * gw])
        glu_val, glu_gate = in_proj(3), in_proj(4)
    if stage3:
        h2 = _rmsnorm(x1, g_ffn_ref[...]).astype(BF16)
        x2 = _ffn(x1, h2, w_gate_ref, w_up_ref, w_down_ref, ff_chunk,
                  hide={c: units[j] for c, j in CONV_UNIT_BEFORE_FFN_CHUNK.items()}, never=never)
        c_gate, v, b_gate = (in_proj(1), in_proj(2), in_proj(0)) if stage1 else (None,) * 3
    else:
        c_gate, v = in_proj(1, units[0]), in_proj(2, units[1])
        b_gate = in_proj(0, units[2] + units[3])
    if stage3:
        y_ref[...] = _rmsnorm(x2, g_final_ref[...])

    if stage1:
        seq_start = (t % tiles_per_seq) == 0
        cv = c_gate * v
        u_b = glu_val * _sigmoid(glu_gate)
        exta_ref[:, 0:hist_a, :] = jnp.where(seq_start, 0.0, tail_a)
        extb_ref[:, 0:hist_b, :] = jnp.where(seq_start, 0.0, tail_b)
        for j in range(n_lane):
            lanes = slice(j * LANES, (j + 1) * LANES)
            exta_ref[j, hist_a:hist_a + tile, :] = cv[:, lanes]
            extb_ref[j, hist_b:hist_b + tile, :] = u_b[:, lanes]
        bg_ref[...] = b_gate
        new_a_ref[...] = cv[tile - (K_A - 1):, :]
        new_b_ref[...] = u_b[tile - (K_B - 1):, :]

    ycat_ref[...] = y_cat


def _sample_mixer(x_ref, sa_ref, sb_ref, g_mix_ref, w_in_ref, wa_ref, wb_ref, bias_ref,
                  ln_g_ref, ln_b_ref, new_a_ref, new_b_ref, exta_ref, rows_ref, slabs_ref,
                  *, n_seq, seq_len, seq_group, after_glu=None):
    gw = wa_ref.shape[-1]
    n_lane = gw // LANES
    ra = exta_ref.shape[1] // n_seq
    off_a = ra - seq_len - (K_A - 1)
    n_hist = K_B - 1

    h = _rmsnorm(x_ref[...], g_mix_ref[...]).astype(BF16)
    in_proj = lambda k: _dot(h, w_in_ref[:, k * gw:(k + 1) * gw])
    u_b = in_proj(3) * _sigmoid(in_proj(4))
    if after_glu is not None:
        after_glu()
    for j in range(n_lane):
        rows_ref[j] = u_b[:, j * LANES:(j + 1) * LANES]
        for t in range(seq_len):
            slabs_ref[j, t * n_seq:(t + 1) * n_seq, :] = (
                rows_ref[j, pl.ds(t, n_seq, stride=seq_len), :])

    def slab(j, i):
        if i < n_hist:
            return sb_ref[i, :, j * LANES:(j + 1) * LANES]
        return slabs_ref[j, (i - n_hist) * n_seq:(i - n_hist + 1) * n_seq, :]

    conv_b = []
    for j in range(n_lane):
        lanes = slice(j * LANES, (j + 1) * LANES)
        accs = [None] * seq_len
        for i in range(n_hist + seq_len):
            xi = slab(j, i)
            for t in range(seq_len):
                k = i - t
                if 0 <= k < K_B:
                    term = wb_ref[k, :, lanes] * xi
                    accs[t] = term if accs[t] is None else accs[t] + term
        for t in range(seq_len):
            rows_ref[j, pl.ds(t, n_seq, stride=seq_len), :] = accs[t]
        conv_b.append(rows_ref[j])

    cv = in_proj(1) * in_proj(2)
    b_gate = in_proj(0)
    conv_a = []
    for j in range(n_lane):
        lanes = slice(j * LANES, (j + 1) * LANES)
        for s in range(n_seq):
            rows = slice(s * seq_len, (s + 1) * seq_len)
            exta_ref[j, s * ra + off_a:s * ra + off_a + K_A - 1, :] = sa_ref[s, :, lanes]
            exta_ref[j, (s + 1) * ra - seq_len:(s + 1) * ra, :] = cv[rows, lanes]
        a = []
        for s0 in range(0, n_seq, seq_group):
            group = range(s0, s0 + seq_group)
            a += _conv_windows(exta_ref, j, [s * ra + off_a for s in group], seq_len, wa_ref, K_A)
        conv_a.append(jnp.concatenate(a, axis=0))

    for j in range(n_lane):
        lanes = slice(j * LANES, (j + 1) * LANES)
        for s in range(n_seq):
            new_a_ref[s, :, lanes] = exta_ref[j, (s + 1) * ra - (K_A - 1):(s + 1) * ra, :]
        for r in range(n_hist):
            new_b_ref[r, :, lanes] = slab(j, seq_len + r)
    return _mix(b_gate, jnp.concatenate(conv_a, axis=1), jnp.concatenate(conv_b, axis=1),
                bias_ref, ln_g_ref, ln_b_ref)


def _sample_kernel(x_ref, sa_ref, sb_ref, g_mix_ref, wa_ref, wb_ref, bias_ref, ln_g_ref,
                   ln_b_ref, g_ffn_ref, g_final_ref, w_in_hbm, w_out_hbm, w_gate_hbm, w_up_hbm,
                   w_down_hbm, y_ref, new_a_ref, new_b_ref, exta_ref, rows_ref, slabs_ref,
                   w_in_ref, w_out_ref, w_gate_ref, w_up_ref, w_down_ref, sem_ref,
                   *, ff_chunk, **mixer):
    d_ff = w_gate_ref.shape[1]
    chunks = [slice(c0, min(c0 + ff_chunk, d_ff)) for c0 in range(0, d_ff, ff_chunk)]
    split = len(chunks) // 2
    glu = slice(3 * wa_ref.shape[-1], 5 * wa_ref.shape[-1])
    rest = slice(0, 3 * wa_ref.shape[-1])
    pieces = [(w_in_hbm.at[:, glu], w_in_ref.at[:, glu]),
              (w_in_hbm.at[:, rest], w_in_ref.at[:, rest]), (w_out_hbm, w_out_ref)]
    for cols in chunks:
        pieces += [(w_gate_hbm.at[:, cols], w_gate_ref.at[:, cols]),
                   (w_up_hbm.at[:, cols], w_up_ref.at[:, cols]),
                   (w_down_hbm.at[cols, :], w_down_ref.at[cols, :])]
    loads = [pltpu.make_async_copy(src, dst, sem_ref.at[k])
             for k, (src, dst) in enumerate(pieces)]
    early, late = loads[2:3 + 3 * split], loads[3 + 3 * split:]

    def wait_late():
        for load in late:
            load.wait()

    def step(first):
        if first:
            loads[0].start()
            loads[1].start()
            loads[0].wait()
            for load in early:
                load.start()
        y_cat = _sample_mixer(x_ref, sa_ref, sb_ref, g_mix_ref, w_in_ref, wa_ref, wb_ref,
                              bias_ref, ln_g_ref, ln_b_ref, new_a_ref, new_b_ref, exta_ref,
                              rows_ref, slabs_ref, after_glu=loads[1].wait if first else None,
                              **mixer)
        if first:
            for load in early:
                load.wait()
            for load in late:
                load.start()
        x1 = x_ref[...] + _dot(y_cat, w_out_ref[...])
        h2 = _rmsnorm(x1, g_ffn_ref[...]).astype(BF16)
        x2 = _ffn(x1, h2, w_gate_ref, w_up_ref, w_down_ref, ff_chunk,
                  before_chunk={split: wait_late} if first else None)
        y_ref[...] = _rmsnorm(x2, g_final_ref[...])

    is_first = pl.program_id(0) == 0
    pl.when(is_first)(lambda: step(True))
    pl.when(jnp.logical_not(is_first))(lambda: step(False))


N_PARAMS = 13
WEIGHT_SLOTS = (1, 7, 9, 10, 11)
CONVERT_ORDER = (0, 2, 3, 4, 1)
WEIGHT_CHUNK_ROWS = 256
WEIGHT_RING = 4


def _weight_pieces(w_hbm_refs, w_vmem_refs, chunk):
    return [(w_hbm_refs[k], w_vmem_refs[k], r0) for k in CONVERT_ORDER
            for r0 in range(0, w_hbm_refs[k].shape[0], chunk)]


def _convert_weights(pieces, first, last, stage_ref, sem_ref):
    ring, chunk = stage_ref.shape[0], stage_ref.shape[1]

    def fetch(p):
        src, _, r0 = pieces[p]
        return pltpu.make_async_copy(src.at[r0:r0 + chunk, :],
                                     stage_ref.at[p % ring, :, 0:src.shape[1]],
                                     sem_ref.at[p % ring])

    if first == 0:
        for p in range(min(ring - 1, len(pieces))):
            fetch(p).start()
    for p in range(first, last):
        src, dst, r0 = pieces[p]
        if p + ring - 1 < len(pieces):
            fetch(p + ring - 1).start()
        fetch(p).wait()
        dst[r0:r0 + chunk, :] = stage_ref[p % ring, :, 0:src.shape[1]].astype(BF16)


def _prompt_entry(*refs, n_steps, step):
    n_w = len(WEIGHT_SLOTS)
    x1_ref, x3_ref = refs[:2]
    params = list(refs[2:2 + N_PARAMS - n_w])
    w_hbm = refs[2 + N_PARAMS - n_w:2 + N_PARAMS]
    outs = refs[2 + N_PARAMS:5 + N_PARAMS]
    w_out_hbm = refs[5 + N_PARAMS:5 + N_PARAMS + n_w]
    scratch = refs[5 + N_PARAMS + n_w:]
    exta_ref, extb_ref, bg_ref, ycat_ref = scratch[:4]
    w_vmem = scratch[4:4 + n_w]
    stage_ref, sem_ref, out_sem_ref = scratch[4 + n_w:]
    g = pl.program_id(0)

    def copy_out(k):
        return pltpu.make_async_copy(w_vmem[k], w_out_hbm[k], out_sem_ref.at[k])

    pieces = _weight_pieces(w_hbm, w_vmem, stage_ref.shape[1])
    n_first = w_hbm[CONVERT_ORDER[0]].shape[0] // stage_ref.shape[1]
    n_second = n_first + stage_ref.shape[0] - 1
    convert = lambda first, last: _convert_weights(pieces, first, last, stage_ref, sem_ref)

    for slot, w in zip(WEIGHT_SLOTS, w_vmem):
        params.insert(slot, w)
    y_ref, new_a_ref, new_b_ref = outs
    tile = step["tile"]

    def run(sub, **stages):
        rows = pl.ds(sub * tile, tile)
        _prompt_kernel(x1_ref.at[rows], x3_ref.at[rows], *params, y_ref.at[rows], new_a_ref,
                       new_b_ref, exta_ref, extb_ref, bg_ref, ycat_ref, t=SUBSTEPS * g + sub,
                       **step, **stages)

    @pl.when(g == 0)
    def _():
        exta_ref[...] = jnp.zeros(exta_ref.shape, F32)
        extb_ref[...] = jnp.zeros(extb_ref.shape, F32)
        bg_ref[...] = jnp.zeros(bg_ref.shape, F32)
        convert(0, n_first)
        run(0, stage3=False)
        convert(n_first, n_second)
        run(1, stage3=False)

    @pl.when(g == 1)
    def _():
        convert(n_second, len(pieces))
        for k in range(n_w):
            copy_out(k).start()

    @pl.when(jnp.logical_and(g > 0, g < n_steps - 1))
    def _():
        run(0)
        run(1)

    @pl.when(g == n_steps - 1)
    def _():
        run(0, stage1=False)
        run(1, stage1=False)
        for k in range(n_w):
            copy_out(k).wait()


def _round_up(n, m):
    return -(-n // m) * m


def _const_spec(shape):
    zeros = (0,) * len(shape)
    return pl.BlockSpec(shape, lambda g: zeros, pipeline_mode=pl.Buffered(1))


PROMPT_TILE = 256
SUBSTEPS = 2
PROMPT_ROW_CHUNK = 64
CONV_UNIT_BEFORE_FFN_CHUNK = {1: 0, 2: 1, 3: 2, 4: 3}
FF_CHUNK = 512
SAMPLE_FF_CHUNK = 256
SAMPLE_SEQS = 32
SAMPLE_SEQ_GROUP = 16
VMEM_LIMIT_BYTES = 56 * 1024 * 1024


def kernel(x_prompt, x_sample, state_conv_a, state_conv_b, g_mix, w_in, conv_a_w, conv_b_w,
           conv_b_bias, ln_b_g, ln_b_b, w_out, g_ffn, w_gate, w_up, w_down, g_final):
    depth = g_mix.shape[0]
    assert depth == 1, "single-layer trunk"
    bp, seq, d = x_prompt.shape
    bs, dec_seq, _ = x_sample.shape
    gw = conv_a_w.shape[-1]
    assert conv_a_w.shape[1] == K_A and conv_b_w.shape[1] == K_B
    assert dec_seq == SUBLANES and gw % LANES == 0
    assert seq % PROMPT_TILE == 0 and bs % SAMPLE_SEQS == 0

    row = lambda a: a.reshape(1, -1)
    taps = lambda w: jnp.swapaxes(w, 0, 1)
    small = (row(g_mix[0]), taps(conv_a_w), taps(conv_b_w), row(conv_b_bias[0]), row(ln_b_g[0]),
             row(ln_b_b[0]), row(g_ffn[0]), row(g_final))
    weights = (w_in[0], w_out[0], w_gate[0], w_up[0], w_down[0])
    assert len(small) + len(weights) == N_PARAMS
    assert all(w.shape[0] % WEIGHT_CHUNK_ROWS == 0 for w in weights)

    tile = PROMPT_TILE
    tiles_per_seq = seq // tile
    assert tiles_per_seq % SUBSTEPS == 0
    n_blocks = bp * tiles_per_seq // SUBSTEPS
    n_steps = n_blocks + 1
    block = SUBSTEPS * tile
    hist_a = _round_up(K_A - 1, SUBLANES)
    hist_b = _round_up(K_B - 1, SUBLANES)
    n_lane = gw // LANES
    tile_1 = lambda g: jnp.minimum(g, n_blocks - 1)
    tile_3 = lambda g: jnp.maximum(g - 1, 0)
    tiles_per_seq_blocks = tiles_per_seq // SUBSTEPS
    any_spec = pl.BlockSpec(memory_space=pl.ANY)
    x_rows = x_prompt.reshape(bp * seq, d)
    y_prompt, pa, pb, *weights_bf16 = pl.pallas_call(
        functools.partial(
            _prompt_entry, n_steps=n_steps,
            step=dict(tile=tile, row_chunk=PROMPT_ROW_CHUNK, tiles_per_seq=tiles_per_seq,
                      ff_chunk=FF_CHUNK)),
        grid=(n_steps,),
        in_specs=[pl.BlockSpec((block, d), lambda g: (tile_1(g), 0)),
                  pl.BlockSpec((block, d), lambda g: (tile_3(g), 0))]
        + [_const_spec(p.shape) for p in small] + [any_spec] * len(weights),
        out_specs=[pl.BlockSpec((block, d), lambda g: (tile_3(g), 0)),
                   pl.BlockSpec((None, K_A - 1, gw),
                                lambda g: (tile_1(g) // tiles_per_seq_blocks, 0, 0)),
                   pl.BlockSpec((None, K_B - 1, gw),
                                lambda g: (tile_1(g) // tiles_per_seq_blocks, 0, 0))]
        + [any_spec] * len(weights),
        out_shape=[jax.ShapeDtypeStruct((bp * seq, d), F32),
                   jax.ShapeDtypeStruct((bp, K_A - 1, gw), F32),
                   jax.ShapeDtypeStruct((bp, K_B - 1, gw), F32)]
        + [jax.ShapeDtypeStruct(w.shape, BF16) for w in weights],
        scratch_shapes=[pltpu.VMEM((n_lane, hist_a + tile, LANES), F32),
                        pltpu.VMEM((n_lane, hist_b + tile, LANES), F32),
                        pltpu.VMEM((tile, gw), F32),
                        pltpu.VMEM((tile, 2 * gw), BF16)]
        + [pltpu.VMEM(w.shape, BF16) for w in weights]
        + [pltpu.VMEM((WEIGHT_RING, WEIGHT_CHUNK_ROWS, max(w.shape[1] for w in weights)), F32),
           pltpu.SemaphoreType.DMA((WEIGHT_RING,)),
           pltpu.SemaphoreType.DMA((len(weights),))],
        compiler_params=pltpu.CompilerParams(
            dimension_semantics=("arbitrary",),
            vmem_limit_bytes=VMEM_LIMIT_BYTES),
        name="prompt_layer",
    )(x_rows, x_rows, *small, *weights)
    y_prompt = y_prompt.reshape(bp, seq, d)

    by_row = lambda a: jnp.swapaxes(a, 0, 1)
    n_seq = SAMPLE_SEQS
    rows = n_seq * dec_seq
    ra = _round_up(K_A - 1, SUBLANES) + dec_seq
    y_sample, sa, sb = pl.pallas_call(
        functools.partial(_sample_kernel, n_seq=n_seq, seq_len=dec_seq,
                          seq_group=SAMPLE_SEQ_GROUP, ff_chunk=SAMPLE_FF_CHUNK),
        grid=(bs // n_seq,),
        in_specs=[pl.BlockSpec((rows, d), lambda i: (i, 0)),
                  pl.BlockSpec((n_seq, K_A - 1, gw), lambda i: (i, 0, 0)),
                  pl.BlockSpec((K_B - 1, n_seq, gw), lambda i: (0, i, 0))]
        + [_const_spec(p.shape) for p in small]
        + [pl.BlockSpec(memory_space=pl.ANY) for _ in weights_bf16],
        out_specs=[pl.BlockSpec((rows, d), lambda i: (i, 0)),
                   pl.BlockSpec((n_seq, K_A - 1, gw), lambda i: (i, 0, 0)),
                   pl.BlockSpec((K_B - 1, n_seq, gw), lambda i: (0, i, 0))],
        out_shape=[jax.ShapeDtypeStruct((bs * dec_seq, d), F32),
                   jax.ShapeDtypeStruct((bs, K_A - 1, gw), F32),
                   jax.ShapeDtypeStruct((K_B - 1, bs, gw), F32)],
        scratch_shapes=[pltpu.VMEM((n_lane, n_seq * ra, LANES), F32),
                        pltpu.VMEM((n_lane, rows, LANES), F32),
                        pltpu.VMEM((n_lane, rows, LANES), F32)]
        + [pltpu.VMEM(w.shape, BF16) for w in weights_bf16]
        + [pltpu.SemaphoreType.DMA((3 + 3 * pl.cdiv(w_gate.shape[-1], SAMPLE_FF_CHUNK),))],
        compiler_params=pltpu.CompilerParams(
            dimension_semantics=("arbitrary",),
            vmem_limit_bytes=VMEM_LIMIT_BYTES),
        name="sample_layer",
    )(x_sample.reshape(bs * dec_seq, d), state_conv_a[0], by_row(state_conv_b[0]),
      *small, *weights_bf16)

    return (y_prompt, y_sample.reshape(bs, dec_seq, d), pa[None], pb[None],
            sa[None], by_row(sb)[None])
```

```python
import functools

import jax
import jax.numpy as jnp
from jax import lax
from jax.experimental import pallas as pl
from jax.experimental.pallas import tpu as pltpu

EPS = 1e-6
K_A = 3
K_B = 31
LANES = 128
SUBLANES = 8

F32 = jnp.float32
BF16 = jnp.bfloat16


def _rmsnorm(x, g):
    ms = jnp.mean(x * x, axis=-1, keepdims=True)
    return x * lax.rsqrt(ms + EPS) * g


def _layernorm(x, g, b):
    mu = jnp.mean(x, axis=-1, keepdims=True)
    xc = x - mu
    var = jnp.mean(xc * xc, axis=-1, keepdims=True)
    return xc * lax.rsqrt(var + EPS) * g + b


def _sigmoid(x):
    return 1.0 / (1.0 + jnp.exp(-x))


def _silu(x):
    return x * _sigmoid(x)


def _dot(a, b):
    return jnp.dot(a, b, preferred_element_type=F32)


def _conv_windows(ext_ref, j, starts, nrows, w_ref, n_taps):
    lanes = slice(j * LANES, (j + 1) * LANES)
    accs = [None] * len(starts)
    for k in range(n_taps):
        wk = w_ref[k, :, lanes]
        for i, s in enumerate(starts):
            term = wk * ext_ref[j, pl.ds(s + k, nrows), :]
            accs[i] = term if accs[i] is None else accs[i] + term
    return accs


def _after(lhs, pieces, never):
    if not pieces:
        return lhs
    rows = 2 * SUBLANES
    total = None
    for p in pieces:
        for r in range(0, p.shape[0], rows):
            for c in range(0, p.shape[1], LANES):
                slab = p[r:r + rows, c:c + LANES].astype(lhs.dtype)
                total = slab if total is None else total + slab
    head = jnp.where(never, total, lhs[0:rows, 0:LANES])
    top = jnp.concatenate([head, lhs[0:rows, LANES:]], axis=1)
    return jnp.concatenate([top, lhs[rows:, :]], axis=0)


def _mix(b_gate, conv_a, conv_b, bias_ref, ln_g_ref, ln_b_ref):
    y_b = _silu(_layernorm(conv_b + bias_ref[...], ln_g_ref[...], ln_b_ref[...]))
    return jnp.concatenate([b_gate * conv_a, y_b], axis=-1).astype(BF16)


def _ffn(x1, h2, w_gate_ref, w_up_ref, w_down_ref, ff_chunk, hide=None, never=None):
    d_ff = w_gate_ref.shape[1]
    acc = None
    pending = None
    for c, c0 in enumerate(range(0, d_ff, ff_chunk)):
        cols = slice(c0, min(c0 + ff_chunk, d_ff))
        lhs = _after(h2, (hide or {}).get(c), never)
        gate = _dot(lhs, w_gate_ref[:, cols])
        up = _dot(lhs, w_up_ref[:, cols])
        if pending is not None:
            part = _dot(pending[0], w_down_ref[pending[1], :])
            acc = part if acc is None else acc + part
        pending = ((_silu(gate) * up).astype(BF16), cols)
    part = _dot(pending[0], w_down_ref[pending[1], :])
    acc = part if acc is None else acc + part
    return x1 + acc


def _prompt_kernel(x1_ref, x3_ref, g_mix_ref, w_in_ref, wa_ref, wb_ref, bias_ref, ln_g_ref,
                   ln_b_ref, w_out_ref, g_ffn_ref, w_gate_ref, w_up_ref, w_down_ref, g_final_ref,
                   y_ref, new_a_ref, new_b_ref, exta_ref, extb_ref, bg_ref, ycat_ref,
                   *, t, tile, row_chunk, tiles_per_seq, ff_chunk, stage1=True, stage3=True):
    assert stage1 or stage3
    gw = wa_ref.shape[-1]
    n_lane = gw // LANES
    hist_a = exta_ref.shape[1] - tile
    hist_b = extb_ref.shape[1] - tile
    never = t < 0

    if stage3:
        x1 = x3_ref[...] + _dot(ycat_ref[...], w_out_ref[...])

    starts = list(range(0, tile, row_chunk))
    units = []
    for j in range(n_lane):
        a = _conv_windows(exta_ref, j, [hist_a - (K_A - 1) + s for s in starts], row_chunk,
                          wa_ref, K_A)
        b = _conv_windows(extb_ref, j, [hist_b - (K_B - 1) + s for s in starts], row_chunk,
                          wb_ref, K_B)
        units.append(a + b)
    conv_a = [jnp.concatenate(u[:len(starts)], axis=0) for u in units]
    conv_b = [jnp.concatenate(u[len(starts):], axis=0) for u in units]
    y_cat = _mix(bg_ref[...], jnp.concatenate(conv_a, axis=1), jnp.concatenate(conv_b, axis=1),
                 bias_ref, ln_g_ref, ln_b_ref)
    tail_a = exta_ref[:, tile:tile + hist_a, :]
    tail_b = extb_ref[:, tile:tile + hist_b, :]

    if stage1:
        h = _rmsnorm(x1_ref[...], g_mix_ref[...]).astype(BF16)
        in_proj = lambda k, wait=None: _dot(_after(h, wait, never),
                                            w_in_ref[:, k * gw:(k + 1) * gw])
        glu_val, glu_gate = in_proj(3), in_proj(4)
    if stage3:
        h2 = _rmsnorm(x1, g_ffn_ref[...]).astype(BF16)
        x2 = _ffn(x1, h2, w_gate_ref, w_up_ref, w_down_ref, ff_chunk,
                  hide={c: units[j] for c, j in CONV_UNIT_BEFORE_FFN_CHUNK.items()}, never=never)
        c_gate, v, b_gate = (in_proj(1), in_proj(2), in_proj(0)) if stage1 else (None,) * 3
    else:
        c_gate, v = in_proj(1, units[0]), in_proj(2, units[1])
        b_gate = in_proj(0, units[2] + units[3])
    if stage3:
        y_ref[...] = _rmsnorm(x2, g_final_ref[...])

    if stage1:
        seq_start = (t % tiles_per_seq) == 0
        cv = c_gate * v
        u_b = glu_val * _sigmoid(glu_gate)
        exta_ref[:, 0:hist_a, :] = jnp.where(seq_start, 0.0, tail_a)
        extb_ref[:, 0:hist_b, :] = jnp.where(seq_start, 0.0, tail_b)
        for j in range(n_lane):
            lanes = slice(j * LANES, (j + 1) * LANES)
            exta_ref[j, hist_a:hist_a + tile, :] = cv[:, lanes]
            extb_ref[j, hist_b:hist_b + tile, :] = u_b[:, lanes]
        bg_ref[...] = b_gate
        new_a_ref[...] = cv[tile - (K_A - 1):, :]
        new_b_ref[...] = u_b[tile - (K_B - 1):, :]

    ycat_ref[...] = y_cat


def _sample_mixer(x_ref, sa_ref, sb_ref, g_mix_ref, w_in_ref, wa_ref, wb_ref, bias_ref,
                  ln_g_ref, ln_b_ref, new_a_ref, new_b_ref, exta_ref, rows_ref, slabs_ref,
                  *, n_seq, seq_len, seq_group):
    gw = wa_ref.shape[-1]
    n_lane = gw // LANES
    ra = exta_ref.shape[1] // n_seq
    off_a = ra - seq_len - (K_A - 1)
    n_hist = K_B - 1

    h = _rmsnorm(x_ref[...], g_mix_ref[...]).astype(BF16)
    in_proj = lambda k: _dot(h, w_in_ref[:, k * gw:(k + 1) * gw])
    u_b = in_proj(3) * _sigmoid(in_proj(4))
    for j in range(n_lane):
        rows_ref[j] = u_b[:, j * LANES:(j + 1) * LANES]
        for t in range(seq_len):
            slabs_ref[j, t * n_seq:(t + 1) * n_seq, :] = (
                rows_ref[j, pl.ds(t, n_seq, stride=seq_len), :])

    def slab(j, i):
        if i < n_hist:
            return sb_ref[i, :, j * LANES:(j + 1) * LANES]
        return slabs_ref[j, (i - n_hist) * n_seq:(i - n_hist + 1) * n_seq, :]

    conv_b = []
    for j in range(n_lane):
        lanes = slice(j * LANES, (j + 1) * LANES)
        accs = [None] * seq_len
        for i in range(n_hist + seq_len):
            xi = slab(j, i)
            for t in range(seq_len):
                k = i - t
                if 0 <= k < K_B:
                    term = wb_ref[k, :, lanes] * xi
                    accs[t] = term if accs[t] is None else accs[t] + term
        for t in range(seq_len):
            rows_ref[j, pl.ds(t, n_seq, stride=seq_len), :] = accs[t]
        conv_b.append(rows_ref[j])

    cv = in_proj(1) * in_proj(2)
    b_gate = in_proj(0)
    conv_a = []
    for j in range(n_lane):
        lanes = slice(j * LANES, (j + 1) * LANES)
        for s in range(n_seq):
            rows = slice(s * seq_len, (s + 1) * seq_len)
            exta_ref[j, s * ra + off_a:s * ra + off_a + K_A - 1, :] = sa_ref[s, :, lanes]
            exta_ref[j, (s + 1) * ra - seq_len:(s + 1) * ra, :] = cv[rows, lanes]
        a = []
        for s0 in range(0, n_seq, seq_group):
            group = range(s0, s0 + seq_group)
            a += _conv_windows(exta_ref, j, [s * ra + off_a for s in group], seq_len, wa_ref, K_A)
        conv_a.append(jnp.concatenate(a, axis=0))

    for j in range(n_lane):
        lanes = slice(j * LANES, (j + 1) * LANES)
        for s in range(n_seq):
            new_a_ref[s, :, lanes] = exta_ref[j, (s + 1) * ra - (K_A - 1):(s + 1) * ra, :]
        for r in range(n_hist):
            new_b_ref[r, :, lanes] = slab(j, seq_len + r)
    return _mix(b_gate, jnp.concatenate(conv_a, axis=1), jnp.concatenate(conv_b, axis=1),
                bias_ref, ln_g_ref, ln_b_ref)


def _sample_kernel(x_ref, sa_ref, sb_ref, g_mix_ref, wa_ref, wb_ref, bias_ref, ln_g_ref,
                   ln_b_ref, g_ffn_ref, g_final_ref, w_in_hbm, w_out_hbm, w_gate_hbm, w_up_hbm,
                   w_down_hbm, y_ref, new_a_ref, new_b_ref, exta_ref, rows_ref, slabs_ref,
                   w_in_ref, w_out_ref, w_gate_ref, w_up_ref, w_down_ref, sem_ref,
                   *, ff_chunk, **mixer):
    first = pl.program_id(0) == 0
    loads = [pltpu.make_async_copy(src, dst, sem_ref.at[k]) for k, (src, dst) in enumerate(
        [(w_in_hbm, w_in_ref), (w_out_hbm, w_out_ref), (w_gate_hbm, w_gate_ref),
         (w_up_hbm, w_up_ref), (w_down_hbm, w_down_ref)])]

    @pl.when(first)
    def _():
        loads[0].start()
        loads[0].wait()
        for load in loads[1:]:
            load.start()

    y_cat = _sample_mixer(x_ref, sa_ref, sb_ref, g_mix_ref, w_in_ref, wa_ref, wb_ref, bias_ref,
                          ln_g_ref, ln_b_ref, new_a_ref, new_b_ref, exta_ref, rows_ref,
                          slabs_ref, **mixer)

    @pl.when(first)
    def _():
        for load in loads[1:]:
            load.wait()

    x1 = x_ref[...] + _dot(y_cat, w_out_ref[...])
    h2 = _rmsnorm(x1, g_ffn_ref[...]).astype(BF16)
    x2 = _ffn(x1, h2, w_gate_ref, w_up_ref, w_down_ref, ff_chunk)
    y_ref[...] = _rmsnorm(x2, g_final_ref[...])


N_PARAMS = 13
WEIGHT_SLOTS = (1, 7, 9, 10, 11)
CONVERT_ORDER = (0, 2, 3, 4, 1)
WEIGHT_CHUNK_ROWS = 256
WEIGHT_RING = 4


def _weight_pieces(w_hbm_refs, w_vmem_refs, chunk):
    return [(w_hbm_refs[k], w_vmem_refs[k], r0) for k in CONVERT_ORDER
            for r0 in range(0, w_hbm_refs[k].shape[0], chunk)]


def _convert_weights(pieces, first, last, stage_ref, sem_ref):
    ring, chunk = stage_ref.shape[0], stage_ref.shape[1]

    def fetch(p):
        src, _, r0 = pieces[p]
        return pltpu.make_async_copy(src.at[r0:r0 + chunk, :],
                                     stage_ref.at[p % ring, :, 0:src.shape[1]],
                                     sem_ref.at[p % ring])

    if first == 0:
        for p in range(min(ring - 1, len(pieces))):
            fetch(p).start()
    for p in range(first, last):
        src, dst, r0 = pieces[p]
        if p + ring - 1 < len(pieces):
            fetch(p + ring - 1).start()
        fetch(p).wait()
        dst[r0:r0 + chunk, :] = stage_ref[p % ring, :, 0:src.shape[1]].astype(BF16)


def _prompt_entry(*refs, n_steps, step):
    n_w = len(WEIGHT_SLOTS)
    x1_ref, x3_ref = refs[:2]
    params = list(refs[2:2 + N_PARAMS - n_w])
    w_hbm = refs[2 + N_PARAMS - n_w:2 + N_PARAMS]
    outs = refs[2 + N_PARAMS:5 + N_PARAMS]
    w_out_hbm = refs[5 + N_PARAMS:5 + N_PARAMS + n_w]
    scratch = refs[5 + N_PARAMS + n_w:]
    exta_ref, extb_ref, bg_ref, ycat_ref = scratch[:4]
    w_vmem = scratch[4:4 + n_w]
    stage_ref, sem_ref, out_sem_ref = scratch[4 + n_w:]
    g = pl.program_id(0)

    def copy_out(k):
        return pltpu.make_async_copy(w_vmem[k], w_out_hbm[k], out_sem_ref.at[k])

    pieces = _weight_pieces(w_hbm, w_vmem, stage_ref.shape[1])
    n_first = w_hbm[CONVERT_ORDER[0]].shape[0] // stage_ref.shape[1]
    n_second = n_first + stage_ref.shape[0] - 1
    convert = lambda first, last: _convert_weights(pieces, first, last, stage_ref, sem_ref)

    for slot, w in zip(WEIGHT_SLOTS, w_vmem):
        params.insert(slot, w)
    y_ref, new_a_ref, new_b_ref = outs
    tile = step["tile"]

    def run(sub, **stages):
        rows = pl.ds(sub * tile, tile)
        _prompt_kernel(x1_ref.at[rows], x3_ref.at[rows], *params, y_ref.at[rows], new_a_ref,
                       new_b_ref, exta_ref, extb_ref, bg_ref, ycat_ref, t=SUBSTEPS * g + sub,
                       **step, **stages)

    @pl.when(g == 0)
    def _():
        exta_ref[...] = jnp.zeros(exta_ref.shape, F32)
        extb_ref[...] = jnp.zeros(extb_ref.shape, F32)
        bg_ref[...] = jnp.zeros(bg_ref.shape, F32)
        convert(0, n_first)
        run(0, stage3=False)
        convert(n_first, n_second)
        run(1, stage3=False)

    @pl.when(g == 1)
    def _():
        convert(n_second, len(pieces))
        for k in range(n_w):
            copy_out(k).start()

    @pl.when(jnp.logical_and(g > 0, g < n_steps - 1))
    def _():
        run(0)
        run(1)

    @pl.when(g == n_steps - 1)
    def _():
        run(0, stage1=False)
        run(1, stage1=False)
        for k in range(n_w):
            copy_out(k).wait()


def _round_up(n, m):
    return -(-n // m) * m


def _const_spec(shape):
    zeros = (0,) * len(shape)
    return pl.BlockSpec(shape, lambda g: zeros, pipeline_mode=pl.Buffered(1))


PROMPT_TILE = 256
SUBSTEPS = 2
PROMPT_ROW_CHUNK = 64
CONV_UNIT_BEFORE_FFN_CHUNK = {1: 0, 2: 1, 3: 2, 4: 3}
FF_CHUNK = 512
SAMPLE_SEQS = 32
SAMPLE_SEQ_GROUP = 16
VMEM_LIMIT_BYTES = 56 * 1024 * 1024


def kernel(x_prompt, x_sample, state_conv_a, state_conv_b, g_mix, w_in, conv_a_w, conv_b_w,
           conv_b_bias, ln_b_g, ln_b_b, w_out, g_ffn, w_gate, w_up, w_down, g_final):
    depth = g_mix.shape[0]
    assert depth == 1, "single-layer trunk"
    bp, seq, d = x_prompt.shape
    bs, dec_seq, _ = x_sample.shape
    gw = conv_a_w.shape[-1]
    assert conv_a_w.shape[1] == K_A and conv_b_w.shape[1] == K_B
    assert dec_seq == SUBLANES and gw % LANES == 0
    assert seq % PROMPT_TILE == 0 and bs % SAMPLE_SEQS == 0

    row = lambda a: a.reshape(1, -1)
    taps = lambda w: jnp.swapaxes(w, 0, 1)
    small = (row(g_mix[0]), taps(conv_a_w), taps(conv_b_w), row(conv_b_bias[0]), row(ln_b_g[0]),
             row(ln_b_b[0]), row(g_ffn[0]), row(g_final))
    weights = (w_in[0], w_out[0], w_gate[0], w_up[0], w_down[0])
    assert len(small) + len(weights) == N_PARAMS
    assert all(w.shape[0] % WEIGHT_CHUNK_ROWS == 0 for w in weights)

    tile = PROMPT_TILE
    tiles_per_seq = seq // tile
    assert tiles_per_seq % SUBSTEPS == 0
    n_blocks = bp * tiles_per_seq // SUBSTEPS
    n_steps = n_blocks + 1
    block = SUBSTEPS * tile
    hist_a = _round_up(K_A - 1, SUBLANES)
    hist_b = _round_up(K_B - 1, SUBLANES)
    n_lane = gw // LANES
    tile_1 = lambda g: jnp.minimum(g, n_blocks - 1)
    tile_3 = lambda g: jnp.maximum(g - 1, 0)
    tiles_per_seq_blocks = tiles_per_seq // SUBSTEPS
    any_spec = pl.BlockSpec(memory_space=pl.ANY)
    x_rows = x_prompt.reshape(bp * seq, d)
    y_prompt, pa, pb, *weights_bf16 = pl.pallas_call(
        functools.partial(
            _prompt_entry, n_steps=n_steps,
            step=dict(tile=tile, row_chunk=PROMPT_ROW_CHUNK, tiles_per_seq=tiles_per_seq,
                      ff_chunk=FF_CHUNK)),
        grid=(n_steps,),
        in_specs=[pl.BlockSpec((block, d), lambda g: (tile_1(g), 0)),
                  pl.BlockSpec((block, d), lambda g: (tile_3(g), 0))]
        + [_const_spec(p.shape) for p in small] + [any_spec] * len(weights),
        out_specs=[pl.BlockSpec((block, d), lambda g: (tile_3(g), 0)),
                   pl.BlockSpec((None, K_A - 1, gw),
                                lambda g: (tile_1(g) // tiles_per_seq_blocks, 0, 0)),
                   pl.BlockSpec((None, K_B - 1, gw),
                                lambda g: (tile_1(g) // tiles_per_seq_blocks, 0, 0))]
        + [any_spec] * len(weights),
        out_shape=[jax.ShapeDtypeStruct((bp * seq, d), F32),
                   jax.ShapeDtypeStruct((bp, K_A - 1, gw), F32),
                   jax.ShapeDtypeStruct((bp, K_B - 1, gw), F32)]
        + [jax.ShapeDtypeStruct(w.shape, BF16) for w in weights],
        scratch_shapes=[pltpu.VMEM((n_lane, hist_a + tile, LANES), F32),
                        pltpu.VMEM((n_lane, hist_b + tile, LANES), F32),
                        pltpu.VMEM((tile, gw), F32),
                        pltpu.VMEM((tile, 2 * gw), BF16)]
        + [pltpu.VMEM(w.shape, BF16) for w in weights]
        + [pltpu.VMEM((WEIGHT_RING, WEIGHT_CHUNK_ROWS, max(w.shape[1] for w in weights)), F32),
           pltpu.SemaphoreType.DMA((WEIGHT_RING,)),
           pltpu.SemaphoreType.DMA((len(weights),))],
        compiler_params=pltpu.CompilerParams(
            dimension_semantics=("arbitrary",),
            vmem_limit_bytes=VMEM_LIMIT_BYTES),
        name="prompt_layer",
    )(x_rows, x_rows, *small, *weights)
    y_prompt = y_prompt.reshape(bp, seq, d)

    by_row = lambda a: jnp.swapaxes(a, 0, 1)
    n_seq = SAMPLE_SEQS
    rows = n_seq * dec_seq
    ra = _round_up(K_A - 1, SUBLANES) + dec_seq
    y_sample, sa, sb = pl.pallas_call(
        functools.partial(_sample_kernel, n_seq=n_seq, seq_len=dec_seq,
                          seq_group=SAMPLE_SEQ_GROUP, ff_chunk=FF_CHUNK),
        grid=(bs // n_seq,),
        in_specs=[pl.BlockSpec((rows, d), lambda i: (i, 0)),
                  pl.BlockSpec((n_seq, K_A - 1, gw), lambda i: (i, 0, 0)),
                  pl.BlockSpec((K_B - 1, n_seq, gw), lambda i: (0, i, 0))]
        + [_const_spec(p.shape) for p in small]
        + [pl.BlockSpec(memory_space=pl.ANY) for _ in weights_bf16],
        out_specs=[pl.BlockSpec((rows, d), lambda i: (i, 0)),
                   pl.BlockSpec((n_seq, K_A - 1, gw), lambda i: (i, 0, 0)),
                   pl.BlockSpec((K_B - 1, n_seq, gw), lambda i: (0, i, 0))],
        out_shape=[jax.ShapeDtypeStruct((bs * dec_seq, d), F32),
                   jax.ShapeDtypeStruct((bs, K_A - 1, gw), F32),
                   jax.ShapeDtypeStruct((K_B - 1, bs, gw), F32)],
        scratch_shapes=[pltpu.VMEM((n_lane, n_seq * ra, LANES), F32),
                        pltpu.VMEM((n_lane, rows, LANES), F32),
                        pltpu.VMEM((n_lane, rows, LANES), F32)]
        + [pltpu.VMEM(w.shape, BF16) for w in weights_bf16]
        + [pltpu.SemaphoreType.DMA((len(weights_bf16),))],
        compiler_params=pltpu.CompilerParams(
            dimension_semantics=("arbitrary",),
            vmem_limit_bytes=VMEM_LIMIT_BYTES),
        name="sample_layer",
    )(x_sample.reshape(bs * dec_seq, d), state_conv_a[0], by_row(state_conv_b[0]),
      *small, *weights_bf16)

    return (y_prompt, y_sample.reshape(bs, dec_seq, d), pa[None], pb[None],
            sa[None], by_row(sb)[None])
```
